```python
import jax, jax.numpy as jnp
from jax import lax
import numpy as np

D_MODEL = 2048
BATCH = 16
SEQ = 256
DEPTH = 4
DEC_BATCH = 2
DEC_SEQ = 2048
PAST_LEN = 256

GRID_W = 64
N_EVEN = (DEPTH + 1) // 2
N_ODD = DEPTH // 2
A_WIDTH = D_MODEL // 2
A_HEAD_DIM = 128
A_HEADS = A_WIDTH // A_HEAD_DIM
B_WIDTH = D_MODEL - A_WIDTH
B_HEADS = 4
B_HEAD_DIM = B_WIDTH // B_HEADS
ATTN_HEAD_DIM = 128
ATTN_HEADS = D_MODEL // ATTN_HEAD_DIM
ATTN_KV_HEADS = 4
FFN_DIM = ((8 * D_MODEL + 3 * 256 - 1) // (3 * 256)) * 256
EVEN_IN_DIM = 5 * A_WIDTH + 4 * B_WIDTH
ATTN_IN_DIM = (ATTN_HEADS + 2 * ATTN_KV_HEADS) * ATTN_HEAD_DIM
GLA_CHUNK = 32
RET_CHUNK = 128
Q_BLOCK = 128
ROPE_THETA = 10000.0
EPS = 1e-6

kernel_name = 'hybrid_hgrn2_retention_gqa_diffusion_step'

F32 = jnp.float32


def rms_norm(x, g):
    xf = x.astype(F32)
    y = xf * lax.rsqrt(jnp.mean(xf * xf, axis=-1, keepdims=True) + EPS)
    return (y * g.astype(F32)).astype(x.dtype)


def head_layer_norm(x, g):
    mu = jnp.mean(x, axis=-1, keepdims=True)
    xc = x - mu
    y = xc * lax.rsqrt(jnp.mean(xc * xc, axis=-1, keepdims=True) + EPS)
    return y.reshape(x.shape[0], x.shape[1], -1) * g.astype(F32)


def grid_rope(n_tokens, head_dim):
    rows = n_tokens // GRID_W
    row = jnp.broadcast_to(jnp.arange(rows)[:, None], (rows, GRID_W)).reshape(-1).astype(F32)
    col = jnp.broadcast_to(jnp.arange(GRID_W)[None, :], (rows, GRID_W)).reshape(-1).astype(F32)
    per_axis = head_dim // 4
    freqs = ROPE_THETA ** (-jnp.arange(per_axis, dtype=F32) / per_axis)
    ang = jnp.concatenate([row[:, None] * freqs, col[:, None] * freqs], axis=-1)
    return jnp.cos(ang), jnp.sin(ang)


def apply_rope(x, rope):
    cos, sin = rope
    half = x.shape[-1] // 2
    xf = x.astype(F32)
    x1, x2 = xf[..., :half], xf[..., half:]
    cos = cos[None, :, None, :]
    sin = sin[None, :, None, :]
    return jnp.concatenate([x1 * cos - x2 * sin, x1 * sin + x2 * cos], axis=-1).astype(x.dtype)


def to_chunks(t, c):
    b, l, h, d = t.shape
    return t.reshape(b, l // c, c, h, d).transpose(1, 0, 3, 2, 4)


def from_chunks(t):
    nc, b, h, c, d = t.shape
    return t.transpose(1, 0, 3, 2, 4).reshape(b, nc * c, h, d)


def gla_chunk_scan(q, k, v, logf, s0):
    c = GLA_CHUNK
    tri = jnp.tril(jnp.ones((c, c), bool))[:, :, None]

    def step(s, inp):
        qc, kc, vc, gc = inp
        g_cum = jnp.cumsum(gc, axis=2)
        diff = g_cum[:, :, :, None, :] - g_cum[:, :, None, :, :]
        dec = jnp.where(tri, jnp.exp(jnp.where(tri, diff, 0.0)), 0.0)
        att = jnp.einsum('bhid,bhjd,bhijd->bhij', qc, kc, dec)
        o = jnp.einsum('bhij,bhjv->bhiv', att, vc) + jnp.einsum('bhid,bhdv->bhiv', qc * jnp.exp(g_cum), s)
        g_end = g_cum[:, :, -1:, :]
        s = jnp.exp(g_end[:, :, 0, :])[..., None] * s + jnp.einsum('bhjd,bhjv->bhdv', kc * jnp.exp(g_end - g_cum), vc)
        return s, o

    ch = lambda t: to_chunks(t.astype(F32), c)
    s_fin, o = lax.scan(step, s0.astype(F32), (ch(q), ch(k), ch(v), ch(logf)))
    return from_chunks(o), s_fin


def retention_chunk_scan(q, k, v, log_gamma, s0):
    c = RET_CHUNK
    idx = jnp.arange(c, dtype=F32)
    tri = idx[:, None] >= idx[None, :]
    intra = jnp.where(tri, jnp.exp(jnp.where(tri, idx[:, None] - idx[None, :], 0.0)[None] * log_gamma[:, None, None]), 0.0)
    q_dec = jnp.exp((idx + 1.0)[None, :] * log_gamma[:, None])[..., None]
    k_dec = jnp.exp((c - 1.0 - idx)[None, :] * log_gamma[:, None])[..., None]
    chunk_dec = jnp.exp(c * log_gamma)[:, None, None]

    def step(r, inp):
        qc, kc, vc = inp
        att = jnp.einsum('bhid,bhjd->bhij', qc, kc) * intra
        o = jnp.einsum('bhij,bhjv->bhiv', att, vc) + jnp.einsum('bhid,bhdv->bhiv', qc * q_dec, r)
        r = chunk_dec * r + jnp.einsum('bhjd,bhjv->bhdv', kc * k_dec, vc)
        return r, o

    ch = lambda t: to_chunks(t.astype(F32), c)
    r_fin, o = lax.scan(step, s0.astype(F32), (ch(q), ch(k), ch(v)))
    return from_chunks(o), r_fin


def block_attention(q, k, v):
    b, lq, h, hd = q.shape
    kvh = k.shape[2]
    grp = h // kvh
    nb = lq // Q_BLOCK
    qb = q.reshape(b, nb, Q_BLOCK, kvh, grp, hd).transpose(1, 0, 2, 3, 4, 5).astype(F32)
    kf = k.astype(F32)
    vf = v.astype(F32)
    scale = hd ** -0.5

    def one_block(qblk):
        s = jnp.einsum('bqkgd,bskd->bkgqs', qblk, kf) * scale
        p = jax.nn.softmax(s, axis=-1)
        return jnp.einsum('bkgqs,bskd->bqkgd', p, vf)

    o = lax.map(one_block, qb)
    return o.transpose(1, 0, 2, 3, 4, 5).reshape(b, lq, h, hd).astype(q.dtype)


def adaln(cvec, w, b):
    m = jax.nn.silu(cvec) @ w + b
    return jnp.split(m[:, None, :], 6, axis=-1)


def modulate(x, g, shift, scale):
    return rms_norm(x, g) * (1.0 + scale) + shift


def even_split_points():
    sizes = [A_WIDTH] * 5 + [B_WIDTH] * 4
    return [int(s) for s in np.cumsum(sizes)[:-1]]


def even_mixer(h, w_in, w_out, lb, onorm, decay_logit, gnorm, rope, s_hgrn0, s_ret0):
    bn, l, _ = h.shape
    q_a, f_fw, f_bw, i_a, g_a, q_b, k_b, v_b, g_b = jnp.split(h @ w_in, even_split_points(), axis=-1)
    heads = lambda t, n: t.reshape(bn, l, n, -1)
    flip = lambda t: jnp.flip(t, axis=1)
    qa = heads(jax.nn.silu(q_a), A_HEADS) * (A_HEAD_DIM ** -0.5)
    ia = heads(i_a, A_HEADS)
    log_lb = jnp.log(lb)
    log_1mlb = jnp.log1p(-lb)

    def gates(f):
        logf = jnp.logaddexp(log_lb, log_1mlb + jax.nn.log_sigmoid(f.astype(F32)))
        return heads(-jnp.expm1(logf), A_HEADS), heads(logf, A_HEADS)

    k_fw, lf_fw = gates(f_fw)
    k_bw, lf_bw = gates(f_bw)
    oa_fw, sa_fw = gla_chunk_scan(qa, k_fw, ia, lf_fw, s_hgrn0[:, 0])
    oa_bw, sa_bw = gla_chunk_scan(flip(qa), flip(k_bw), flip(ia), flip(lf_bw), s_hgrn0[:, 1])
    oa = rms_norm(oa_fw + flip(oa_bw), onorm).reshape(bn, l, A_WIDTH) * jax.nn.silu(g_a.astype(F32))
    qb = heads(q_b, B_HEADS)
    kb = heads(k_b, B_HEADS) * (B_HEAD_DIM ** -0.5)
    vb = heads(v_b, B_HEADS)
    if rope is not None:
        qb = apply_rope(qb, rope)
        kb = apply_rope(kb, rope)
    log_gamma = jax.nn.log_sigmoid(decay_logit.astype(F32))
    ob_fw, sb_fw = retention_chunk_scan(qb, kb, vb, log_gamma[0], s_ret0[:, 0])
    ob_bw, sb_bw = retention_chunk_scan(flip(qb), flip(kb), flip(vb), log_gamma[1], s_ret0[:, 1])
    ob = head_layer_norm(ob_fw + flip(ob_bw), gnorm) * jax.nn.silu(g_b.astype(F32))
    y = jnp.concatenate([oa, ob], axis=-1).astype(h.dtype) @ w_out
    new_hgrn = jnp.stack([sa_fw, sa_bw], axis=1).astype(h.dtype)
    new_ret = jnp.stack([sb_fw, sb_bw], axis=1).astype(h.dtype)
    return y, new_hgrn, new_ret


def attn_qkv(h, w_in, qn, kn):
    bn, l, _ = h.shape
    q, k, v = jnp.split(h @ w_in, [ATTN_HEADS * ATTN_HEAD_DIM, (ATTN_HEADS + ATTN_KV_HEADS) * ATTN_HEAD_DIM], axis=-1)
    q = rms_norm(q.reshape(bn, l, ATTN_HEADS, ATTN_HEAD_DIM), qn)
    k = rms_norm(k.reshape(bn, l, ATTN_KV_HEADS, ATTN_HEAD_DIM), kn)
    v = v.reshape(bn, l, ATTN_KV_HEADS, ATTN_HEAD_DIM)
    return q, k, v


def swiglu(h, wg, wu, wd):
    return (jax.nn.silu(h @ wg) * (h @ wu)) @ wd


def setup_inputs(seed: int = 0) -> dict:
    key = jax.random.key(seed)
    ks = jax.random.split(key, 26)
    nrm = lambda k, shape, s: jax.random.normal(k, shape, F32) * s
    gamma = 1.0 - 2.0 ** (-5.0 - jnp.arange(B_HEADS, dtype=F32))
    gamma_logit = jnp.log(gamma) - jnp.log1p(-gamma)
    return {
        'x_prompt': nrm(ks[0], (BATCH, SEQ, D_MODEL), 1.0),
        'x_sample': nrm(ks[1], (DEC_BATCH, DEC_SEQ, D_MODEL), 1.0),
        'c': nrm(ks[2], (DEC_BATCH, D_MODEL), 1.0),
        'state_hgrn': nrm(ks[3], (DEC_BATCH, N_EVEN, 2, A_HEADS, A_HEAD_DIM, A_HEAD_DIM), 0.5),
        'state_ret': nrm(ks[4], (DEC_BATCH, N_EVEN, 2, B_HEADS, B_HEAD_DIM, B_HEAD_DIM), 0.5),
        'cache_k': nrm(ks[5], (DEC_BATCH, N_ODD, PAST_LEN, ATTN_KV_HEADS, ATTN_HEAD_DIM), 1.0),
        'cache_v': nrm(ks[6], (DEC_BATCH, N_ODD, PAST_LEN, ATTN_KV_HEADS, ATTN_HEAD_DIM), 1.0),
        'c_ctx': nrm(ks[7], (D_MODEL,), 1.0),
        'w_mod': nrm(ks[8], (DEPTH, D_MODEL, 6 * D_MODEL), 0.5 * D_MODEL ** -0.5),
        'b_mod': nrm(ks[9], (DEPTH, 6 * D_MODEL), 0.02),
        'norm_mix': 1.0 + nrm(ks[10], (DEPTH, D_MODEL), 0.02),
        'norm_ffn': 1.0 + nrm(ks[11], (DEPTH, D_MODEL), 0.02),
        'w_in_even': nrm(ks[12], (N_EVEN, D_MODEL, EVEN_IN_DIM), D_MODEL ** -0.5),
        'w_out_even': nrm(ks[13], (N_EVEN, A_WIDTH + B_WIDTH, D_MODEL), (A_WIDTH + B_WIDTH) ** -0.5),
        'hgrn_lb_logits': nrm(ks[14], (DEPTH, A_WIDTH), 0.1),
        'hgrn_onorm': 1.0 + nrm(ks[15], (N_EVEN, A_HEAD_DIM), 0.02),
        'ret_decay_logit': gamma_logit[None, None, :] + nrm(ks[16], (N_EVEN, 2, B_HEADS), 0.1),
        'ret_gnorm': 1.0 + nrm(ks[17], (N_EVEN, B_WIDTH), 0.02),
        'w_in_attn': nrm(ks[18], (N_ODD, D_MODEL, ATTN_IN_DIM), D_MODEL ** -0.5),
        'w_out_attn': nrm(ks[19], (N_ODD, ATTN_HEADS * ATTN_HEAD_DIM, D_MODEL), (ATTN_HEADS * ATTN_HEAD_DIM) ** -0.5),
        'q_norm': 1.0 + nrm(ks[20], (N_ODD, ATTN_HEAD_DIM), 0.02),
        'k_norm': 1.0 + nrm(ks[21], (N_ODD, ATTN_HEAD_DIM), 0.02),
        'w_ffn_gate': nrm(ks[22], (DEPTH, D_MODEL, FFN_DIM), D_MODEL ** -0.5),
        'w_ffn_up': nrm(ks[23], (DEPTH, D_MODEL, FFN_DIM), D_MODEL ** -0.5),
        'w_ffn_down': nrm(ks[24], (DEPTH, FFN_DIM, D_MODEL), FFN_DIM ** -0.5),
    }


def reference(x_prompt, x_sample, c, state_hgrn, state_ret, cache_k, cache_v, c_ctx, w_mod, b_mod, norm_mix, norm_ffn, w_in_even, w_out_even, hgrn_lb_logits, hgrn_onorm, ret_decay_logit, ret_gnorm, w_in_attn, w_out_attn, q_norm, k_norm, w_ffn_gate, w_ffn_up, w_ffn_down):
    n_lat = x_sample.shape[1]
    rope_attn = grid_rope(n_lat, ATTN_HEAD_DIM)
    rope_ret = grid_rope(n_lat, B_HEAD_DIM)
    lb_cum = jnp.cumsum(jax.nn.softmax(hgrn_lb_logits.astype(F32), axis=0), axis=0)
    lower_bounds = lb_cum - lb_cum[0]
    xp, xs = x_prompt, x_sample
    zero_hgrn = jnp.zeros((xp.shape[0], 2, A_HEADS, A_HEAD_DIM, A_HEAD_DIM), xp.dtype)
    zero_ret = jnp.zeros((xp.shape[0], 2, B_HEADS, B_HEAD_DIM, B_HEAD_DIM), xp.dtype)
    new_hgrn, new_ret, new_k, new_v = [], [], [], []
    for l in range(DEPTH):
        mp = adaln(c_ctx[None, :], w_mod[l], b_mod[l])
        ms = adaln(c, w_mod[l], b_mod[l])
        hp = modulate(xp, norm_mix[l], mp[0], mp[1])
        hs = modulate(xs, norm_mix[l], ms[0], ms[1])
        if l % 2 == 0:
            e = l // 2
            args = (w_in_even[e], w_out_even[e], lower_bounds[l], hgrn_onorm[e], ret_decay_logit[e], ret_gnorm[e])
            yp, sh, sr = even_mixer(hp, *args, None, zero_hgrn, zero_ret)
            ys, _, _ = even_mixer(hs, *args, rope_ret, state_hgrn[:, e], state_ret[:, e])
            new_hgrn.append(sh)
            new_ret.append(sr)
        else:
            o = l // 2
            qp, kp, vp = attn_qkv(hp, w_in_attn[o], q_norm[o], k_norm[o])
            yp = block_attention(qp, kp, vp).reshape(xp.shape) @ w_out_attn[o]
            qs, kl, vl = attn_qkv(hs, w_in_attn[o], q_norm[o], k_norm[o])
            qs = apply_rope(qs, rope_attn)
            kl = apply_rope(kl, rope_attn)
            k_all = jnp.concatenate([cache_k[:, o], kl], axis=1)
            v_all = jnp.concatenate([cache_v[:, o], vl], axis=1)
            ys = block_attention(qs, k_all, v_all).reshape(xs.shape) @ w_out_attn[o]
            new_k.append(kp)
            new_v.append(vp)
        xp = xp + mp[2] * yp
        xs = xs + ms[2] * ys
        hp = modulate(xp, norm_ffn[l], mp[3], mp[4])
        hs = modulate(xs, norm_ffn[l], ms[3], ms[4])
        xp = xp + mp[5] * swiglu(hp, w_ffn_gate[l], w_ffn_up[l], w_ffn_down[l])
        xs = xs + ms[5] * swiglu(hs, w_ffn_gate[l], w_ffn_up[l], w_ffn_down[l])
    out_hgrn = jnp.stack(new_hgrn, axis=1)
    out_ret = jnp.stack(new_ret, axis=1)
    out_k = jnp.stack(new_k, axis=1)
    out_v = jnp.stack(new_v, axis=1)
    return (xp, xs, out_hgrn, out_ret, out_k, out_v)
```

```python
import functools

import jax
import jax.numpy as jnp
from jax import lax
from jax.experimental import pallas as pl
from jax.experimental.pallas import tpu as pltpu

F32 = jnp.float32
BF16 = jnp.bfloat16

EPS = 1e-6
ROPE_THETA = 10000.0
GRID_W = 64
A_HEAD_DIM = 128
B_HEADS = 4
ATTN_HEAD_DIM = 128
ATTN_KV_HEADS = 4
LANES = 128
SUBLANES = 8
VMEM_LIMIT = 56 * 1024 * 1024
NEG_BIG = -1e30

NT_DIMS = (((1,), (1,)), ((), ()))
TN_DIMS = (((0,), (0,)), ((), ()))


def _params(*sem):
    return pltpu.CompilerParams(dimension_semantics=sem, vmem_limit_bytes=VMEM_LIMIT)


def _silu(x):
    return x / (1.0 + jnp.exp(-x))


def _log_sigmoid(x):
    return jnp.minimum(x, 0.0) - jnp.log1p(jnp.exp(-jnp.abs(x)))


def _adaln_kernel(c_ref, w_ref, b_ref, o_ref):
    s = _silu(c_ref[...]).astype(BF16)
    o_ref[...] = jnp.dot(s, w_ref[...].astype(BF16), preferred_element_type=F32) + b_ref[...]


def _adaln_call(cond, w_mod, b_mod, tn=1024):
    depth, d, n = w_mod.shape
    rows = cond.shape[0]
    return pl.pallas_call(
        _adaln_kernel,
        grid=(depth, n // tn),
        in_specs=[
            pl.BlockSpec((rows, d), lambda l, j: (0, 0)),
            pl.BlockSpec((None, d, tn), lambda l, j: (l, 0, j)),
            pl.BlockSpec((None, 1, tn), lambda l, j: (l, 0, j)),
        ],
        out_specs=pl.BlockSpec((None, rows, tn), lambda l, j: (l, 0, j)),
        out_shape=jax.ShapeDtypeStruct((depth, rows, n), F32),
        compiler_params=_params("arbitrary", "arbitrary"),
    )(cond, w_mod, b_mod.reshape(depth, 1, n))


def _modulate_kernel(x_ref, g_ref, shift_ref, scale_ref, o_ref):
    x = x_ref[...]
    y = x * lax.rsqrt(jnp.mean(x * x, axis=-1, keepdims=True) + EPS) * g_ref[...]
    o_ref[...] = (y * (1.0 + scale_ref[...]) + shift_ref[...]).astype(BF16)


def _modulate_call(x, gains, mods, layer, shift_idx, group_of_row, tr=256):
    m, d = x.shape
    depth = gains.shape[0]
    grp = lambda i: group_of_row(i * tr)
    return pl.pallas_call(
        _modulate_kernel,
        grid=(m // tr,),
        in_specs=[
            pl.BlockSpec((tr, d), lambda i: (i, 0)),
            pl.BlockSpec((None, 1, d), lambda i: (layer, 0, 0)),
            pl.BlockSpec((None, None, None, 1, d), lambda i: (layer, grp(i), shift_idx, 0, 0)),
            pl.BlockSpec((None, None, None, 1, d), lambda i: (layer, grp(i), shift_idx + 1, 0, 0)),
        ],
        out_specs=pl.BlockSpec((tr, d), lambda i: (i, 0)),
        out_shape=jax.ShapeDtypeStruct((m, d), BF16),
        compiler_params=_params("arbitrary"),
    )(x, gains.reshape(depth, 1, d), mods, mods)


def _cast_weights_once(w_refs, wbf_refs):
    @pl.when(pl.program_id(1) == 0)
    def _():
        for w_ref, wbf_ref in zip(w_refs, wbf_refs):
            wbf_ref[...] = w_ref[...].astype(BF16)


def _mm_plain_kernel(a_ref, w_ref, o_ref, wbf_ref):
    _cast_weights_once((w_ref,), (wbf_ref,))
    o_ref[...] = jnp.dot(a_ref[...], wbf_ref[...], preferred_element_type=F32).astype(o_ref.dtype)


def _mm_swiglu_kernel(a_ref, wg_ref, wu_ref, o_ref, wgbf_ref, wubf_ref):
    _cast_weights_once((wg_ref, wu_ref), (wgbf_ref, wubf_ref))
    a = a_ref[...]
    g = jnp.dot(a, wgbf_ref[...], preferred_element_type=F32)
    u = jnp.dot(a, wubf_ref[...], preferred_element_type=F32)
    o_ref[...] = (_silu(g) * u).astype(o_ref.dtype)


def _mm_residual_kernel(a_ref, w_ref, x_ref, gate_ref, o_ref, wbf_ref):
    _cast_weights_once((w_ref,), (wbf_ref,))
    y = jnp.dot(a_ref[...], wbf_ref[...], preferred_element_type=F32)
    o_ref[...] = x_ref[...] + gate_ref[...] * y


def _mm_plain_call(a, w, widx, out_dtype, tm, tn):
    m, k = a.shape
    n = w.shape[-1]
    return pl.pallas_call(
        _mm_plain_kernel,
        grid=(n // tn, m // tm),
        in_specs=[
            pl.BlockSpec((tm, k), lambda j, i: (i, 0)),
            pl.BlockSpec((None, k, tn), lambda j, i: (widx, 0, j)),
        ],
        out_specs=pl.BlockSpec((tm, tn), lambda j, i: (i, j)),
        out_shape=jax.ShapeDtypeStruct((m, n), out_dtype),
        scratch_shapes=[pltpu.VMEM((k, tn), BF16)],
        compiler_params=_params("arbitrary", "arbitrary"),
    )(a, w)


def _mm_swiglu_call(a, wg, wu, widx, tm, tn):
    m, k = a.shape
    n = wg.shape[-1]
    wspec = pl.BlockSpec((None, k, tn), lambda j, i: (widx, 0, j))
    return pl.pallas_call(
        _mm_swiglu_kernel,
        grid=(n // tn, m // tm),
        in_specs=[pl.BlockSpec((tm, k), lambda j, i: (i, 0)), wspec, wspec],
        out_specs=pl.BlockSpec((tm, tn), lambda j, i: (i, j)),
        out_shape=jax.ShapeDtypeStruct((m, n), BF16),
        scratch_shapes=[pltpu.VMEM((k, tn), BF16), pltpu.VMEM((k, tn), BF16)],
        compiler_params=_params("arbitrary", "arbitrary"),
    )(a, wg, wu)


def _mm_residual_call(a, w, widx, x, mods, layer, gate_idx, group_of_row, tm, tn):
    m, k = a.shape
    n = w.shape[-1]
    grp = lambda i: group_of_row(i * tm)
    return pl.pallas_call(
        _mm_residual_kernel,
        grid=(n // tn, m // tm),
        in_specs=[
            pl.BlockSpec((tm, k), lambda j, i: (i, 0)),
            pl.BlockSpec((None, k, tn), lambda j, i: (widx, 0, j)),
            pl.BlockSpec((tm, tn), lambda j, i: (i, j)),
            pl.BlockSpec((None, None, None, 1, tn), lambda j, i: (layer, grp(i), gate_idx, 0, j)),
        ],
        out_specs=pl.BlockSpec((tm, tn), lambda j, i: (i, j)),
        out_shape=jax.ShapeDtypeStruct((m, n), F32),
        scratch_shapes=[pltpu.VMEM((k, tn), BF16)],
        compiler_params=_params("arbitrary", "arbitrary"),
    )(a, w, x, mods)


GLA_DIAG = SUBLANES


def _bcast_row_in_blocks(x, block, r):
    c, d = x.shape
    x3 = x.reshape(c // block, block, d)
    return jnp.broadcast_to(x3[:, r:r + 1, :], x3.shape).reshape(c, d)


def _gla_masks(c, fwd):
    i = lax.broadcasted_iota(jnp.int32, (c, c), 0)
    j = lax.broadcasted_iota(jnp.int32, (c, c), 1)
    masks = {}
    h = GLA_DIAG
    while h < c:
        same = (i // (2 * h)) == (j // (2 * h))
        i_hi = (i % (2 * h)) >= h
        j_hi = (j % (2 * h)) >= h
        if fwd:
            ok = jnp.where(same, jnp.where(i_hi, jnp.where(j_hi, 0.0, 1.0), 0.0), 0.0)
        else:
            ok = jnp.where(same, jnp.where(i_hi, 0.0, jnp.where(j_hi, 1.0, 0.0)), 0.0)
        masks[h] = ok.astype(F32)
        h *= 2
    return masks


def _gla_chunk(q, kg, v, logf, st, fwd, tri, masks, ones_bf, rloc):
    c, d = q.shape
    e = jnp.dot(tri, logf, precision=lax.Precision.HIGHEST, preferred_element_type=F32)
    vb = v.astype(BF16)
    att = None
    h = GLA_DIAG
    while h < c:
        e_ref = _bcast_row_in_blocks(e, 2 * h, h - 1 if fwd else h)
        qt = (q * jnp.exp(jnp.minimum(e - e_ref, 0.0))).astype(BF16)
        kt = (kg * jnp.exp(jnp.minimum(e_ref - e, 0.0))).astype(BF16)
        a = lax.dot_general(qt, kt, NT_DIMS, preferred_element_type=F32) * masks[h]
        att = a if att is None else att + a
        h *= 2
    o = jnp.dot(att.astype(BF16), vb, preferred_element_type=F32)
    ts = []
    for jl in range(GLA_DIAG):
        valid = (rloc >= jl) if fwd else (rloc <= jl)
        dec = jnp.exp(jnp.where(valid, e - _bcast_row_in_blocks(e, GLA_DIAG, jl), NEG_BIG))
        ts.append((dec * (q * _bcast_row_in_blocks(kg, GLA_DIAG, jl))).astype(BF16))
    sums = jnp.dot(jnp.concatenate(ts, axis=0), ones_bf, preferred_element_type=F32)
    for jl in range(GLA_DIAG):
        o = o + sums[jl * c:(jl + 1) * c] * _bcast_row_in_blocks(v, GLA_DIAG, jl)
    o = o + lax.dot_general((q * jnp.exp(e)).astype(BF16), st.astype(BF16), NT_DIMS,
                            preferred_element_type=F32)
    e_edge = e[c - 1:c] if fwd else e[0:1]
    kh = (kg * jnp.exp(e_edge - e)).astype(BF16)
    st_new = st * jnp.exp(e_edge) + lax.dot_general(vb, kh, TN_DIMS, preferred_element_type=F32)
    return o, st_new


def _gla_kernel(*refs, seq_len, chunk, has_s0, want_final):
    refs = list(refs)
    q_ref, ffw_ref, fbw_ref, v_ref, g_ref, lb_ref, on_ref = refs[:7]
    del refs[:7]
    s0_ref = refs.pop(0) if has_s0 else None
    o_ref = refs.pop(0)
    sf_ref = refs.pop(0) if want_final else None
    st_ref, oacc_ref = refs
    c = chunk
    nch = seq_len // c
    d = q_ref.shape[-1]

    lb = lb_ref[...]
    log_lb = jnp.log(lb)
    log_1mlb = jnp.log1p(-lb)
    i = lax.broadcasted_iota(jnp.int32, (c, c), 0)
    j = lax.broadcasted_iota(jnp.int32, (c, c), 1)
    tri_f = jnp.where(i >= j, 1.0, 0.0).astype(F32)
    tri_b = jnp.where(i <= j, 1.0, 0.0).astype(F32)
    masks_f = _gla_masks(c, True)
    masks_b = _gla_masks(c, False)
    ones_bf = jnp.ones((d, d), BF16)
    rloc = lax.broadcasted_iota(jnp.int32, (c, d), 0) % GLA_DIAG
    q_scale = d ** -0.5

    def gates(f):
        b = log_1mlb + _log_sigmoid(f)
        logf = jnp.maximum(log_lb, b) + jnp.log1p(jnp.exp(-jnp.abs(log_lb - b)))
        return logf, jnp.exp(b - f)

    def chunk_inputs(ci, f_ref):
        sl = pl.ds(pl.multiple_of(ci * c, c), c)
        logf, kg = gates(f_ref[sl, :])
        return sl, _silu(q_ref[sl, :]) * q_scale, kg, v_ref[sl, :], logf

    def fwd_body(ci, carry):
        sl, q, kg, v, logf = chunk_inputs(ci, ffw_ref)
        o, st = _gla_chunk(q, kg, v, logf, st_ref[...], True, tri_f, masks_f, ones_bf, rloc)
        st_ref[...] = st
        oacc_ref[sl, :] = o
        return carry

    def bwd_body(t, carry):
        sl, q, kg, v, logf = chunk_inputs(nch - 1 - t, fbw_ref)
        o, st = _gla_chunk(q, kg, v, logf, st_ref[...], False, tri_b, masks_b, ones_bf, rloc)
        st_ref[...] = st
        y = oacc_ref[sl, :] + o
        y = y * lax.rsqrt(jnp.mean(y * y, axis=-1, keepdims=True) + EPS) * on_ref[...]
        o_ref[sl, :] = (y * _silu(g_ref[sl, :])).astype(o_ref.dtype)
        return carry

    st_ref[...] = s0_ref[0].T if has_s0 else jnp.zeros((d, d), F32)
    lax.fori_loop(0, nch, fwd_body, 0)
    if want_final:
        sf_ref[0] = st_ref[...].T
    st_ref[...] = s0_ref[1].T if has_s0 else jnp.zeros((d, d), F32)
    lax.fori_loop(0, nch, bwd_body, 0)
    if want_final:
        sf_ref[1] = st_ref[...].T


def _gla_call(p, lower_bounds, layer, onorm, eidx, s0, n_seq, seq_len, row0, want_final, chunk=64):
    d = A_HEAD_DIM
    n_heads = lower_bounds.shape[-1] // d
    a_width = n_heads * d
    sb = row0 // seq_len
    col = lambda k: (lambda n, h: (n + sb, k * n_heads + h))
    slab = lambda k: pl.BlockSpec((seq_len, d), col(k))
    in_specs = [slab(0), slab(1), slab(2), slab(3), slab(4),
                pl.BlockSpec((None, 1, d), lambda n, h: (layer, 0, h)),
                pl.BlockSpec((None, 1, d), lambda n, h: (eidx, 0, 0))]
    args = [p, p, p, p, p, lower_bounds.reshape(lower_bounds.shape[0], 1, a_width),
            onorm.reshape(onorm.shape[0], 1, d)]
    if s0 is not None:
        in_specs.append(pl.BlockSpec((None, None, 2, None, d, d), lambda n, h: (n, eidx, 0, h, 0, 0)))
        args.append(s0)
    out_specs = [pl.BlockSpec((seq_len, d), lambda n, h: (n, h))]
    out_shape = [jax.ShapeDtypeStruct((n_seq * seq_len, a_width), BF16)]
    if want_final:
        out_specs.append(pl.BlockSpec((None, 2, None, d, d), lambda n, h: (n, 0, h, 0, 0)))
        out_shape.append(jax.ShapeDtypeStruct((n_seq, 2, n_heads, d, d), F32))
    kern = functools.partial(_gla_kernel, seq_len=seq_len, chunk=chunk,
                             has_s0=s0 is not None, want_final=want_final)
    return pl.pallas_call(
        kern,
        grid=(n_seq, n_heads),
        in_specs=in_specs,
        out_specs=out_specs,
        out_shape=out_shape,
        scratch_shapes=[pltpu.VMEM((d, d), F32), pltpu.VMEM((seq_len, d), F32)],
        compiler_params=_params("arbitrary", "arbitrary"),
    )(*args)


def _ret_kernel(*refs, seq_len, chunk, has_s0, want_final, rope):
    refs = list(refs)
    q_ref, k_ref, v_ref, g_ref, dl_ref, gn_ref = refs[:6]
    del refs[:6]
    cos_ref = refs.pop(0) if rope else None
    sin_ref = refs.pop(0) if rope else None
    s0_ref = refs.pop(0) if has_s0 else None
    o_ref = refs.pop(0)
    sf_ref = refs.pop(0) if want_final else None
    st_ref, oacc_ref = refs
    c = chunk
    nch = seq_len // c
    d = q_ref.shape[-1]
    half = d // 2
    k_scale = d ** -0.5

    i = lax.broadcasted_iota(jnp.int32, (c, c), 0)
    j = lax.broadcasted_iota(jnp.int32, (c, c), 1)
    row = lax.broadcasted_iota(jnp.int32, (c, d), 0).astype(F32)

    def decays(direction):
        lg = _log_sigmoid(dl_ref[direction])
        dist = (i - j) if direction == 0 else (j - i)
        ok = dist >= 0
        intra = jnp.where(ok, jnp.exp(jnp.where(ok, dist, 0).astype(F32) * lg[:, :c]), 0.0)
        if direction == 0:
            q_dec = jnp.exp((row + 1.0) * lg)
            k_dec = jnp.exp((c - 1.0 - row) * lg)
        else:
            q_dec = jnp.exp((c - row) * lg)
            k_dec = jnp.exp(row * lg)
        return intra, q_dec, k_dec, jnp.exp(c * lg)

    def rotate(x, sl):
        if not rope:
            return x
        cos, sin = cos_ref[sl, :], sin_ref[sl, :]
        x1, x2 = x[:, :half], x[:, half:]
        return jnp.concatenate([x1 * cos - x2 * sin, x1 * sin + x2 * cos], axis=-1)

    def chunk_out(ci, dec):
        intra, q_dec, k_dec, c_dec = dec
        sl = pl.ds(pl.multiple_of(ci * c, c), c)
        q = rotate(q_ref[sl, :], sl)
        k = rotate(k_ref[sl, :] * k_scale, sl)
        vb = v_ref[sl, :].astype(BF16)
        st = st_ref[...]
        att = lax.dot_general(q.astype(BF16), k.astype(BF16), NT_DIMS, preferred_element_type=F32) * intra
        o = jnp.dot(att.astype(BF16), vb, preferred_element_type=F32)
        o = o + jnp.dot((q * q_dec).astype(BF16), st.astype(BF16), preferred_element_type=F32)
        st_ref[...] = c_dec * st + lax.dot_general((k * k_dec).astype(BF16), vb, TN_DIMS,
                                                   preferred_element_type=F32)
        return sl, o

    def run(direction):
        dec = decays(direction)
        st_ref[...] = s0_ref[direction] if has_s0 else jnp.zeros((d, d), F32)

        def fwd_body(ci, carry):
            sl, o = chunk_out(ci, dec)
            oacc_ref[sl, :] = o
            return carry

        def bwd_body(t, carry):
            sl, o = chunk_out(nch - 1 - t, dec)
            y = oacc_ref[sl, :] + o
            yc = y - jnp.mean(y, axis=-1, keepdims=True)
            yn = yc * lax.rsqrt(jnp.mean(yc * yc, axis=-1, keepdims=True) + EPS)
            o_ref[sl, :] = (yn * gn_ref[...] * _silu(g_ref[sl, :])).astype(o_ref.dtype)
            return carry

        lax.fori_loop(0, nch, fwd_body if direction == 0 else bwd_body, 0)
        if want_final:
            sf_ref[direction] = st_ref[...]

    run(0)
    run(1)


def _ret_call(p, col0, decay_logit, gnorm, eidx, rope_tabs, s0, n_seq, seq_len, row0, want_final,
              chunk=128):
    n_heads = decay_logit.shape[-1]
    b_width = gnorm.shape[-1]
    d = b_width // n_heads
    sb = row0 // seq_len
    cb0 = col0 // d
    slab = lambda k: pl.BlockSpec((seq_len, d), lambda n, h: (n + sb, cb0 + k * n_heads + h))
    dl = jnp.broadcast_to(decay_logit[:, :, :, None, None], decay_logit.shape + (1, d))
    in_specs = [slab(0), slab(1), slab(2), slab(3),
                pl.BlockSpec((None, 2, None, 1, d), lambda n, h: (eidx, 0, h, 0, 0)),
                pl.BlockSpec((None, 1, d), lambda n, h: (eidx, 0, h))]
    args = [p, p, p, p, dl, gnorm.reshape(gnorm.shape[0], 1, b_width)]
    if rope_tabs is not None:
        tab = pl.BlockSpec((seq_len, d // 2), lambda n, h: (0, 0))
        in_specs += [tab, tab]
        args += list(rope_tabs)
    if s0 is not None:
        in_specs.append(pl.BlockSpec((None, None, 2, None, d, d), lambda n, h: (n, eidx, 0, h, 0, 0)))
        args.append(s0)
    out_specs = [pl.BlockSpec((seq_len, d), lambda n, h: (n, h))]
    out_shape = [jax.ShapeDtypeStruct((n_seq * seq_len, b_width), BF16)]
    if want_final:
        out_specs.append(pl.BlockSpec((None, 2, None, d, d), lambda n, h: (n, 0, h, 0, 0)))
        out_shape.append(jax.ShapeDtypeStruct((n_seq, 2, n_heads, d, d), F32))
    kern = functools.partial(_ret_kernel, seq_len=seq_len, chunk=chunk, has_s0=s0 is not None,
                             want_final=want_final, rope=rope_tabs is not None)
    return pl.pallas_call(
        kern,
        grid=(n_seq, n_heads),
        in_specs=in_specs,
        out_specs=out_specs,
        out_shape=out_shape,
        scratch_shapes=[pltpu.VMEM((d, d), F32), pltpu.VMEM((seq_len, d), F32)],
        compiler_params=_params("arbitrary", "arbitrary"),
    )(*args)


def _qkv_prep_kernel(*refs, n_q, n_kv, rope, want_cache):
    refs = list(refs)
    x_ref, qn_ref, kn_ref = refs[:3]
    del refs[:3]
    cos_ref = refs.pop(0) if rope else None
    sin_ref = refs.pop(0) if rope else None
    q_out, k_out, v_out = refs[:3]
    del refs[:3]
    kf_out, vf_out = refs if want_cache else (None, None)
    hd = ATTN_HEAD_DIM

    def head(h, gain):
        x = x_ref[:, h * hd:(h + 1) * hd]
        y = x * lax.rsqrt(jnp.mean(x * x, axis=-1, keepdims=True) + EPS) * gain
        if rope:
            y = y * cos_ref[...] + pltpu.roll(y, hd // 2, 1) * sin_ref[...]
        return y

    for h in range(n_q):
        q_out[:, h * hd:(h + 1) * hd] = head(h, qn_ref[...]).astype(BF16)
    for h in range(n_kv):
        y = head(n_q + h, kn_ref[...])
        k_out[:, h * hd:(h + 1) * hd] = y.astype(BF16)
        if want_cache:
            kf_out[:, h * hd:(h + 1) * hd] = y
    v = x_ref[:, (n_q + n_kv) * hd:]
    v_out[...] = v.astype(BF16)
    if want_cache:
        vf_out[...] = v


def _qkv_prep_call(qkv, q_norm, k_norm, oidx, rope_tabs, n_rows, row0, seq_len, want_cache, tr=256):
    hd = ATTN_HEAD_DIM
    n_kv = ATTN_KV_HEADS
    n_q = qkv.shape[-1] // hd - 2 * n_kv
    rb0 = row0 // tr
    gain = pl.BlockSpec((None, 1, hd), lambda i: (oidx, 0, 0))
    in_specs = [pl.BlockSpec((tr, qkv.shape[-1]), lambda i: (i + rb0, 0)), gain, gain]
    args = [qkv, q_norm.reshape(-1, 1, hd), k_norm.reshape(-1, 1, hd)]
    if rope_tabs is not None:
        per_seq = seq_len // tr
        tab = pl.BlockSpec((tr, hd), lambda i: (i % per_seq, 0))
        in_specs += [tab, tab]
        args += list(rope_tabs)
    row_spec = lambda w: pl.BlockSpec((tr, w), lambda i: (i, 0))
    out_specs = [row_spec(n_q * hd), row_spec(n_kv * hd), row_spec(n_kv * hd)]
    out_shape = [jax.ShapeDtypeStruct((n_rows, n_q * hd), BF16),
                 jax.ShapeDtypeStruct((n_rows, n_kv * hd), BF16),
                 jax.ShapeDtypeStruct((n_rows, n_kv * hd), BF16)]
    if want_cache:
        out_specs += [row_spec(n_kv * hd), row_spec(n_kv * hd)]
        out_shape += [jax.ShapeDtypeStruct((n_rows, n_kv * hd), F32)] * 2
    kern = functools.partial(_qkv_prep_kernel, n_q=n_q, n_kv=n_kv, rope=rope_tabs is not None,
                             want_cache=want_cache)
    return pl.pallas_call(
        kern,
        grid=(n_rows // tr,),
        in_specs=in_specs,
        out_specs=out_specs,
        out_shape=out_shape,
        compiler_params=_params("arbitrary"),
    )(*args)


def _attn_kernel(q_ref, k_ref, v_ref, o_ref, *, group):
    hd = ATTN_HEAD_DIM
    scale = hd ** -0.5
    k = k_ref[...]
    v = v_ref[...]
    for g in range(group):
        q = q_ref[:, g * hd:(g + 1) * hd]
        s = lax.dot_general(q, k, NT_DIMS, preferred_element_type=F32) * scale
        p = jnp.exp(s - jnp.max(s, axis=-1, keepdims=True))
        o = jnp.dot(p.astype(BF16), v, preferred_element_type=F32)
        o_ref[:, g * hd:(g + 1) * hd] = (o / jnp.sum(p, axis=-1, keepdims=True)).astype(o_ref.dtype)


def _attn_call(q, k, v, tq=256):
    b, lq, qw = q.shape
    lk = k.shape[1]
    hd = ATTN_HEAD_DIM
    n_kv = k.shape[-1] // hd
    group = qw // hd // n_kv
    kv_spec = pl.BlockSpec((None, lk, hd), lambda n, kh, i: (n, 0, kh))
    return pl.pallas_call(
        functools.partial(_attn_kernel, group=group),
        grid=(b, n_kv, lq // tq),
        in_specs=[pl.BlockSpec((None, tq, group * hd), lambda n, kh, i: (n, i, kh)), kv_spec, kv_spec],
        out_specs=pl.BlockSpec((None, tq, group * hd), lambda n, kh, i: (n, i, kh)),
        out_shape=jax.ShapeDtypeStruct((b, lq, qw), BF16),
        compiler_params=_params("arbitrary", "arbitrary", "arbitrary"),
    )(q, k, v)


def _grid_rope(n_tokens, head_dim):
    rows = n_tokens // GRID_W
    row = jnp.broadcast_to(jnp.arange(rows)[:, None], (rows, GRID_W)).reshape(-1).astype(F32)
    col = jnp.broadcast_to(jnp.arange(GRID_W)[None, :], (rows, GRID_W)).reshape(-1).astype(F32)
    per_axis = head_dim // 4
    freqs = ROPE_THETA ** (-jnp.arange(per_axis, dtype=F32) / per_axis)
    ang = jnp.concatenate([row[:, None] * freqs, col[:, None] * freqs], axis=-1)
    return jnp.cos(ang), jnp.sin(ang)


def kernel(x_prompt, x_sample, c, state_hgrn, state_ret, cache_k, cache_v, c_ctx, w_mod, b_mod, norm_mix, norm_ffn, w_in_even, w_out_even, hgrn_lb_logits, hgrn_onorm, ret_decay_logit, ret_gnorm, w_in_attn, w_out_attn, q_norm, k_norm, w_ffn_gate, w_ffn_up, w_ffn_down):
    n_p, l_p, d = x_prompt.shape
    n_s, l_s, _ = x_sample.shape
    depth = w_mod.shape[0]
    rows_p = n_p * l_p
    rows_s = n_s * l_s
    a_width = hgrn_lb_logits.shape[-1]
    b_width = ret_gnorm.shape[-1]
    kv_width = ATTN_KV_HEADS * ATTN_HEAD_DIM

    def group_of_row(r):
        return jnp.where(r < rows_p, 0, 1 + (r - rows_p) // l_s)

    n_groups = 1 + n_s
    cond = jnp.concatenate([c_ctx[None, :], c, jnp.zeros((-(1 + n_s) % SUBLANES, d), F32)], axis=0)
    mods = _adaln_call(cond, w_mod, b_mod)
    mods = mods[:, :n_groups].reshape(depth, n_groups, 6, 1, d)

    cos_a, sin_a = _grid_rope(l_s, ATTN_HEAD_DIM)
    rope_attn = (jnp.concatenate([cos_a, cos_a], axis=-1), jnp.concatenate([-sin_a, sin_a], axis=-1))
    rope_ret = _grid_rope(l_s, b_width // B_HEADS)
    lb_cum = jnp.cumsum(jax.nn.softmax(hgrn_lb_logits.astype(F32), axis=0), axis=0)
    lower_bounds = lb_cum - lb_cum[0]

    x = jnp.concatenate([x_prompt.reshape(rows_p, d), x_sample.reshape(rows_s, d)], axis=0)
    new_hgrn, new_ret, new_k, new_v = [], [], [], []
    for l in range(depth):
        h = _modulate_call(x, norm_mix, mods, l, 0, group_of_row)
        if l % 2 == 0:
            e = l // 2
            p = _mm_plain_call(h, w_in_even, e, F32, tm=1024, tn=512)
            oa_p, sh = _gla_call(p, lower_bounds, l, hgrn_onorm, e, None, n_p, l_p, 0, True)
            (oa_s,) = _gla_call(p, lower_bounds, l, hgrn_onorm, e, state_hgrn, n_s, l_s, rows_p, False)
            ob_p, sr = _ret_call(p, 5 * a_width, ret_decay_logit, ret_gnorm, e, None, None,
                                 n_p, l_p, 0, True)
            (ob_s,) = _ret_call(p, 5 * a_width, ret_decay_logit, ret_gnorm, e, rope_ret, state_ret,
                                n_s, l_s, rows_p, False)
            mix = jnp.concatenate([jnp.concatenate([oa_p, ob_p], axis=1),
                                   jnp.concatenate([oa_s, ob_s], axis=1)], axis=0)
            x = _mm_residual_call(mix, w_out_even, e, x, mods, l, 2, group_of_row, tm=1024, tn=512)
            new_hgrn.append(sh)
            new_ret.append(sr)
        else:
            o = l // 2
            qkv = _mm_plain_call(h, w_in_attn, o, F32, tm=1024, tn=512)
            q_p, k_p, v_p, kf, vf = _qkv_prep_call(qkv, q_norm, k_norm, o, None, rows_p, 0, l_p, True)
            q_s, k_s, v_s = _qkv_prep_call(qkv, q_norm, k_norm, o, rope_attn, rows_s, rows_p, l_s, False)
            att_p = _attn_call(q_p.reshape(n_p, l_p, -1), k_p.reshape(n_p, l_p, kv_width),
                               v_p.reshape(n_p, l_p, kv_width))
            past = cache_k.shape[2]
            k_all = jnp.concatenate([cache_k[:, o].reshape(n_s, past, kv_width).astype(BF16),
                                     k_s.reshape(n_s, l_s, kv_width)], axis=1)
            v_all = jnp.concatenate([cache_v[:, o].reshape(n_s, past, kv_width).astype(BF16),
                                     v_s.reshape(n_s, l_s, kv_width)], axis=1)
            att_s = _attn_call(q_s.reshape(n_s, l_s, -1), k_all, v_all)
            mix = jnp.concatenate([att_p.reshape(rows_p, d), att_s.reshape(rows_s, d)], axis=0)
            x = _mm_residual_call(mix, w_out_attn, o, x, mods, l, 2, group_of_row, tm=1024, tn=512)
            new_k.append(kf.reshape(n_p, l_p, ATTN_KV_HEADS, ATTN_HEAD_DIM))
            new_v.append(vf.reshape(n_p, l_p, ATTN_KV_HEADS, ATTN_HEAD_DIM))
        h = _modulate_call(x, norm_ffn, mods, l, 3, group_of_row)
        u = _mm_swiglu_call(h, w_ffn_gate, w_ffn_up, l, tm=1024, tn=512)
        x = _mm_residual_call(u, w_ffn_down, l, x, mods, l, 5, group_of_row, tm=256, tn=512)
    return (x[:rows_p].reshape(n_p, l_p, d), x[rows_p:].reshape(n_s, l_s, d),
            jnp.stack(new_hgrn, axis=1), jnp.stack(new_ret, axis=1),
            jnp.stack(new_k, axis=1), jnp.stack(new_v, axis=1))
```

```python
import functools
import math

import jax
import jax.numpy as jnp
from jax import lax
from jax.experimental import pallas as pl
from jax.experimental.pallas import tpu as pltpu

F32 = jnp.float32
BF16 = jnp.bfloat16

EPS = 1e-6
ROPE_THETA = 10000.0
GRID_W = 64
A_HEAD_DIM = 128
B_HEADS = 4
ATTN_HEAD_DIM = 128
ATTN_KV_HEADS = 4
LANES = 128
SUBLANES = 8
VMEM_LIMIT = 56 * 1024 * 1024
VMEM_TILE_BUDGET = 46 * 1024 * 1024

NT_DIMS = (((1,), (1,)), ((), ()))
TN_DIMS = (((0,), (0,)), ((), ()))


def _params(*sem):
    return pltpu.CompilerParams(dimension_semantics=sem, vmem_limit_bytes=VMEM_LIMIT)


def _silu(x):
    return x / (1.0 + jnp.exp(-x))


def _log_sigmoid(x):
    return jnp.minimum(x, 0.0) - jnp.log1p(jnp.exp(-jnp.abs(x)))


def _adaln_kernel(c_ref, w_ref, b_ref, o_ref):
    s = _silu(c_ref[...]).astype(BF16)
    o_ref[...] = jnp.dot(s, w_ref[...].astype(BF16), preferred_element_type=F32) + b_ref[...]


def _adaln_call(cond, w_mod, b_mod, tn=1024):
    depth, d, n = w_mod.shape
    rows = cond.shape[0]
    return pl.pallas_call(
        _adaln_kernel,
        grid=(depth, n // tn),
        in_specs=[
            pl.BlockSpec((rows, d), lambda l, j: (0, 0)),
            pl.BlockSpec((None, d, tn), lambda l, j: (l, 0, j)),
            pl.BlockSpec((None, 1, tn), lambda l, j: (l, 0, j)),
        ],
        out_specs=pl.BlockSpec((None, rows, tn), lambda l, j: (l, 0, j)),
        out_shape=jax.ShapeDtypeStruct((depth, rows, n), F32),
        compiler_params=_params("arbitrary", "arbitrary"),
    )(cond, w_mod, b_mod.reshape(depth, 1, n))


def _modulate_kernel(x_ref, g_ref, shift_ref, scale_ref, o_ref):
    x = x_ref[...]
    y = x * lax.rsqrt(jnp.mean(x * x, axis=-1, keepdims=True) + EPS) * g_ref[...]
    o_ref[...] = (y * (1.0 + scale_ref[...]) + shift_ref[...]).astype(BF16)


def _modulate_call(x, gains, mods, layer, shift_idx, group_of_row, tr=256):
    m, d = x.shape
    depth = gains.shape[0]
    grp = lambda i: group_of_row(i * tr)
    return pl.pallas_call(
        _modulate_kernel,
        grid=(m // tr,),
        in_specs=[
            pl.BlockSpec((tr, d), lambda i: (i, 0)),
            pl.BlockSpec((None, 1, d), lambda i: (layer, 0, 0)),
            pl.BlockSpec((None, None, None, 1, d), lambda i: (layer, grp(i), shift_idx, 0, 0)),
            pl.BlockSpec((None, None, None, 1, d), lambda i: (layer, grp(i), shift_idx + 1, 0, 0)),
        ],
        out_specs=pl.BlockSpec((tr, d), lambda i: (i, 0)),
        out_shape=jax.ShapeDtypeStruct((m, d), BF16),
        compiler_params=_params("arbitrary"),
    )(x, gains.reshape(depth, 1, d), mods, mods)


def _matmul_tiles(m, k, n, n_weights, out_bytes, residual, row_quantum):
    for tn in (2048, 1024, 512, 256, 128):
        if n % tn:
            continue
        for tm in (1024, 512, 256):
            if m % tm or row_quantum % tm:
                continue
            need = (2 * tm * k * 2 + n_weights * (2 * k * tn * 4 + k * tn * 2)
                    + 2 * tm * tn * out_bytes + (2 * tm * tn * 4 if residual else 0))
            if need <= VMEM_TILE_BUDGET:
                return tm, tn
    raise ValueError(f"no matmul tiling fits VMEM for {(m, k, n)}")


def _cast_weights_once(w_refs, wbf_refs):
    @pl.when(pl.program_id(1) == 0)
    def _():
        for w_ref, wbf_ref in zip(w_refs, wbf_refs):
            wbf_ref[...] = w_ref[...].astype(BF16)


def _mm_plain_kernel(a_ref, w_ref, o_ref, wbf_ref):
    _cast_weights_once((w_ref,), (wbf_ref,))
    o_ref[...] = jnp.dot(a_ref[...], wbf_ref[...], preferred_element_type=F32).astype(o_ref.dtype)


def _mm_swiglu_kernel(a_ref, wg_ref, wu_ref, o_ref, wgbf_ref, wubf_ref):
    _cast_weights_once((wg_ref, wu_ref), (wgbf_ref, wubf_ref))
    a = a_ref[...]
    g = jnp.dot(a, wgbf_ref[...], preferred_element_type=F32)
    u = jnp.dot(a, wubf_ref[...], preferred_element_type=F32)
    o_ref[...] = (_silu(g) * u).astype(o_ref.dtype)


def _mm_residual_kernel(a_ref, w_ref, x_ref, gate_ref, o_ref, wbf_ref):
    _cast_weights_once((w_ref,), (wbf_ref,))
    y = jnp.dot(a_ref[...], wbf_ref[...], preferred_element_type=F32)
    o_ref[...] = x_ref[...] + gate_ref[...] * y


def _mm_plain_call(a, w, widx, out_dtype):
    m, k = a.shape
    n = w.shape[-1]
    tm, tn = _matmul_tiles(m, k, n, 1, jnp.dtype(out_dtype).itemsize, False, m)
    return pl.pallas_call(
        _mm_plain_kernel,
        grid=(n // tn, m // tm),
        in_specs=[
            pl.BlockSpec((tm, k), lambda j, i: (i, 0)),
            pl.BlockSpec((None, k, tn), lambda j, i: (widx, 0, j)),
        ],
        out_specs=pl.BlockSpec((tm, tn), lambda j, i: (i, j)),
        out_shape=jax.ShapeDtypeStruct((m, n), out_dtype),
        scratch_shapes=[pltpu.VMEM((k, tn), BF16)],
        compiler_params=_params("arbitrary", "arbitrary"),
    )(a, w)


def _mm_swiglu_call(a, wg, wu, widx):
    m, k = a.shape
    n = wg.shape[-1]
    tm, tn = _matmul_tiles(m, k, n, 2, 2, False, m)
    wspec = pl.BlockSpec((None, k, tn), lambda j, i: (widx, 0, j))
    return pl.pallas_call(
        _mm_swiglu_kernel,
        grid=(n // tn, m // tm),
        in_specs=[pl.BlockSpec((tm, k), lambda j, i: (i, 0)), wspec, wspec],
        out_specs=pl.BlockSpec((tm, tn), lambda j, i: (i, j)),
        out_shape=jax.ShapeDtypeStruct((m, n), BF16),
        scratch_shapes=[pltpu.VMEM((k, tn), BF16), pltpu.VMEM((k, tn), BF16)],
        compiler_params=_params("arbitrary", "arbitrary"),
    )(a, wg, wu)


def _mm_residual_call(a, w, widx, x, mods, layer, gate_idx, group_of_row, row_quantum):
    m, k = a.shape
    n = w.shape[-1]
    tm, tn = _matmul_tiles(m, k, n, 1, 4, True, row_quantum)
    grp = lambda i: group_of_row(i * tm)
    return pl.pallas_call(
        _mm_residual_kernel,
        grid=(n // tn, m // tm),
        in_specs=[
            pl.BlockSpec((tm, k), lambda j, i: (i, 0)),
            pl.BlockSpec((None, k, tn), lambda j, i: (widx, 0, j)),
            pl.BlockSpec((tm, tn), lambda j, i: (i, j)),
            pl.BlockSpec((None, None, None, 1, tn), lambda j, i: (layer, grp(i), gate_idx, 0, j)),
        ],
        out_specs=pl.BlockSpec((tm, tn), lambda j, i: (i, j)),
        out_shape=jax.ShapeDtypeStruct((m, n), F32),
        scratch_shapes=[pltpu.VMEM((k, tn), BF16)],
        compiler_params=_params("arbitrary", "arbitrary"),
    )(a, w, x, mods)


def _bcast_row_in_blocks(x, block, r):
    c, d = x.shape
    x3 = x.reshape(c // block, block, d)
    return jnp.broadcast_to(x3[:, r:r + 1, :], x3.shape).reshape(c, d)


MXU_WIDTH = 256


def _gla_blocks(c):
    return [1 << b for b in range((c // 2).bit_length())] + [0]


def _gla_masks(c, fwd):
    i = lax.broadcasted_iota(jnp.int32, (c, LANES), 0)
    lane = lax.broadcasted_iota(jnp.int32, (c, LANES), 1)
    masks = []
    for b, h in enumerate(_gla_blocks(c)):
        j = lane - (b * c) % LANES
        in_block = jnp.where(j >= 0, jnp.where(j < c, 1.0, 0.0), 0.0)
        if h == 0:
            ok = jnp.where(i == j, 1.0, 0.0)
        else:
            same = (i // (2 * h)) == (j // (2 * h))
            i_hi = (i % (2 * h)) >= h
            j_hi = (j % (2 * h)) >= h
            if fwd:
                ok = jnp.where(same, jnp.where(i_hi, jnp.where(j_hi, 0.0, 1.0), 0.0), 0.0)
            else:
                ok = jnp.where(same, jnp.where(i_hi, 0.0, jnp.where(j_hi, 1.0, 0.0)), 0.0)
        masks.append((ok * in_block).astype(F32))
    return masks


def _split_roles(q, kg, h, fwd, rloc):
    c = q.shape[0]
    if h >= SUBLANES:
        parts = []
        for b in range(c // (2 * h)):
            lo = slice(2 * h * b, 2 * h * b + h)
            hi = slice(2 * h * b + h, 2 * h * (b + 1))
            parts += [kg[lo], q[hi]] if fwd else [q[lo], kg[hi]]
        return jnp.concatenate(parts, axis=0)
    in_hi = (rloc % (2 * h)) >= h
    return jnp.where(in_hi, q, kg) if fwd else jnp.where(in_hi, kg, q)


def _gla_chunks(chains, rloc):
    c, d = chains[0][0].shape
    per_group = MXU_WIDTH // c
    blocks = _gla_blocks(c)
    n_groups = -(-len(blocks) // per_group)
    n_kvregs = -(-len(blocks) * c // LANES)

    es = []
    for q, kg, fg, v, logf, st, fwd, tri_bf, masks in chains:
        hi = logf.astype(BF16)
        r1 = logf - hi.astype(F32)
        mid = r1.astype(BF16)
        lo = (r1 - mid.astype(F32)).astype(BF16)
        e3 = jnp.dot(tri_bf, jnp.concatenate([hi, mid, lo], axis=1), preferred_element_type=F32)
        es.append(e3[:, :d] + e3[:, d:2 * d] + e3[:, 2 * d:])

    sides = []
    for (q, kg, fg, v, logf, st, fwd, tri_bf, masks), e in zip(chains, es):
        qs, ks = [], []
        for h in blocks:
            if h == 0:
                qs.append(q.astype(BF16))
                ks.append(kg.astype(BF16))
                continue
            if h == 1:
                in_hi = (rloc % 2) >= 1
                w = jnp.where(in_hi, q * fg, kg) if fwd else jnp.where(in_hi, kg, q * fg)
            else:
                r = h - 1 if fwd else h
                if 2 * h >= SUBLANES:
                    e_ref = _bcast_row_in_blocks(e, 2 * h, r)
                else:
                    e_ref = jnp.where(rloc < 2 * h, _bcast_row_in_blocks(e, SUBLANES, r),
                                      _bcast_row_in_blocks(e, SUBLANES, 2 * h + r))
                w = _split_roles(q, kg, h, fwd, rloc) * jnp.exp(-jnp.abs(e - e_ref))
            qs.append(w.astype(BF16))
            ks.append(qs[-1])
        pad = [jnp.zeros((c, d), BF16)] * (n_groups * per_group - len(blocks))
        sides.append((qs + pad, ks + pad))

    atts = [[None] * n_kvregs for _ in chains]
    for g in range(n_groups):
        for ci, (qs, ks) in enumerate(sides):
            sl = slice(g * per_group, (g + 1) * per_group)
            prod = lax.dot_general(jnp.concatenate(qs[sl], axis=0), jnp.concatenate(ks[sl], axis=0),
                                   NT_DIMS, preferred_element_type=F32)
            masks = chains[ci][8]
            for b in range(g * per_group, min((g + 1) * per_group, len(blocks))):
                t = b * c // LANES
                col = (t * LANES) % MXU_WIDTH
                row = (b - g * per_group) * c
                term = prod[row:row + c, col:col + LANES] * masks[b]
                atts[ci][t] = term if atts[ci][t] is None else atts[ci][t] + term

    outs = []
    for (q, kg, fg, v, logf, st, fwd, tri_bf, masks), e, att in zip(chains, es, atts):
        vb = v.astype(BF16)
        v_rep = jnp.concatenate([vb] * (n_kvregs * LANES // c), axis=0)
        o = jnp.dot(jnp.concatenate(att, axis=1).astype(BF16), v_rep, preferred_element_type=F32)
        o = o + lax.dot_general((q * jnp.exp(e)).astype(BF16), st.astype(BF16), NT_DIMS,
                                preferred_element_type=F32)
        e_edge = e[c - 1:c] if fwd else e[0:1]
        kh = (kg * jnp.exp(e_edge - e)).astype(BF16)
        st_new = st * jnp.exp(e_edge) + lax.dot_general(vb, kh, TN_DIMS, preferred_element_type=F32)
        outs.append((o, st_new))
    return outs


def _gla_kernel(*refs, seq_len, chunk, has_s0, want_final):
    refs = list(refs)
    q_ref, ffw_ref, fbw_ref, v_ref, g_ref, lb_ref, on_ref = refs[:7]
    del refs[:7]
    s0_ref = refs.pop(0) if has_s0 else None
    o_ref = refs.pop(0)
    sf_ref = refs.pop(0) if want_final else None
    stf_ref, stb_ref, oacc_ref = refs
    c = chunk
    nch = seq_len // c
    half_n = nch // 2
    d = q_ref.shape[-1]

    lb = lb_ref[...]
    one_m_lb = 1.0 - lb
    log_1mlb = jnp.log1p(-lb)
    i = lax.broadcasted_iota(jnp.int32, (c, c), 0)
    j = lax.broadcasted_iota(jnp.int32, (c, c), 1)
    tri_f = jnp.where(i >= j, 1.0, 0.0).astype(BF16)
    tri_b = jnp.where(i <= j, 1.0, 0.0).astype(BF16)
    masks_f = _gla_masks(c, True)
    masks_b = _gla_masks(c, False)
    rloc = lax.broadcasted_iota(jnp.int32, (c, d), 0) % SUBLANES
    q_scale = d ** -0.5

    def gates(x):
        t = jnp.exp(-jnp.abs(x))
        r = 1.0 / (1.0 + t)
        pos = x >= 0.0
        fg = lb + one_m_lb * jnp.where(pos, r, t * r)
        kg = one_m_lb * jnp.where(pos, t * r, r)
        logf = jnp.maximum(jnp.log(fg), log_1mlb + jnp.minimum(x, 0.0) - jnp.log(1.0 + t))
        return fg, kg, logf

    def both_directions(t):
        slices, chains = [], []
        for fwd in (True, False):
            sl = pl.ds(pl.multiple_of((t if fwd else nch - 1 - t) * c, c), c)
            fg, kg, logf = gates((ffw_ref if fwd else fbw_ref)[sl, :])
            q = _silu(q_ref[sl, :]) * q_scale
            slices.append(sl)
            chains.append((q, kg, fg, v_ref[sl, :], logf, (stf_ref if fwd else stb_ref)[...], fwd,
                           tri_f if fwd else tri_b, masks_f if fwd else masks_b))
        (o_f, st_f), (o_b, st_b) = _gla_chunks(chains, rloc)
        stf_ref[...] = st_f
        stb_ref[...] = st_b
        return zip(slices, (o_f, o_b))

    def finish(sl, o):
        y = oacc_ref[sl, :] + o
        y = y * lax.rsqrt(jnp.mean(y * y, axis=-1, keepdims=True) + EPS) * on_ref[...]
        o_ref[sl, :] = (y * _silu(g_ref[sl, :])).astype(o_ref.dtype)

    def first_half(t, carry):
        for sl, o in both_directions(t):
            oacc_ref[sl, :] = o
        return carry

    def second_half(t, carry):
        for sl, o in both_directions(t):
            finish(sl, o)
        return carry

    for st_ref, direction in ((stf_ref, 0), (stb_ref, 1)):
        st_ref[...] = s0_ref[direction].T if has_s0 else jnp.zeros((d, d), F32)
    unroll = 2 if half_n % 2 == 0 else 1
    lax.fori_loop(0, half_n, first_half, 0, unroll=unroll)
    lax.fori_loop(half_n, nch, second_half, 0, unroll=unroll)
    if want_final:
        sf_ref[0] = stf_ref[...].T
        sf_ref[1] = stb_ref[...].T


def _gla_call(p, lower_bounds, layer, onorm, eidx, s0, n_seq, seq_len, row0, want_final, chunk=64):
    d = A_HEAD_DIM
    n_heads = lower_bounds.shape[-1] // d
    a_width = n_heads * d
    assert row0 % seq_len == 0 and seq_len % (2 * chunk) == 0
    sb = row0 // seq_len
    col = lambda k: (lambda n, h: (n + sb, k * n_heads + h))
    slab = lambda k: pl.BlockSpec((seq_len, d), col(k))
    in_specs = [slab(0), slab(1), slab(2), slab(3), slab(4),
                pl.BlockSpec((None, 1, d), lambda n, h: (layer, 0, h)),
                pl.BlockSpec((None, 1, d), lambda n, h: (eidx, 0, 0))]
    args = [p, p, p, p, p, lower_bounds.reshape(lower_bounds.shape[0], 1, a_width),
            onorm.reshape(onorm.shape[0], 1, d)]
    if s0 is not None:
        in_specs.append(pl.BlockSpec((None, None, 2, None, d, d), lambda n, h: (n, eidx, 0, h, 0, 0)))
        args.append(s0)
    out_specs = [pl.BlockSpec((seq_len, d), lambda n, h: (n, h))]
    out_shape = [jax.ShapeDtypeStruct((n_seq * seq_len, a_width), BF16)]
    if want_final:
        out_specs.append(pl.BlockSpec((None, 2, None, d, d), lambda n, h: (n, 0, h, 0, 0)))
        out_shape.append(jax.ShapeDtypeStruct((n_seq, 2, n_heads, d, d), F32))
    kern = functools.partial(_gla_kernel, seq_len=seq_len, chunk=chunk,
                             has_s0=s0 is not None, want_final=want_final)
    return pl.pallas_call(
        kern,
        grid=(n_seq, n_heads),
        in_specs=in_specs,
        out_specs=out_specs,
        out_shape=out_shape,
        scratch_shapes=[pltpu.VMEM((d, d), F32), pltpu.VMEM((d, d), F32), pltpu.VMEM((seq_len, d), F32)],
        compiler_params=_params("arbitrary", "arbitrary"),
    )(*args)


def _ret_kernel(*refs, seq_len, chunk, has_s0, want_final, rope):
    refs = list(refs)
    q_ref, k_ref, v_ref, g_ref, dl_ref, gn_ref = refs[:6]
    del refs[:6]
    cos_ref = refs.pop(0) if rope else None
    sin_ref = refs.pop(0) if rope else None
    s0_ref = refs.pop(0) if has_s0 else None
    o_ref = refs.pop(0)
    sf_ref = refs.pop(0) if want_final else None
    stf_ref, stb_ref, oacc_ref = refs
    c = chunk
    nch = seq_len // c
    half_n = nch // 2
    d = q_ref.shape[-1]
    half = d // 2
    k_scale = d ** -0.5

    lg_f = _log_sigmoid(dl_ref[0])
    lg_b = _log_sigmoid(dl_ref[1])
    dist = (lax.broadcasted_iota(jnp.int32, (c, c), 0)
            - lax.broadcasted_iota(jnp.int32, (c, c), 1)).astype(F32)
    low = dist >= 0.0
    up = dist <= 0.0
    intra = (jnp.where(low, jnp.exp(jnp.where(low, dist, 0.0) * lg_f[:, :c]), 0.0)
             + jnp.where(up, jnp.exp(jnp.where(up, -dist, 0.0) * lg_b[:, :c]), 0.0))
    row = lax.broadcasted_iota(jnp.int32, (c, d), 0).astype(F32)
    q_dec_f = jnp.exp((row + 1.0) * lg_f)
    k_dec_f = jnp.exp((c - 1.0 - row) * lg_f)
    q_dec_b = jnp.exp((c - row) * lg_b)
    k_dec_b = jnp.exp(row * lg_b)
    c_dec_f = jnp.exp(c * lg_f)
    c_dec_b = jnp.exp(c * lg_b)

    def rotate(x, sl):
        if not rope:
            return x
        cos, sin = cos_ref[sl, :], sin_ref[sl, :]
        x1, x2 = x[:, :half], x[:, half:]
        return jnp.concatenate([x1 * cos - x2 * sin, x1 * sin + x2 * cos], axis=-1)

    def load(ci):
        sl = pl.ds(pl.multiple_of(ci * c, c), c)
        return sl, rotate(q_ref[sl, :], sl), rotate(k_ref[sl, :] * k_scale, sl), v_ref[sl, :].astype(BF16)

    def from_left(ci):
        sl, q, k, vb = load(ci)
        st = stf_ref[...]
        att = lax.dot_general(q.astype(BF16), k.astype(BF16), NT_DIMS, preferred_element_type=F32) * intra
        o = jnp.dot(att.astype(BF16), vb, preferred_element_type=F32)
        o = o + jnp.dot((q * q_dec_f).astype(BF16), st.astype(BF16), preferred_element_type=F32)
        stf_ref[...] = c_dec_f * st + lax.dot_general((k * k_dec_f).astype(BF16), vb, TN_DIMS,
                                                      preferred_element_type=F32)
        return sl, o

    def from_right(ci):
        sl, q, k, vb = load(ci)
        st = stb_ref[...]
        o = jnp.dot((q * q_dec_b).astype(BF16), st.astype(BF16), preferred_element_type=F32)
        stb_ref[...] = c_dec_b * st + lax.dot_general((k * k_dec_b).astype(BF16), vb, TN_DIMS,
                                                      preferred_element_type=F32)
        return sl, o

    def finish(sl, o):
        y = oacc_ref[sl, :] + o
        yc = y - jnp.mean(y, axis=-1, keepdims=True)
        yn = yc * lax.rsqrt(jnp.mean(yc * yc, axis=-1, keepdims=True) + EPS)
        o_ref[sl, :] = (yn * gn_ref[...] * _silu(g_ref[sl, :])).astype(o_ref.dtype)

    def first_half(t, carry):
        for sl, o in (from_left(t), from_right(nch - 1 - t)):
            oacc_ref[sl, :] = o
        return carry

    def second_half(t, carry):
        finish(*from_left(t))
        finish(*from_right(nch - 1 - t))
        return carry

    for st_ref, direction in ((stf_ref, 0), (stb_ref, 1)):
        st_ref[...] = s0_ref[direction] if has_s0 else jnp.zeros((d, d), F32)
    lax.fori_loop(0, half_n, first_half, 0)
    lax.fori_loop(half_n, nch, second_half, 0)
    if want_final:
        sf_ref[0] = stf_ref[...]
        sf_ref[1] = stb_ref[...]


def _ret_call(p, col0, decay_logit, gnorm, eidx, rope_tabs, s0, n_seq, seq_len, row0, want_final,
              chunk=128):
    n_heads = decay_logit.shape[-1]
    b_width = gnorm.shape[-1]
    d = b_width // n_heads
    assert row0 % seq_len == 0 and seq_len % (2 * chunk) == 0 and col0 % d == 0 and chunk <= d
    sb = row0 // seq_len
    cb0 = col0 // d
    slab = lambda k: pl.BlockSpec((seq_len, d), lambda n, h: (n + sb, cb0 + k * n_heads + h))
    dl = jnp.broadcast_to(decay_logit[:, :, :, None, None], decay_logit.shape + (1, d))
    in_specs = [slab(0), slab(1), slab(2), slab(3),
                pl.BlockSpec((None, 2, None, 1, d), lambda n, h: (eidx, 0, h, 0, 0)),
                pl.BlockSpec((None, 1, d), lambda n, h: (eidx, 0, h))]
    args = [p, p, p, p, dl, gnorm.reshape(gnorm.shape[0], 1, b_width)]
    if rope_tabs is not None:
        tab = pl.BlockSpec((seq_len, d // 2), lambda n, h: (0, 0))
        in_specs += [tab, tab]
        args += list(rope_tabs)
    if s0 is not None:
        in_specs.append(pl.BlockSpec((None, None, 2, None, d, d), lambda n, h: (n, eidx, 0, h, 0, 0)))
        args.append(s0)
    out_specs = [pl.BlockSpec((seq_len, d), lambda n, h: (n, h))]
    out_shape = [jax.ShapeDtypeStruct((n_seq * seq_len, b_width), BF16)]
    if want_final:
        out_specs.append(pl.BlockSpec((None, 2, None, d, d), lambda n, h: (n, 0, h, 0, 0)))
        out_shape.append(jax.ShapeDtypeStruct((n_seq, 2, n_heads, d, d), F32))
    kern = functools.partial(_ret_kernel, seq_len=seq_len, chunk=chunk, has_s0=s0 is not None,
                             want_final=want_final, rope=rope_tabs is not None)
    return pl.pallas_call(
        kern,
        grid=(n_seq, n_heads),
        in_specs=in_specs,
        out_specs=out_specs,
        out_shape=out_shape,
        scratch_shapes=[pltpu.VMEM((d, d), F32), pltpu.VMEM((d, d), F32), pltpu.VMEM((seq_len, d), F32)],
        compiler_params=_params("arbitrary", "arbitrary"),
    )(*args)


Q_PRESCALE = ATTN_HEAD_DIM ** -0.5 * math.log2(math.e)


def _qkv_prep_kernel(*refs, n_q, n_kv, rope, want_cache):
    refs = list(refs)
    x_ref, qn_ref, kn_ref = refs[:3]
    del refs[:3]
    cos_ref = refs.pop(0) if rope else None
    sin_ref = refs.pop(0) if rope else None
    q_out, k_out, v_out = refs[:3]
    del refs[:3]
    kf_out, vf_out = refs if want_cache else (None, None)
    hd = ATTN_HEAD_DIM

    def head(h, gain):
        x = x_ref[:, h * hd:(h + 1) * hd]
        y = x * lax.rsqrt(jnp.mean(x * x, axis=-1, keepdims=True) + EPS) * gain
        if rope:
            y = y * cos_ref[...] + pltpu.roll(y, hd // 2, 1) * sin_ref[...]
        return y

    for h in range(n_q):
        q_out[:, h * hd:(h + 1) * hd] = (head(h, qn_ref[...]) * Q_PRESCALE).astype(BF16)
    for h in range(n_kv):
        y = head(n_q + h, kn_ref[...])
        k_out[:, h * hd:(h + 1) * hd] = y.astype(BF16)
        if want_cache:
            kf_out[:, h * hd:(h + 1) * hd] = y
    v = x_ref[:, (n_q + n_kv) * hd:]
    v_out[...] = v.astype(BF16)
    if want_cache:
        vf_out[...] = v


def _qkv_prep_call(qkv, q_norm, k_norm, oidx, rope_tabs, n_rows, row0, seq_len, want_cache, tr=256):
    hd = ATTN_HEAD_DIM
    n_kv = ATTN_KV_HEADS
    n_q = qkv.shape[-1] // hd - 2 * n_kv
    assert row0 % tr == 0 and seq_len % tr == 0
    rb0 = row0 // tr
    gain = pl.BlockSpec((None, 1, hd), lambda i: (oidx, 0, 0))
    in_specs = [pl.BlockSpec((tr, qkv.shape[-1]), lambda i: (i + rb0, 0)), gain, gain]
    args = [qkv, q_norm.reshape(-1, 1, hd), k_norm.reshape(-1, 1, hd)]
    if rope_tabs is not None:
        per_seq = seq_len // tr
        tab = pl.BlockSpec((tr, hd), lambda i: (i % per_seq, 0))
        in_specs += [tab, tab]
        args += list(rope_tabs)
    row_spec = lambda w: pl.BlockSpec((tr, w), lambda i: (i, 0))
    out_specs = [row_spec(n_q * hd), row_spec(n_kv * hd), row_spec(n_kv * hd)]
    out_shape = [jax.ShapeDtypeStruct((n_rows, n_q * hd), BF16),
                 jax.ShapeDtypeStruct((n_rows, n_kv * hd), BF16),
                 jax.ShapeDtypeStruct((n_rows, n_kv * hd), BF16)]
    if want_cache:
        out_specs += [row_spec(n_kv * hd), row_spec(n_kv * hd)]
        out_shape += [jax.ShapeDtypeStruct((n_rows, n_kv * hd), F32)] * 2
    kern = functools.partial(_qkv_prep_kernel, n_q=n_q, n_kv=n_kv, rope=rope_tabs is not None,
                             want_cache=want_cache)
    return pl.pallas_call(
        kern,
        grid=(n_rows // tr,),
        in_specs=in_specs,
        out_specs=out_specs,
        out_shape=out_shape,
        compiler_params=_params("arbitrary"),
    )(*args)


def _attn_kernel(q_ref, k_ref, v_ref, o_ref, *, group):
    hd = ATTN_HEAD_DIM
    k = k_ref[...]
    v = v_ref[...]
    v_ext = jnp.concatenate([v, jnp.ones_like(v)], axis=1)
    for g in range(group):
        q = q_ref[:, g * hd:(g + 1) * hd]
        s = lax.dot_general(q, k, NT_DIMS, preferred_element_type=F32)
        p = jnp.exp2(s - jnp.max(s, axis=-1, keepdims=True))
        o = jnp.dot(p.astype(BF16), v_ext, preferred_element_type=F32)
        o_ref[:, g * hd:(g + 1) * hd] = (o[:, :hd] / o[:, hd:]).astype(o_ref.dtype)


def _attn_call(q, k, v, tq=256):
    b, lq, qw = q.shape
    lk = k.shape[1]
    hd = ATTN_HEAD_DIM
    n_kv = k.shape[-1] // hd
    group = qw // hd // n_kv
    kv_spec = pl.BlockSpec((None, lk, hd), lambda n, kh, i: (n, 0, kh))
    return pl.pallas_call(
        functools.partial(_attn_kernel, group=group),
        grid=(b, n_kv, lq // tq),
        in_specs=[pl.BlockSpec((None, tq, group * hd), lambda n, kh, i: (n, i, kh)), kv_spec, kv_spec],
        out_specs=pl.BlockSpec((None, tq, group * hd), lambda n, kh, i: (n, i, kh)),
        out_shape=jax.ShapeDtypeStruct((b, lq, qw), BF16),
        compiler_params=_params("arbitrary", "arbitrary", "arbitrary"),
    )(q, k, v)


def _grid_rope(n_tokens, head_dim):
    rows = n_tokens // GRID_W
    row = jnp.broadcast_to(jnp.arange(rows)[:, None], (rows, GRID_W)).reshape(-1).astype(F32)
    col = jnp.broadcast_to(jnp.arange(GRID_W)[None, :], (rows, GRID_W)).reshape(-1).astype(F32)
    per_axis = head_dim // 4
    freqs = ROPE_THETA ** (-jnp.arange(per_axis, dtype=F32) / per_axis)
    ang = jnp.concatenate([row[:, None] * freqs, col[:, None] * freqs], axis=-1)
    return jnp.cos(ang), jnp.sin(ang)


def kernel(x_prompt, x_sample, c, state_hgrn, state_ret, cache_k, cache_v, c_ctx, w_mod, b_mod, norm_mix, norm_ffn, w_in_even, w_out_even, hgrn_lb_logits, hgrn_onorm, ret_decay_logit, ret_gnorm, w_in_attn, w_out_attn, q_norm, k_norm, w_ffn_gate, w_ffn_up, w_ffn_down):
    n_p, l_p, d = x_prompt.shape
    n_s, l_s, _ = x_sample.shape
    depth = w_mod.shape[0]
    rows_p = n_p * l_p
    rows_s = n_s * l_s
    a_width = hgrn_lb_logits.shape[-1]
    b_width = ret_gnorm.shape[-1]
    kv_width = ATTN_KV_HEADS * ATTN_HEAD_DIM
    row_quantum = math.gcd(rows_p, l_s)

    def group_of_row(r):
        return jnp.where(r < rows_p, 0, 1 + (r - rows_p) // l_s)

    n_groups = 1 + n_s
    cond = jnp.concatenate([c_ctx[None, :], c, jnp.zeros((-(1 + n_s) % SUBLANES, d), F32)], axis=0)
    mods = _adaln_call(cond, w_mod, b_mod)
    mods = mods[:, :n_groups].reshape(depth, n_groups, 6, 1, d)

    cos_a, sin_a = _grid_rope(l_s, ATTN_HEAD_DIM)
    rope_attn = (jnp.concatenate([cos_a, cos_a], axis=-1), jnp.concatenate([-sin_a, sin_a], axis=-1))
    rope_ret = _grid_rope(l_s, b_width // B_HEADS)
    lb_cum = jnp.cumsum(jax.nn.softmax(hgrn_lb_logits.astype(F32), axis=0), axis=0)
    lower_bounds = lb_cum - lb_cum[0]

    x = jnp.concatenate([x_prompt.reshape(rows_p, d), x_sample.reshape(rows_s, d)], axis=0)
    new_hgrn, new_ret, new_k, new_v = [], [], [], []
    for l in range(depth):
        h = _modulate_call(x, norm_mix, mods, l, 0, group_of_row)
        if l % 2 == 0:
            e = l // 2
            p = _mm_plain_call(h, w_in_even, e, F32)
            oa_p, sh = _gla_call(p, lower_bounds, l, hgrn_onorm, e, None, n_p, l_p, 0, True)
            (oa_s,) = _gla_call(p, lower_bounds, l, hgrn_onorm, e, state_hgrn, n_s, l_s, rows_p, False)
            ob_p, sr = _ret_call(p, 5 * a_width, ret_decay_logit, ret_gnorm, e, None, None,
                                 n_p, l_p, 0, True)
            (ob_s,) = _ret_call(p, 5 * a_width, ret_decay_logit, ret_gnorm, e, rope_ret, state_ret,
                                n_s, l_s, rows_p, False)
            mix = jnp.concatenate([jnp.concatenate([oa_p, ob_p], axis=1),
                                   jnp.concatenate([oa_s, ob_s], axis=1)], axis=0)
            x = _mm_residual_call(mix, w_out_even, e, x, mods, l, 2, group_of_row, row_quantum)
            new_hgrn.append(sh)
            new_ret.append(sr)
        else:
            o = l // 2
            qkv = _mm_plain_call(h, w_in_attn, o, F32)
            q_p, k_p, v_p, kf, vf = _qkv_prep_call(qkv, q_norm, k_norm, o, None, rows_p, 0, l_p, True)
            q_s, k_s, v_s = _qkv_prep_call(qkv, q_norm, k_norm, o, rope_attn, rows_s, rows_p, l_s, False)
            att_p = _attn_call(q_p.reshape(n_p, l_p, -1), k_p.reshape(n_p, l_p, kv_width),
                               v_p.reshape(n_p, l_p, kv_width))
            past = cache_k.shape[2]
            k_all = jnp.concatenate([cache_k[:, o].reshape(n_s, past, kv_width).astype(BF16),
                                     k_s.reshape(n_s, l_s, kv_width)], axis=1)
            v_all = jnp.concatenate([cache_v[:, o].reshape(n_s, past, kv_width).astype(BF16),
                                     v_s.reshape(n_s, l_s, kv_width)], axis=1)
            att_s = _attn_call(q_s.reshape(n_s, l_s, -1), k_all, v_all)
            mix = jnp.concatenate([att_p.reshape(rows_p, d), att_s.reshape(rows_s, d)], axis=0)
            x = _mm_residual_call(mix, w_out_attn, o, x, mods, l, 2, group_of_row, row_quantum)
            new_k.append(kf.reshape(n_p, l_p, ATTN_KV_HEADS, ATTN_HEAD_DIM))
            new_v.append(vf.reshape(n_p, l_p, ATTN_KV_HEADS, ATTN_HEAD_DIM))
        h = _modulate_call(x, norm_ffn, mods, l, 3, group_of_row)
        u = _mm_swiglu_call(h, w_ffn_gate, w_ffn_up, l)
        x = _mm_residual_call(u, w_ffn_down, l, x, mods, l, 5, group_of_row, row_quantum)
    return (x[:rows_p].reshape(n_p, l_p, d), x[rows_p:].reshape(n_s, l_s, d),
            jnp.stack(new_hgrn, axis=1), jnp.stack(new_ret, axis=1),
            jnp.stack(new_k, axis=1), jnp.stack(new_v, axis=1))
```

```python
import functools
import math

import jax
import jax.numpy as jnp
from jax import lax
from jax.experimental import pallas as pl
from jax.experimental.pallas import tpu as pltpu

F32 = jnp.float32
BF16 = jnp.bfloat16

EPS = 1e-6
ROPE_THETA = 10000.0
GRID_W = 64
A_HEAD_DIM = 128
B_HEADS = 4
ATTN_HEAD_DIM = 128
ATTN_KV_HEADS = 4
LANES = 128
SUBLANES = 8
VMEM_LIMIT = 56 * 1024 * 1024
VMEM_TILE_BUDGET = 46 * 1024 * 1024

NT_DIMS = (((1,), (1,)), ((), ()))
TN_DIMS = (((0,), (0,)), ((), ()))


def _params(*sem):
    return pltpu.CompilerParams(dimension_semantics=sem, vmem_limit_bytes=VMEM_LIMIT)


def _silu(x):
    return x / (1.0 + jnp.exp(-x))


def _log_sigmoid(x):
    return jnp.minimum(x, 0.0) - jnp.log1p(jnp.exp(-jnp.abs(x)))


def _adaln_kernel(c_ref, w_ref, b_ref, o_ref):
    s = _silu(c_ref[...]).astype(BF16)
    o_ref[...] = jnp.dot(s, w_ref[...].astype(BF16), preferred_element_type=F32) + b_ref[...]


def _adaln_call(cond, w_mod, b_mod, tn=1024):
    depth, d, n = w_mod.shape
    rows = cond.shape[0]
    return pl.pallas_call(
        _adaln_kernel,
        grid=(depth, n // tn),
        in_specs=[
            pl.BlockSpec((rows, d), lambda l, j: (0, 0)),
            pl.BlockSpec((None, d, tn), lambda l, j: (l, 0, j)),
            pl.BlockSpec((None, 1, tn), lambda l, j: (l, 0, j)),
        ],
        out_specs=pl.BlockSpec((None, rows, tn), lambda l, j: (l, 0, j)),
        out_shape=jax.ShapeDtypeStruct((depth, rows, n), F32),
        compiler_params=_params("arbitrary", "arbitrary"),
    )(cond, w_mod, b_mod.reshape(depth, 1, n))


def _modulate_kernel(x_ref, g_ref, shift_ref, scale_ref, o_ref):
    x = x_ref[...]
    y = x * lax.rsqrt(jnp.mean(x * x, axis=-1, keepdims=True) + EPS) * g_ref[...]
    o_ref[...] = (y * (1.0 + scale_ref[...]) + shift_ref[...]).astype(BF16)


def _modulate_call(x, gains, mods, layer, shift_idx, group_of_row, tr=256):
    m, d = x.shape
    depth = gains.shape[0]
    grp = lambda i: group_of_row(i * tr)
    return pl.pallas_call(
        _modulate_kernel,
        grid=(m // tr,),
        in_specs=[
            pl.BlockSpec((tr, d), lambda i: (i, 0)),
            pl.BlockSpec((None, 1, d), lambda i: (layer, 0, 0)),
            pl.BlockSpec((None, None, None, 1, d), lambda i: (layer, grp(i), shift_idx, 0, 0)),
            pl.BlockSpec((None, None, None, 1, d), lambda i: (layer, grp(i), shift_idx + 1, 0, 0)),
        ],
        out_specs=pl.BlockSpec((tr, d), lambda i: (i, 0)),
        out_shape=jax.ShapeDtypeStruct((m, d), BF16),
        compiler_params=_params("arbitrary"),
    )(x, gains.reshape(depth, 1, d), mods, mods)


def _matmul_tiles(m, k, n, n_weights, out_bytes, residual, row_quantum):
    for tn in (2048, 1024, 512, 256, 128):
        if n % tn:
            continue
        for tm in (1024, 512, 256):
            if m % tm or row_quantum % tm:
                continue
            need = (2 * tm * k * 2 + n_weights * (2 * k * tn * 4 + k * tn * 2)
                    + 2 * tm * tn * out_bytes + (2 * tm * tn * 4 if residual else 0))
            if need <= VMEM_TILE_BUDGET:
                return tm, tn
    raise ValueError(f"no matmul tiling fits VMEM for {(m, k, n)}")


def _cast_weights_once(w_refs, wbf_refs):
    @pl.when(pl.program_id(1) == 0)
    def _():
        for w_ref, wbf_ref in zip(w_refs, wbf_refs):
            wbf_ref[...] = w_ref[...].astype(BF16)


def _mm_plain_kernel(a_ref, w_ref, o_ref, wbf_ref):
    _cast_weights_once((w_ref,), (wbf_ref,))
    o_ref[...] = jnp.dot(a_ref[...], wbf_ref[...], preferred_element_type=F32).astype(o_ref.dtype)


def _mm_swiglu_kernel(a_ref, wg_ref, wu_ref, o_ref, wgbf_ref, wubf_ref):
    _cast_weights_once((wg_ref, wu_ref), (wgbf_ref, wubf_ref))
    a = a_ref[...]
    g = jnp.dot(a, wgbf_ref[...], preferred_element_type=F32)
    u = jnp.dot(a, wubf_ref[...], preferred_element_type=F32)
    o_ref[...] = (_silu(g) * u).astype(o_ref.dtype)


def _mm_residual_kernel(a_ref, w_ref, x_ref, gate_ref, o_ref, wbf_ref):
    _cast_weights_once((w_ref,), (wbf_ref,))
    y = jnp.dot(a_ref[...], wbf_ref[...], preferred_element_type=F32)
    o_ref[...] = x_ref[...] + gate_ref[...] * y


def _mm_residual_modulate_kernel(a_ref, w_ref, x_ref, gate_ref, g_ref, shift_ref, scale_ref,
                                 ox_ref, oh_ref, wbf_ref):
    @pl.when(pl.program_id(0) == 0)
    def _():
        wbf_ref[...] = w_ref[...].astype(BF16)
    y = jnp.dot(a_ref[...], wbf_ref[...], preferred_element_type=F32)
    x = x_ref[...] + gate_ref[...] * y
    ox_ref[...] = x
    h = x * lax.rsqrt(jnp.mean(x * x, axis=-1, keepdims=True) + EPS) * g_ref[...]
    oh_ref[...] = (h * (1.0 + scale_ref[...]) + shift_ref[...]).astype(BF16)


def _mm_residual_modulate_call(a, w, widx, x, mods, layer, gate_idx, gains, shift_idx, group_of_row,
                               row_quantum):
    m, k = a.shape
    n = w.shape[-1]
    depth = gains.shape[0]
    for tm in (512, 256, 128):
        need = k * n * (4 + 2) + 2 * tm * (k * 2 + n * (4 + 4 + 2))
        if m % tm == 0 and row_quantum % tm == 0 and need <= VMEM_TILE_BUDGET:
            break
    else:
        raise ValueError(f"no row tile fits VMEM for {(m, k, n)}")
    grp = lambda i: group_of_row(i * tm)
    mod_spec = lambda which: pl.BlockSpec((None, None, None, 1, n), lambda i: (layer, grp(i), which, 0, 0))
    row_spec = pl.BlockSpec((tm, n), lambda i: (i, 0))
    return pl.pallas_call(
        _mm_residual_modulate_kernel,
        grid=(m // tm,),
        in_specs=[
            pl.BlockSpec((tm, k), lambda i: (i, 0)),
            pl.BlockSpec((None, k, n), lambda i: (widx, 0, 0), pipeline_mode=pl.Buffered(1)),
            row_spec,
            mod_spec(gate_idx),
            pl.BlockSpec((None, 1, n), lambda i: (layer, 0, 0)),
            mod_spec(shift_idx),
            mod_spec(shift_idx + 1),
        ],
        out_specs=[row_spec, row_spec],
        out_shape=[jax.ShapeDtypeStruct((m, n), F32), jax.ShapeDtypeStruct((m, n), BF16)],
        scratch_shapes=[pltpu.VMEM((k, n), BF16)],
        compiler_params=_params("arbitrary"),
    )(a, w, x, mods, gains.reshape(depth, 1, n), mods, mods)


def _mm_plain_call(a, w, widx, out_dtype):
    m, k = a.shape
    n = w.shape[-1]
    tm, tn = _matmul_tiles(m, k, n, 1, jnp.dtype(out_dtype).itemsize, False, m)
    return pl.pallas_call(
        _mm_plain_kernel,
        grid=(n // tn, m // tm),
        in_specs=[
            pl.BlockSpec((tm, k), lambda j, i: (i, 0)),
            pl.BlockSpec((None, k, tn), lambda j, i: (widx, 0, j)),
        ],
        out_specs=pl.BlockSpec((tm, tn), lambda j, i: (i, j)),
        out_shape=jax.ShapeDtypeStruct((m, n), out_dtype),
        scratch_shapes=[pltpu.VMEM((k, tn), BF16)],
        compiler_params=_params("arbitrary", "arbitrary"),
    )(a, w)


def _mm_swiglu_call(a, wg, wu, widx):
    m, k = a.shape
    n = wg.shape[-1]
    tm, tn = _matmul_tiles(m, k, n, 2, 2, False, m)
    wspec = pl.BlockSpec((None, k, tn), lambda j, i: (widx, 0, j))
    return pl.pallas_call(
        _mm_swiglu_kernel,
        grid=(n // tn, m // tm),
        in_specs=[pl.BlockSpec((tm, k), lambda j, i: (i, 0)), wspec, wspec],
        out_specs=pl.BlockSpec((tm, tn), lambda j, i: (i, j)),
        out_shape=jax.ShapeDtypeStruct((m, n), BF16),
        scratch_shapes=[pltpu.VMEM((k, tn), BF16), pltpu.VMEM((k, tn), BF16)],
        compiler_params=_params("arbitrary", "arbitrary"),
    )(a, wg, wu)


def _mm_residual_call(a, w, widx, x, mods, layer, gate_idx, group_of_row, row_quantum):
    m, k = a.shape
    n = w.shape[-1]
    tm, tn = _matmul_tiles(m, k, n, 1, 4, True, row_quantum)
    grp = lambda i: group_of_row(i * tm)
    return pl.pallas_call(
        _mm_residual_kernel,
        grid=(n // tn, m // tm),
        in_specs=[
            pl.BlockSpec((tm, k), lambda j, i: (i, 0)),
            pl.BlockSpec((None, k, tn), lambda j, i: (widx, 0, j)),
            pl.BlockSpec((tm, tn), lambda j, i: (i, j)),
            pl.BlockSpec((None, None, None, 1, tn), lambda j, i: (layer, grp(i), gate_idx, 0, j)),
        ],
        out_specs=pl.BlockSpec((tm, tn), lambda j, i: (i, j)),
        out_shape=jax.ShapeDtypeStruct((m, n), F32),
        scratch_shapes=[pltpu.VMEM((k, tn), BF16)],
        compiler_params=_params("arbitrary", "arbitrary"),
    )(a, w, x, mods)


def _bcast_row_in_blocks(x, block, r):
    c, d = x.shape
    x3 = x.reshape(c // block, block, d)
    return jnp.broadcast_to(x3[:, r:r + 1, :], x3.shape).reshape(c, d)


MXU_WIDTH = 256


def _gla_blocks(c):
    return [1 << b for b in range((c // 2).bit_length())] + [0]


def _gla_masks(c, fwd):
    i = lax.broadcasted_iota(jnp.int32, (c, LANES), 0)
    lane = lax.broadcasted_iota(jnp.int32, (c, LANES), 1)
    masks = []
    for b, h in enumerate(_gla_blocks(c)):
        j = lane - (b * c) % LANES
        in_block = jnp.where(j >= 0, jnp.where(j < c, 1.0, 0.0), 0.0)
        if h == 0:
            ok = jnp.where(i == j, 1.0, 0.0)
        else:
            same = (i // (2 * h)) == (j // (2 * h))
            i_hi = (i % (2 * h)) >= h
            j_hi = (j % (2 * h)) >= h
            if fwd:
                ok = jnp.where(same, jnp.where(i_hi, jnp.where(j_hi, 0.0, 1.0), 0.0), 0.0)
            else:
                ok = jnp.where(same, jnp.where(i_hi, 0.0, jnp.where(j_hi, 1.0, 0.0)), 0.0)
        masks.append((ok * in_block).astype(F32))
    return masks


def _split_roles(q, kg, h, fwd, rloc):
    c = q.shape[0]
    if h >= SUBLANES:
        parts = []
        for b in range(c // (2 * h)):
            lo = slice(2 * h * b, 2 * h * b + h)
            hi = slice(2 * h * b + h, 2 * h * (b + 1))
            parts += [kg[lo], q[hi]] if fwd else [q[lo], kg[hi]]
        return jnp.concatenate(parts, axis=0)
    in_hi = (rloc % (2 * h)) >= h
    return jnp.where(in_hi, q, kg) if fwd else jnp.where(in_hi, kg, q)


def _gla_chunks(chains, rloc):
    c, d = chains[0][0].shape
    per_group = MXU_WIDTH // c
    blocks = _gla_blocks(c)
    n_groups = -(-len(blocks) // per_group)
    n_kvregs = -(-len(blocks) * c // LANES)

    es = []
    for q, kg, fg, v, logf, st, fwd, tri_bf, masks in chains:
        hi = logf.astype(BF16)
        r1 = logf - hi.astype(F32)
        mid = r1.astype(BF16)
        lo = (r1 - mid.astype(F32)).astype(BF16)
        e3 = jnp.dot(tri_bf, jnp.concatenate([hi, mid, lo], axis=1), preferred_element_type=F32)
        es.append(e3[:, :d] + e3[:, d:2 * d] + e3[:, 2 * d:])

    sides = []
    for (q, kg, fg, v, logf, st, fwd, tri_bf, masks), e in zip(chains, es):
        qs, ks = [], []
        for h in blocks:
            if h == 0:
                qs.append(q.astype(BF16))
                ks.append(kg.astype(BF16))
                continue
            if h == 1:
                in_hi = (rloc % 2) >= 1
                w = jnp.where(in_hi, q * fg, kg) if fwd else jnp.where(in_hi, kg, q * fg)
            else:
                r = h - 1 if fwd else h
                if 2 * h >= SUBLANES:
                    e_ref = _bcast_row_in_blocks(e, 2 * h, r)
                else:
                    e_ref = jnp.where(rloc < 2 * h, _bcast_row_in_blocks(e, SUBLANES, r),
                                      _bcast_row_in_blocks(e, SUBLANES, 2 * h + r))
                w = _split_roles(q, kg, h, fwd, rloc) * jnp.exp(-jnp.abs(e - e_ref))
            qs.append(w.astype(BF16))
            ks.append(qs[-1])
        pad = [jnp.zeros((c, d), BF16)] * (n_groups * per_group - len(blocks))
        sides.append((qs + pad, ks + pad))

    atts = [[None] * n_kvregs for _ in chains]
    for g in range(n_groups):
        for ci, (qs, ks) in enumerate(sides):
            sl = slice(g * per_group, (g + 1) * per_group)
            prod = lax.dot_general(jnp.concatenate(qs[sl], axis=0), jnp.concatenate(ks[sl], axis=0),
                                   NT_DIMS, preferred_element_type=F32)
            masks = chains[ci][8]
            for b in range(g * per_group, min((g + 1) * per_group, len(blocks))):
                t = b * c // LANES
                col = (t * LANES) % MXU_WIDTH
                row = (b - g * per_group) * c
                term = prod[row:row + c, col:col + LANES] * masks[b]
                atts[ci][t] = term if atts[ci][t] is None else atts[ci][t] + term

    outs = []
    for (q, kg, fg, v, logf, st, fwd, tri_bf, masks), e, att in zip(chains, es, atts):
        vb = v.astype(BF16)
        v_rep = jnp.concatenate([vb] * (n_kvregs * LANES // c), axis=0)
        o = jnp.dot(jnp.concatenate(att, axis=1).astype(BF16), v_rep, preferred_element_type=F32)
        o = o + lax.dot_general((q * jnp.exp(e)).astype(BF16), st.astype(BF16), NT_DIMS,
                                preferred_element_type=F32)
        e_edge = e[c - 1:c] if fwd else e[0:1]
        kh = (kg * jnp.exp(e_edge - e)).astype(BF16)
        st_new = st * jnp.exp(e_edge) + lax.dot_general(vb, kh, TN_DIMS, preferred_element_type=F32)
        outs.append((o, st_new))
    return outs


def _gla_kernel(*refs, seq_len, chunk, has_s0, final):
    refs = list(refs)
    q_ref, ffw_ref, fbw_ref, v_ref, g_ref, lb_ref, on_ref = refs[:7]
    del refs[:7]
    s0_ref = refs.pop(0) if has_s0 else None
    del refs[:2 if final == "join" else 1]
    o_ref = refs.pop(0)
    sf_ref = _final_state_slot(refs.pop(0), final) if final is not None else None
    stf_ref, stb_ref, oacc_ref = refs
    c = chunk
    nch = seq_len // c
    half_n = nch // 2
    d = A_HEAD_DIM
    n_h = q_ref.shape[-1] // d

    i = lax.broadcasted_iota(jnp.int32, (c, c), 0)
    j = lax.broadcasted_iota(jnp.int32, (c, c), 1)
    tri_f = jnp.where(i >= j, 1.0, 0.0).astype(BF16)
    tri_b = jnp.where(i <= j, 1.0, 0.0).astype(BF16)
    masks_f = _gla_masks(c, True)
    masks_b = _gla_masks(c, False)
    rloc = lax.broadcasted_iota(jnp.int32, (c, d), 0) % SUBLANES
    q_scale = d ** -0.5

    def gates(x, lb):
        one_m_lb = 1.0 - lb
        t = jnp.exp(-jnp.abs(x))
        r = 1.0 / (1.0 + t)
        pos = x >= 0.0
        fg = lb + one_m_lb * jnp.where(pos, r, t * r)
        kg = one_m_lb * jnp.where(pos, t * r, r)
        logf = jnp.maximum(jnp.log(fg), jnp.log1p(-lb) + jnp.minimum(x, 0.0) - jnp.log(1.0 + t))
        return fg, kg, logf

    def all_chains(t):
        where, chains = [], []
        for hh in range(n_h):
            cols = slice(hh * d, (hh + 1) * d)
            for fwd in (True, False):
                sl = pl.ds(pl.multiple_of((t if fwd else nch - 1 - t) * c, c), c)
                fg, kg, logf = gates((ffw_ref if fwd else fbw_ref)[sl, cols], lb_ref[:, cols])
                q = _silu(q_ref[sl, cols]) * q_scale
                where.append((sl, cols))
                chains.append((q, kg, fg, v_ref[sl, cols], logf, (stf_ref if fwd else stb_ref)[hh], fwd,
                               tri_f if fwd else tri_b, masks_f if fwd else masks_b))
        outs = _gla_chunks(chains, rloc)
        for n, (o, st) in enumerate(outs):
            (stf_ref if n % 2 == 0 else stb_ref)[n // 2] = st
        return [(sl, cols, o) for (sl, cols), (o, st) in zip(where, outs)]

    def finish(sl, cols, o):
        y = oacc_ref[sl, cols] + o
        y = y * lax.rsqrt(jnp.mean(y * y, axis=-1, keepdims=True) + EPS) * on_ref[...]
        o_ref[sl, cols] = (y * _silu(g_ref[sl, cols])).astype(o_ref.dtype)

    def first_half(t, carry):
        for sl, cols, o in all_chains(t):
            oacc_ref[sl, cols] = o
        return carry

    def second_half(t, carry):
        for sl, cols, o in all_chains(t):
            finish(sl, cols, o)
        return carry

    for hh in range(n_h):
        for st_ref, direction in ((stf_ref, 0), (stb_ref, 1)):
            st_ref[hh] = s0_ref[direction, hh].T if has_s0 else jnp.zeros((d, d), F32)
    lax.fori_loop(0, half_n, first_half, 0)
    lax.fori_loop(half_n, nch, second_half, 0)
    if final is not None:
        for hh in range(n_h):
            sf_ref[0, hh] = stf_ref[hh].T
            sf_ref[1, hh] = stb_ref[hh].T


GLA_CHUNK = 64


def _join_buffers(bufs, in_specs, args):
    shapes, aliases = [], {}
    for n, buf in enumerate(bufs):
        in_specs.append(pl.BlockSpec(memory_space=pl.ANY))
        args.append(buf)
        shapes.append(jax.ShapeDtypeStruct(buf.shape, buf.dtype))
        aliases[len(args) - 1] = n
    return shapes, aliases


def _final_state_specs(finals, slot, dims, heads_per_step, out_specs):
    n_seq, heads, d = dims
    if finals is None:
        return [], None
    if isinstance(finals, int):
        out_specs.append(pl.BlockSpec((None, finals, 2, heads_per_step, d, d),
                                      lambda n, h: (n, 0, 0, h, 0, 0)))
        return [], (slot, finals)
    out_specs.append(pl.BlockSpec((None, None, 2, heads_per_step, d, d),
                                  lambda n, h: (n, slot, 0, h, 0, 0)))
    return [finals], "join"


def _final_state_slot(sf_ref, final):
    if final == "join":
        return sf_ref
    slot, n_slots = final
    for s in range(n_slots):
        if s != slot:
            sf_ref[s] = jnp.zeros(sf_ref.shape[1:], sf_ref.dtype)
    return sf_ref.at[slot]


def _gla_heads_per_step(seq_len):
    for n_h in (4, 2, 1):
        if 2 * 5 * seq_len * n_h * A_HEAD_DIM * 4 + 4 * seq_len * n_h * A_HEAD_DIM * 2 <= VMEM_TILE_BUDGET:
            return n_h
    raise ValueError(f"sequence of {seq_len} rows does not fit VMEM")


def _gla_call(p, lower_bounds, layer, onorm, eidx, s0, n_seq, seq_len, row0, finals, mix):
    d = A_HEAD_DIM
    n_h = _gla_heads_per_step(seq_len)
    w = n_h * d
    a_width = lower_bounds.shape[-1]
    n_hb = a_width // w
    assert row0 % seq_len == 0 and seq_len % (2 * GLA_CHUNK) == 0 and a_width % w == 0
    sb = row0 // seq_len
    slab = lambda k: pl.BlockSpec((seq_len, w), lambda n, h: (n + sb, k * n_hb + h))
    in_specs = [slab(0), slab(1), slab(2), slab(3), slab(4),
                pl.BlockSpec((None, 1, w), lambda n, h: (layer, 0, h)),
                pl.BlockSpec((None, 1, d), lambda n, h: (eidx, 0, 0))]
    args = [p, p, p, p, p, lower_bounds.reshape(lower_bounds.shape[0], 1, a_width),
            onorm.reshape(onorm.shape[0], 1, d)]
    if s0 is not None:
        in_specs.append(pl.BlockSpec((None, None, 2, n_h, d, d), lambda n, h: (n, eidx, 0, h, 0, 0)))
        args.append(s0)
    out_specs = [pl.BlockSpec((seq_len, w), lambda n, h: (n + sb, h))]
    joined, final = _final_state_specs(finals, eidx, (n_seq, a_width // d, d), n_h, out_specs)
    out_shape, aliases = _join_buffers([mix] + joined, in_specs, args)
    if isinstance(final, tuple):
        out_shape.append(jax.ShapeDtypeStruct((n_seq, final[1], 2, a_width // d, d, d), F32))
    kern = functools.partial(_gla_kernel, seq_len=seq_len, chunk=GLA_CHUNK,
                             has_s0=s0 is not None, final=final)
    return pl.pallas_call(
        kern,
        grid=(n_seq, n_hb),
        in_specs=in_specs,
        out_specs=out_specs,
        out_shape=out_shape,
        input_output_aliases=aliases,
        scratch_shapes=[pltpu.VMEM((n_h, d, d), F32), pltpu.VMEM((n_h, d, d), F32),
                        pltpu.VMEM((seq_len, w), F32)],
        compiler_params=_params("arbitrary", "arbitrary"),
    )(*args)


def _ret_kernel(*refs, seq_len, chunk, has_s0, final, rope):
    refs = list(refs)
    q_ref, k_ref, v_ref, g_ref, dl_ref, gn_ref = refs[:6]
    del refs[:6]
    cos_ref = refs.pop(0) if rope else None
    sin_ref = refs.pop(0) if rope else None
    s0_ref = refs.pop(0) if has_s0 else None
    del refs[:2 if final == "join" else 1]
    o_ref = refs.pop(0)
    sf_ref = _final_state_slot(refs.pop(0), final) if final is not None else None
    stf_ref, stb_ref, oacc_ref = refs
    c = chunk
    nch = seq_len // c
    half_n = nch // 2
    d = q_ref.shape[-1]
    half = d // 2
    k_scale = d ** -0.5

    lg_f = _log_sigmoid(dl_ref[0])
    lg_b = _log_sigmoid(dl_ref[1])
    dist = (lax.broadcasted_iota(jnp.int32, (c, c), 0)
            - lax.broadcasted_iota(jnp.int32, (c, c), 1)).astype(F32)
    low = dist >= 0.0
    up = dist <= 0.0
    intra = (jnp.where(low, jnp.exp(jnp.where(low, dist, 0.0) * lg_f[:, :c]), 0.0)
             + jnp.where(up, jnp.exp(jnp.where(up, -dist, 0.0) * lg_b[:, :c]), 0.0))
    row = lax.broadcasted_iota(jnp.int32, (c, d), 0).astype(F32)
    q_dec_f = jnp.exp((row + 1.0) * lg_f)
    k_dec_f = jnp.exp((c - 1.0 - row) * lg_f)
    q_dec_b = jnp.exp((c - row) * lg_b)
    k_dec_b = jnp.exp(row * lg_b)
    c_dec_f = jnp.exp(c * lg_f)
    c_dec_b = jnp.exp(c * lg_b)

    def rotate(x, sl):
        if not rope:
            return x
        cos, sin = cos_ref[sl, :], sin_ref[sl, :]
        x1, x2 = x[:, :half], x[:, half:]
        return jnp.concatenate([x1 * cos - x2 * sin, x1 * sin + x2 * cos], axis=-1)

    def load(ci):
        sl = pl.ds(pl.multiple_of(ci * c, c), c)
        return sl, rotate(q_ref[sl, :], sl), rotate(k_ref[sl, :] * k_scale, sl), v_ref[sl, :].astype(BF16)

    def from_left(ci):
        sl, q, k, vb = load(ci)
        st = stf_ref[...]
        att = lax.dot_general(q.astype(BF16), k.astype(BF16), NT_DIMS, preferred_element_type=F32) * intra
        o = jnp.dot(att.astype(BF16), vb, preferred_element_type=F32)
        o = o + jnp.dot((q * q_dec_f).astype(BF16), st.astype(BF16), preferred_element_type=F32)
        stf_ref[...] = c_dec_f * st + lax.dot_general((k * k_dec_f).astype(BF16), vb, TN_DIMS,
                                                      preferred_element_type=F32)
        return sl, o

    def from_right(ci):
        sl, q, k, vb = load(ci)
        st = stb_ref[...]
        o = jnp.dot((q * q_dec_b).astype(BF16), st.astype(BF16), preferred_element_type=F32)
        stb_ref[...] = c_dec_b * st + lax.dot_general((k * k_dec_b).astype(BF16), vb, TN_DIMS,
                                                      preferred_element_type=F32)
        return sl, o

    def finish(sl, o):
        y = oacc_ref[sl, :] + o
        yc = y - jnp.mean(y, axis=-1, keepdims=True)
        yn = yc * lax.rsqrt(jnp.mean(yc * yc, axis=-1, keepdims=True) + EPS)
        o_ref[sl, :] = (yn * gn_ref[...] * _silu(g_ref[sl, :])).astype(o_ref.dtype)

    def first_half(t, carry):
        for sl, o in (from_left(t), from_right(nch - 1 - t)):
            oacc_ref[sl, :] = o
        return carry

    def second_half(t, carry):
        finish(*from_left(t))
        finish(*from_right(nch - 1 - t))
        return carry

    for st_ref, direction in ((stf_ref, 0), (stb_ref, 1)):
        st_ref[...] = s0_ref[direction] if has_s0 else jnp.zeros((d, d), F32)
    lax.fori_loop(0, half_n, first_half, 0)
    lax.fori_loop(half_n, nch, second_half, 0)
    if final is not None:
        sf_ref[0, 0] = stf_ref[...]
        sf_ref[1, 0] = stb_ref[...]


def _ret_call(p, col0, decay_logit, gnorm, eidx, rope_tabs, s0, n_seq, seq_len, row0, finals,
              mix, mix_col0, chunk=128):
    n_heads = decay_logit.shape[-1]
    b_width = gnorm.shape[-1]
    d = b_width // n_heads
    assert row0 % seq_len == 0 and seq_len % (2 * chunk) == 0 and col0 % d == 0 and chunk <= d
    assert mix_col0 % d == 0
    sb = row0 // seq_len
    cb0 = col0 // d
    ocb0 = mix_col0 // d
    slab = lambda k: pl.BlockSpec((seq_len, d), lambda n, h: (n + sb, cb0 + k * n_heads + h))
    dl = jnp.broadcast_to(decay_logit[:, :, :, None, None], decay_logit.shape + (1, d))
    in_specs = [slab(0), slab(1), slab(2), slab(3),
                pl.BlockSpec((None, 2, None, 1, d), lambda n, h: (eidx, 0, h, 0, 0)),
                pl.BlockSpec((None, 1, d), lambda n, h: (eidx, 0, h))]
    args = [p, p, p, p, dl, gnorm.reshape(gnorm.shape[0], 1, b_width)]
    if rope_tabs is not None:
        tab = pl.BlockSpec((seq_len, d // 2), lambda n, h: (0, 0))
        in_specs += [tab, tab]
        args += list(rope_tabs)
    if s0 is not None:
        in_specs.append(pl.BlockSpec((None, None, 2, None, d, d), lambda n, h: (n, eidx, 0, h, 0, 0)))
        args.append(s0)
    out_specs = [pl.BlockSpec((seq_len, d), lambda n, h: (n + sb, ocb0 + h))]
    joined, final = _final_state_specs(finals, eidx, (n_seq, n_heads, d), 1, out_specs)
    out_shape, aliases = _join_buffers([mix] + joined, in_specs, args)
    if isinstance(final, tuple):
        out_shape.append(jax.ShapeDtypeStruct((n_seq, final[1], 2, n_heads, d, d), F32))
    kern = functools.partial(_ret_kernel, seq_len=seq_len, chunk=chunk, has_s0=s0 is not None,
                             final=final, rope=rope_tabs is not None)
    return pl.pallas_call(
        kern,
        grid=(n_seq, n_heads),
        in_specs=in_specs,
        out_specs=out_specs,
        out_shape=out_shape,
        input_output_aliases=aliases,
        scratch_shapes=[pltpu.VMEM((d, d), F32), pltpu.VMEM((d, d), F32), pltpu.VMEM((seq_len, d), F32)],
        compiler_params=_params("arbitrary", "arbitrary"),
    )(*args)


Q_PRESCALE = ATTN_HEAD_DIM ** -0.5 * math.log2(math.e)


def _head_norm(x, gain):
    return x * lax.rsqrt(jnp.mean(x * x, axis=-1, keepdims=True) + EPS) * gain


def _rotate(y, cos, sin):
    return y * cos + pltpu.roll(y, y.shape[-1] // 2, 1) * sin


def _mm_qproj_kernel(a_ref, w_ref, gain_ref, cos_ref, sin_ref, o_ref, wbf_ref):
    _cast_weights_once((w_ref,), (wbf_ref,))
    y = jnp.dot(a_ref[...], wbf_ref[...], preferred_element_type=F32)
    hd = ATTN_HEAD_DIM
    for h in range(y.shape[-1] // hd):
        cols = slice(h * hd, (h + 1) * hd)
        z = _rotate(_head_norm(y[:, cols], gain_ref[...]), cos_ref[...], sin_ref[...])
        o_ref[:, cols] = (z * Q_PRESCALE).astype(BF16)


def _mm_qproj_call(a, w, widx, n_cols, q_norm, rope_rows):
    m, k = a.shape
    hd = ATTN_HEAD_DIM
    tm, tn = _matmul_tiles(m, k, n_cols, 1, 2, False, m)
    tab = pl.BlockSpec((tm, hd), lambda j, i: (i, 0))
    return pl.pallas_call(
        _mm_qproj_kernel,
        grid=(n_cols // tn, m // tm),
        in_specs=[
            pl.BlockSpec((tm, k), lambda j, i: (i, 0)),
            pl.BlockSpec((None, k, tn), lambda j, i: (widx, 0, j)),
            pl.BlockSpec((None, 1, hd), lambda j, i: (widx, 0, 0)),
            tab, tab,
        ],
        out_specs=pl.BlockSpec((tm, tn), lambda j, i: (i, j)),
        out_shape=jax.ShapeDtypeStruct((m, n_cols), BF16),
        scratch_shapes=[pltpu.VMEM((k, tn), BF16)],
        compiler_params=_params("arbitrary", "arbitrary"),
    )(a, w, q_norm.reshape(-1, 1, hd), *rope_rows)


def _mm_kvproj_kernel(a_ref, w_ref, gain_ref, cos_ref, sin_ref, k_out, v_out, kf_out, vf_out, wbf_ref):
    _cast_weights_once((w_ref,), (wbf_ref,))
    y = jnp.dot(a_ref[...], wbf_ref[...], preferred_element_type=F32)
    hd = ATTN_HEAD_DIM
    kvw = y.shape[-1] // 2
    for h in range(kvw // hd):
        cols = slice(h * hd, (h + 1) * hd)
        z = _rotate(_head_norm(y[:, cols], gain_ref[...]), cos_ref[...], sin_ref[...])
        k_out[:, cols] = z.astype(BF16)
        kf_out[:, cols] = z
    v = y[:, kvw:]
    v_out[...] = v.astype(BF16)
    vf_out[...] = v


def _mm_kvproj_call(a, w, widx, col0, k_norm, rope_rows):
    m, k = a.shape
    hd = ATTN_HEAD_DIM
    kvw = ATTN_KV_HEADS * hd
    tn = 2 * kvw
    assert col0 % tn == 0
    tm, _ = _matmul_tiles(m, k, tn, 1, 2 + 4, False, m)
    tab = pl.BlockSpec((tm, hd), lambda j, i: (i, 0))
    row_spec = pl.BlockSpec((tm, kvw), lambda j, i: (i, 0))
    return pl.pallas_call(
        _mm_kvproj_kernel,
        grid=(1, m // tm),
        in_specs=[pl.BlockSpec((tm, k), lambda j, i: (i, 0)),
                  pl.BlockSpec((None, k, tn), lambda j, i: (widx, 0, col0 // tn)),
                  pl.BlockSpec((None, 1, hd), lambda j, i: (widx, 0, 0)),
                  tab, tab],
        out_specs=[row_spec] * 4,
        out_shape=[jax.ShapeDtypeStruct((m, kvw), BF16)] * 2 + [jax.ShapeDtypeStruct((m, kvw), F32)] * 2,
        scratch_shapes=[pltpu.VMEM((k, tn), BF16)],
        compiler_params=_params("arbitrary", "arbitrary"),
    )(a, w, k_norm.reshape(-1, 1, hd), *rope_rows)


def _attn_kernel(*refs, group):
    q_ref, k_ref, v_ref, _, o_ref = refs
    hd = ATTN_HEAD_DIM
    k = k_ref[...]
    v = v_ref[...]
    v_ext = jnp.concatenate([v, jnp.ones_like(v)], axis=1)
    for g in range(group):
        q = q_ref[:, g * hd:(g + 1) * hd]
        s = lax.dot_general(q, k, NT_DIMS, preferred_element_type=F32)
        p = jnp.exp2(s - jnp.max(s, axis=-1, keepdims=True))
        o = jnp.dot(p.astype(BF16), v_ext, preferred_element_type=F32)
        o_ref[:, g * hd:(g + 1) * hd] = (o[:, :hd] / o[:, hd:]).astype(o_ref.dtype)


def _attn_call(q, k, v, n_seq, seq_len, row0, mix, tq=256):
    kvw = k.shape[-1]
    qw = q.shape[-1]
    hd = ATTN_HEAD_DIM
    n_kv = kvw // hd
    group = qw // hd // n_kv
    assert row0 % tq == 0 and seq_len % tq == 0 and row0 % seq_len == 0
    per_seq = seq_len // tq
    rb0 = row0 // tq
    q_spec = pl.BlockSpec((tq, group * hd), lambda n, kh, i: (rb0 + n * per_seq + i, kh))
    if k.ndim == 2:
        kv_spec = pl.BlockSpec((seq_len, hd), lambda n, kh, i: (row0 // seq_len + n, kh))
    else:
        kv_spec = pl.BlockSpec((None, k.shape[1], hd), lambda n, kh, i: (n, 0, kh))
    in_specs = [q_spec, kv_spec, kv_spec]
    args = [q, k, v]
    (mix_shape,), aliases = _join_buffers([mix], in_specs, args)
    return pl.pallas_call(
        functools.partial(_attn_kernel, group=group),
        grid=(n_seq, n_kv, per_seq),
        in_specs=in_specs,
        out_specs=q_spec,
        out_shape=mix_shape,
        input_output_aliases=aliases,
        compiler_params=_params("arbitrary", "arbitrary", "arbitrary"),
    )(*args)


def _grid_rope(n_tokens, head_dim):
    rows = n_tokens // GRID_W
    row = jnp.broadcast_to(jnp.arange(rows)[:, None], (rows, GRID_W)).reshape(-1).astype(F32)
    col = jnp.broadcast_to(jnp.arange(GRID_W)[None, :], (rows, GRID_W)).reshape(-1).astype(F32)
    per_axis = head_dim // 4
    freqs = ROPE_THETA ** (-jnp.arange(per_axis, dtype=F32) / per_axis)
    ang = jnp.concatenate([row[:, None] * freqs, col[:, None] * freqs], axis=-1)
    return jnp.cos(ang), jnp.sin(ang)


def kernel(x_prompt, x_sample, c, state_hgrn, state_ret, cache_k, cache_v, c_ctx, w_mod, b_mod, norm_mix, norm_ffn, w_in_even, w_out_even, hgrn_lb_logits, hgrn_onorm, ret_decay_logit, ret_gnorm, w_in_attn, w_out_attn, q_norm, k_norm, w_ffn_gate, w_ffn_up, w_ffn_down):
    n_p, l_p, d = x_prompt.shape
    n_s, l_s, _ = x_sample.shape
    depth = w_mod.shape[0]
    rows_p = n_p * l_p
    rows_s = n_s * l_s
    a_width = hgrn_lb_logits.shape[-1]
    b_width = ret_gnorm.shape[-1]
    kv_width = ATTN_KV_HEADS * ATTN_HEAD_DIM
    row_quantum = math.gcd(rows_p, l_s)

    def group_of_row(r):
        return jnp.where(r < rows_p, 0, 1 + (r - rows_p) // l_s)

    n_groups = 1 + n_s
    cond = jnp.concatenate([c_ctx[None, :], c, jnp.zeros((-(1 + n_s) % SUBLANES, d), F32)], axis=0)
    mods = _adaln_call(cond, w_mod, b_mod)
    mods = mods[:, :n_groups].reshape(depth, n_groups, 6, 1, d)

    cos_a, sin_a = _grid_rope(l_s, ATTN_HEAD_DIM)
    rope_attn = (jnp.concatenate([cos_a, cos_a], axis=-1), jnp.concatenate([-sin_a, sin_a], axis=-1))
    rope_rows = (jnp.concatenate([jnp.ones((rows_p, ATTN_HEAD_DIM), F32), jnp.tile(rope_attn[0], (n_s, 1))]),
                 jnp.concatenate([jnp.zeros((rows_p, ATTN_HEAD_DIM), F32), jnp.tile(rope_attn[1], (n_s, 1))]))
    rope_ret = _grid_rope(l_s, b_width // B_HEADS)
    lb_cum = jnp.cumsum(jax.nn.softmax(hgrn_lb_logits.astype(F32), axis=0), axis=0)
    lower_bounds = lb_cum - lb_cum[0]

    x = jnp.concatenate([x_prompt.reshape(rows_p, d), x_sample.reshape(rows_s, d)], axis=0)
    n_even = w_in_even.shape[0]
    out_hgrn, out_ret = n_even, n_even
    new_k, new_v = [], []
    for l in range(depth):
        mix = _modulate_call(x, norm_mix, mods, l, 0, group_of_row)
        if l % 2 == 0:
            e = l // 2
            p = _mm_plain_call(mix, w_in_even, e, F32)
            mix, out_hgrn = _gla_call(p, lower_bounds, l, hgrn_onorm, e, None, n_p, l_p, 0, out_hgrn, mix)
            (mix,) = _gla_call(p, lower_bounds, l, hgrn_onorm, e, state_hgrn, n_s, l_s, rows_p, None, mix)
            mix, out_ret = _ret_call(p, 5 * a_width, ret_decay_logit, ret_gnorm, e, None, None,
                                     n_p, l_p, 0, out_ret, mix, a_width)
            (mix,) = _ret_call(p, 5 * a_width, ret_decay_logit, ret_gnorm, e, rope_ret, state_ret,
                               n_s, l_s, rows_p, None, mix, a_width)
            w_out, widx = w_out_even, e
        else:
            o = l // 2
            q = _mm_qproj_call(mix, w_in_attn, o, d, q_norm, rope_rows)
            k, v, kf, vf = _mm_kvproj_call(mix, w_in_attn, o, d, k_norm, rope_rows)
            past = cache_k.shape[2]
            with_cache = lambda new, cache: jnp.concatenate(
                [new[rows_p:].reshape(n_s, l_s, kv_width),
                 cache[:, o].reshape(n_s, past, kv_width).astype(BF16)], axis=1)
            k_all, v_all = with_cache(k, cache_k), with_cache(v, cache_v)
            mix = _attn_call(q, k, v, n_p, l_p, 0, mix)
            mix = _attn_call(q, k_all, v_all, n_s, l_s, rows_p, mix)
            w_out, widx = w_out_attn, o
            new_k.append(kf[:rows_p].reshape(n_p, l_p, ATTN_KV_HEADS, ATTN_HEAD_DIM))
            new_v.append(vf[:rows_p].reshape(n_p, l_p, ATTN_KV_HEADS, ATTN_HEAD_DIM))
        x, h = _mm_residual_modulate_call(mix, w_out, widx, x, mods, l, 2, norm_ffn, 3, group_of_row,
                                          row_quantum)
        u = _mm_swiglu_call(h, w_ffn_gate, w_ffn_up, l)
        x = _mm_residual_call(u, w_ffn_down, l, x, mods, l, 5, group_of_row, row_quantum)
    return (x[:rows_p].reshape(n_p, l_p, d), x[rows_p:].reshape(n_s, l_s, d),
            out_hgrn, out_ret, jnp.stack(new_k, axis=1), jnp.stack(new_v, axis=1))
```

```python
import functools
import math

import jax
import jax.numpy as jnp
from jax import lax
from jax.experimental import pallas as pl
from jax.experimental.pallas import tpu as pltpu

F32 = jnp.float32
BF16 = jnp.bfloat16

EPS = 1e-6
ROPE_THETA = 10000.0
GRID_W = 64
A_HEAD_DIM = 128
B_HEADS = 4
ATTN_HEAD_DIM = 128
ATTN_KV_HEADS = 4
LANES = 128
SUBLANES = 8
VMEM_LIMIT = 56 * 1024 * 1024
VMEM_TILE_BUDGET = 46 * 1024 * 1024

LOG2_E = math.log2(math.e)

NT_DIMS = (((1,), (1,)), ((), ()))
TN_DIMS = (((0,), (0,)), ((), ()))


def _params(*sem):
    return pltpu.CompilerParams(dimension_semantics=sem, vmem_limit_bytes=VMEM_LIMIT)


def _silu(x):
    return x / (1.0 + jnp.exp(-x))


def _log_sigmoid(x):
    return jnp.minimum(x, 0.0) - jnp.log1p(jnp.exp(-jnp.abs(x)))


def _adaln_kernel(c_ref, w_ref, b_ref, o_ref):
    s = _silu(c_ref[...]).astype(BF16)
    o_ref[...] = jnp.dot(s, w_ref[...].astype(BF16), preferred_element_type=F32) + b_ref[...]


def _adaln_call(cond, w_mod, b_mod, tn=1024):
    depth, d, n = w_mod.shape
    rows = cond.shape[0]
    return pl.pallas_call(
        _adaln_kernel,
        grid=(depth, n // tn),
        in_specs=[
            pl.BlockSpec((rows, d), lambda l, j: (0, 0)),
            pl.BlockSpec((None, d, tn), lambda l, j: (l, 0, j)),
            pl.BlockSpec((None, 1, tn), lambda l, j: (l, 0, j)),
        ],
        out_specs=pl.BlockSpec((None, rows, tn), lambda l, j: (l, 0, j)),
        out_shape=jax.ShapeDtypeStruct((depth, rows, n), F32),
        compiler_params=_params("arbitrary", "arbitrary"),
    )(cond, w_mod, b_mod.reshape(depth, 1, n))


def _modulate_kernel(x_ref, g_ref, shift_ref, scale_ref, o_ref):
    x = x_ref[...]
    y = x * lax.rsqrt(jnp.mean(x * x, axis=-1, keepdims=True) + EPS) * g_ref[...]
    o_ref[...] = (y * (1.0 + scale_ref[...]) + shift_ref[...]).astype(BF16)


def _modulate_call(x, gains, mods, layer, shift_idx, group_of_row, tr=256):
    m, d = x.shape
    depth = gains.shape[0]
    grp = lambda i: group_of_row(i * tr)
    return pl.pallas_call(
        _modulate_kernel,
        grid=(m // tr,),
        in_specs=[
            pl.BlockSpec((tr, d), lambda i: (i, 0)),
            pl.BlockSpec((None, 1, d), lambda i: (layer, 0, 0)),
            pl.BlockSpec((None, None, None, 1, d), lambda i: (layer, grp(i), shift_idx, 0, 0)),
            pl.BlockSpec((None, None, None, 1, d), lambda i: (layer, grp(i), shift_idx + 1, 0, 0)),
        ],
        out_specs=pl.BlockSpec((tr, d), lambda i: (i, 0)),
        out_shape=jax.ShapeDtypeStruct((m, d), BF16),
        compiler_params=_params("arbitrary"),
    )(x, gains.reshape(depth, 1, d), mods, mods)


def _matmul_tiles(m, k, n, n_weights, out_bytes, residual, row_quantum):
    for tn in (2048, 1024, 512, 256, 128):
        if n % tn:
            continue
        for tm in (1024, 512, 256):
            if m % tm or row_quantum % tm:
                continue
            need = (2 * tm * k * 2 + n_weights * (2 * k * tn * 4 + k * tn * 2)
                    + 2 * tm * tn * out_bytes + (2 * tm * tn * 4 if residual else 0))
            if need <= VMEM_TILE_BUDGET:
                return tm, tn
    raise ValueError(f"no matmul tiling fits VMEM for {(m, k, n)}")


def _cast_weights_once(w_refs, wbf_refs):
    @pl.when(pl.program_id(1) == 0)
    def _():
        for w_ref, wbf_ref in zip(w_refs, wbf_refs):
            wbf_ref[...] = w_ref[...].astype(BF16)


def _mm_plain_kernel(a_ref, w_ref, o_ref, wbf_ref):
    _cast_weights_once((w_ref,), (wbf_ref,))
    o_ref[...] = jnp.dot(a_ref[...], wbf_ref[...], preferred_element_type=F32).astype(o_ref.dtype)


def _mm_swiglu_kernel(a_ref, wg_ref, wu_ref, o_ref, wgbf_ref, wubf_ref):
    _cast_weights_once((wg_ref, wu_ref), (wgbf_ref, wubf_ref))
    a = a_ref[...]
    g = jnp.dot(a, wgbf_ref[...], preferred_element_type=F32)
    u = jnp.dot(a, wubf_ref[...], preferred_element_type=F32)
    o_ref[...] = (_silu(g) * u).astype(o_ref.dtype)


def _mm_residual_kernel(a_ref, w_ref, x_ref, gate_ref, o_ref, wbf_ref):
    _cast_weights_once((w_ref,), (wbf_ref,))
    y = jnp.dot(a_ref[...], wbf_ref[...], preferred_element_type=F32)
    o_ref[...] = x_ref[...] + gate_ref[...] * y


def _mm_residual_modulate_kernel(a_ref, w_ref, x_ref, gate_ref, g_ref, shift_ref, scale_ref,
                                 ox_ref, oh_ref, wbf_ref):
    @pl.when(pl.program_id(0) == 0)
    def _():
        wbf_ref[...] = w_ref[...].astype(BF16)
    y = jnp.dot(a_ref[...], wbf_ref[...], preferred_element_type=F32)
    x = x_ref[...] + gate_ref[...] * y
    ox_ref[...] = x
    h = x * lax.rsqrt(jnp.mean(x * x, axis=-1, keepdims=True) + EPS) * g_ref[...]
    oh_ref[...] = (h * (1.0 + scale_ref[...]) + shift_ref[...]).astype(BF16)


def _mm_residual_modulate_call(a, w, widx, x, mods, layer, gate_idx, gains, shift_idx, group_of_row,
                               row_quantum):
    m, k = a.shape
    n = w.shape[-1]
    depth = gains.shape[0]
    for tm in (512, 256, 128):
        need = k * n * (4 + 2) + 2 * tm * (k * 2 + n * (4 + 4 + 2))
        if m % tm == 0 and row_quantum % tm == 0 and need <= VMEM_TILE_BUDGET:
            break
    else:
        raise ValueError(f"no row tile fits VMEM for {(m, k, n)}")
    grp = lambda i: group_of_row(i * tm)
    mod_spec = lambda which: pl.BlockSpec((None, None, None, 1, n), lambda i: (layer, grp(i), which, 0, 0))
    row_spec = pl.BlockSpec((tm, n), lambda i: (i, 0))
    return pl.pallas_call(
        _mm_residual_modulate_kernel,
        grid=(m // tm,),
        in_specs=[
            pl.BlockSpec((tm, k), lambda i: (i, 0)),
            pl.BlockSpec((None, k, n), lambda i: (widx, 0, 0), pipeline_mode=pl.Buffered(1)),
            row_spec,
            mod_spec(gate_idx),
            pl.BlockSpec((None, 1, n), lambda i: (layer, 0, 0)),
            mod_spec(shift_idx),
            mod_spec(shift_idx + 1),
        ],
        out_specs=[row_spec, row_spec],
        out_shape=[jax.ShapeDtypeStruct((m, n), F32), jax.ShapeDtypeStruct((m, n), BF16)],
        scratch_shapes=[pltpu.VMEM((k, n), BF16)],
        compiler_params=_params("arbitrary"),
    )(a, w, x, mods, gains.reshape(depth, 1, n), mods, mods)


def _mm_plain_call(a, w, widx, out_dtype):
    m, k = a.shape
    n = w.shape[-1]
    tm, tn = _matmul_tiles(m, k, n, 1, jnp.dtype(out_dtype).itemsize, False, m)
    return pl.pallas_call(
        _mm_plain_kernel,
        grid=(n // tn, m // tm),
        in_specs=[
            pl.BlockSpec((tm, k), lambda j, i: (i, 0)),
            pl.BlockSpec((None, k, tn), lambda j, i: (widx, 0, j)),
        ],
        out_specs=pl.BlockSpec((tm, tn), lambda j, i: (i, j)),
        out_shape=jax.ShapeDtypeStruct((m, n), out_dtype),
        scratch_shapes=[pltpu.VMEM((k, tn), BF16)],
        compiler_params=_params("arbitrary", "arbitrary"),
    )(a, w)


def _mm_swiglu_call(a, wg, wu, widx):
    m, k = a.shape
    n = wg.shape[-1]
    tm, tn = _matmul_tiles(m, k, n, 2, 2, False, m)
    wspec = pl.BlockSpec((None, k, tn), lambda j, i: (widx, 0, j))
    return pl.pallas_call(
        _mm_swiglu_kernel,
        grid=(n // tn, m // tm),
        in_specs=[pl.BlockSpec((tm, k), lambda j, i: (i, 0)), wspec, wspec],
        out_specs=pl.BlockSpec((tm, tn), lambda j, i: (i, j)),
        out_shape=jax.ShapeDtypeStruct((m, n), BF16),
        scratch_shapes=[pltpu.VMEM((k, tn), BF16), pltpu.VMEM((k, tn), BF16)],
        compiler_params=_params("arbitrary", "arbitrary"),
    )(a, wg, wu)


def _mm_residual_call(a, w, widx, x, mods, layer, gate_idx, group_of_row, row_quantum, row0=0, m=None):
    k = a.shape[1]
    m = a.shape[0] if m is None else m
    n = w.shape[-1]
    tm, tn = _matmul_tiles(m, k, n, 1, 4, True, row_quantum)
    assert row0 % tm == 0
    rb0 = row0 // tm
    grp = lambda i: group_of_row((i + rb0) * tm)
    return pl.pallas_call(
        _mm_residual_kernel,
        grid=(n // tn, m // tm),
        in_specs=[
            pl.BlockSpec((tm, k), lambda j, i: (i + rb0, 0)),
            pl.BlockSpec((None, k, tn), lambda j, i: (widx, 0, j)),
            pl.BlockSpec((tm, tn), lambda j, i: (i + rb0, j)),
            pl.BlockSpec((None, None, None, 1, tn), lambda j, i: (layer, grp(i), gate_idx, 0, j)),
        ],
        out_specs=pl.BlockSpec((tm, tn), lambda j, i: (i, j)),
        out_shape=jax.ShapeDtypeStruct((m, n), F32),
        scratch_shapes=[pltpu.VMEM((k, tn), BF16)],
        compiler_params=_params("arbitrary", "arbitrary"),
    )(a, w, x, mods)


def _bcast_row_in_blocks(x, block, r):
    c, d = x.shape
    x3 = x.reshape(c // block, block, d)
    return jnp.broadcast_to(x3[:, r:r + 1, :], x3.shape).reshape(c, d)


MXU_WIDTH = 256


def _gla_blocks(c):
    return [1 << b for b in range((c // 2).bit_length())] + [0]


def _gla_masks(c, fwd):
    i = lax.broadcasted_iota(jnp.int32, (c, LANES), 0)
    lane = lax.broadcasted_iota(jnp.int32, (c, LANES), 1)
    masks = []
    for b, h in enumerate(_gla_blocks(c)):
        j = lane - (b * c) % LANES
        in_block = jnp.where(j >= 0, jnp.where(j < c, 1.0, 0.0), 0.0)
        if h == 0:
            ok = jnp.where(i == j, 1.0, 0.0)
        else:
            same = (i // (2 * h)) == (j // (2 * h))
            i_hi = (i % (2 * h)) >= h
            j_hi = (j % (2 * h)) >= h
            if fwd:
                ok = jnp.where(same, jnp.where(i_hi, jnp.where(j_hi, 0.0, 1.0), 0.0), 0.0)
            else:
                ok = jnp.where(same, jnp.where(i_hi, 0.0, jnp.where(j_hi, 1.0, 0.0)), 0.0)
        masks.append((ok * in_block).astype(F32))
    return masks


def _split_roles(q, kg, h, fwd, rloc):
    c = q.shape[0]
    if h >= SUBLANES:
        parts = []
        for b in range(c // (2 * h)):
            lo = slice(2 * h * b, 2 * h * b + h)
            hi = slice(2 * h * b + h, 2 * h * (b + 1))
            parts += [kg[lo], q[hi]] if fwd else [q[lo], kg[hi]]
        return jnp.concatenate(parts, axis=0)
    in_hi = (rloc % (2 * h)) >= h
    return jnp.where(in_hi, q, kg) if fwd else jnp.where(in_hi, kg, q)


def _split3(x):
    hi = x.astype(BF16)
    r1 = x - hi.astype(F32)
    mid = r1.astype(BF16)
    lo = (r1 - mid.astype(F32)).astype(BF16)
    return jnp.concatenate([hi, mid, lo], axis=-1)


def _gla_chunks(chains, rloc):
    c, d = chains[0][0].shape
    per_group = MXU_WIDTH // c
    blocks = _gla_blocks(c)
    n_groups = -(-len(blocks) // per_group)
    n_kvregs = -(-len(blocks) * c // LANES)

    es = []
    for q, kg, fg, v, logf3, st, fwd, tri_bf, masks in chains:
        e3 = jnp.dot(tri_bf, logf3, preferred_element_type=F32)
        es.append(e3[:, :d] + e3[:, d:2 * d] + e3[:, 2 * d:])

    sides = []
    for (q, kg, fg, v, logf, st, fwd, tri_bf, masks), e in zip(chains, es):
        qs, ks = [], []
        for h in blocks:
            if h == 0:
                qs.append(q.astype(BF16))
                ks.append(kg.astype(BF16))
                continue
            if h == 1:
                in_hi = (rloc % 2) >= 1
                w = jnp.where(in_hi, q * fg, kg) if fwd else jnp.where(in_hi, kg, q * fg)
            else:
                r = h - 1 if fwd else h
                if 2 * h >= SUBLANES:
                    e_ref = _bcast_row_in_blocks(e, 2 * h, r)
                else:
                    e_ref = jnp.where(rloc < 2 * h, _bcast_row_in_blocks(e, SUBLANES, r),
                                      _bcast_row_in_blocks(e, SUBLANES, 2 * h + r))
                w = _split_roles(q, kg, h, fwd, rloc) * (1.0 / jnp.exp2(jnp.abs(e - e_ref)))
            qs.append(w.astype(BF16))
            ks.append(qs[-1])
        pad = [jnp.zeros((c, d), BF16)] * (n_groups * per_group - len(blocks))
        sides.append((qs + pad, ks + pad))

    atts = [[None] * n_kvregs for _ in chains]
    for g in range(n_groups):
        for ci, (qs, ks) in enumerate(sides):
            sl = slice(g * per_group, (g + 1) * per_group)
            prod = lax.dot_general(jnp.concatenate(qs[sl], axis=0), jnp.concatenate(ks[sl], axis=0),
                                   NT_DIMS, preferred_element_type=F32)
            masks = chains[ci][8]
            for b in range(g * per_group, min((g + 1) * per_group, len(blocks))):
                t = b * c // LANES
                col = (t * LANES) % MXU_WIDTH
                row = (b - g * per_group) * c
                term = prod[row:row + c, col:col + LANES] * masks[b]
                atts[ci][t] = term if atts[ci][t] is None else atts[ci][t] + term

    outs = []
    for (q, kg, fg, v, logf, st, fwd, tri_bf, masks), e, att in zip(chains, es, atts):
        vb = v.astype(BF16)
        v_rep = jnp.concatenate([vb] * (n_kvregs * LANES // c), axis=0)
        o = jnp.dot(jnp.concatenate(att, axis=1).astype(BF16), v_rep, preferred_element_type=F32)
        o = o + lax.dot_general((q * jnp.exp2(e)).astype(BF16), st.astype(BF16), NT_DIMS,
                                preferred_element_type=F32)
        e_edge = e[c - 1:c] if fwd else e[0:1]
        kh = (kg * jnp.exp2(e_edge - e)).astype(BF16)
        st_new = st * jnp.exp2(e_edge) + lax.dot_general(vb, kh, TN_DIMS, preferred_element_type=F32)
        outs.append((o, st_new))
    return outs


def _gla_kernel(*refs, seq_len, chunk, has_s0, final):
    refs = list(refs)
    q_ref, ffw_ref, fbw_ref, v_ref, g_ref, lb_ref, on_ref = refs[:7]
    del refs[:7]
    s0_ref = refs.pop(0) if has_s0 else None
    del refs[:2 if final == "join" else 1]
    o_ref = refs.pop(0)
    sf_ref = _final_state_slot(refs.pop(0), final) if final is not None else None
    stf_ref, stb_ref, oacc_ref = refs
    c = chunk
    nch = seq_len // c
    half_n = nch // 2
    d = A_HEAD_DIM
    n_h = q_ref.shape[-1] // d

    i = lax.broadcasted_iota(jnp.int32, (c, c), 0)
    j = lax.broadcasted_iota(jnp.int32, (c, c), 1)
    tri_f = jnp.where(i >= j, 1.0, 0.0).astype(BF16)
    tri_b = jnp.where(i <= j, 1.0, 0.0).astype(BF16)
    masks_f = _gla_masks(c, True)
    masks_b = _gla_masks(c, False)
    rloc = lax.broadcasted_iota(jnp.int32, (c, d), 0) % SUBLANES
    q_scale = d ** -0.5

    def gates(x, lb):
        one_m_lb = 1.0 - lb
        t = jnp.exp(-jnp.abs(x))
        r = 1.0 / (1.0 + t)
        pos = x >= 0.0
        fg = lb + one_m_lb * jnp.where(pos, r, t * r)
        kg = one_m_lb * jnp.where(pos, t * r, r)
        log2f = jnp.maximum(jnp.log2(fg),
                            jnp.log2(one_m_lb) + jnp.minimum(x, 0.0) * LOG2_E - jnp.log2(1.0 + t))
        return fg, kg, log2f

    def chain_rows(t, fwd):
        return pl.ds(pl.multiple_of((t if fwd else nch - 1 - t) * c, c), c)

    chain_ids = [(hh, fwd) for hh in range(n_h) for fwd in (True, False)]

    def all_chains(t):
        where, chains = [], []
        for hh, fwd in chain_ids:
            sl, cols = chain_rows(t, fwd), slice(hh * d, (hh + 1) * d)
            fg, kg, log2f = gates((ffw_ref if fwd else fbw_ref)[sl, cols], lb_ref[:, cols])
            where.append((sl, cols))
            chains.append((_silu(q_ref[sl, cols]) * q_scale, kg, fg, v_ref[sl, cols], _split3(log2f),
                           (stf_ref if fwd else stb_ref)[hh], fwd,
                           tri_f if fwd else tri_b, masks_f if fwd else masks_b))
        outs = _gla_chunks(chains, rloc)
        for n, (o, st) in enumerate(outs):
            (stf_ref if n % 2 == 0 else stb_ref)[n // 2] = st
        return [(sl, cols, o) for (sl, cols), (o, st) in zip(where, outs)]

    def finish(sl, cols, o):
        y = oacc_ref[sl, cols] + o
        y = y * lax.rsqrt(jnp.mean(y * y, axis=-1, keepdims=True) + EPS) * on_ref[...]
        o_ref[sl, cols] = (y * _silu(g_ref[sl, cols])).astype(o_ref.dtype)

    def first_half(t, carry):
        for sl, cols, o in all_chains(t):
            oacc_ref[sl, cols] = o
        return carry

    def second_half(t, carry):
        for sl, cols, o in all_chains(t):
            finish(sl, cols, o)
        return carry

    for hh in range(n_h):
        for st_ref, direction in ((stf_ref, 0), (stb_ref, 1)):
            st_ref[hh] = s0_ref[direction, hh].T if has_s0 else jnp.zeros((d, d), F32)
    unroll = 2 if 2 * n_h < 8 and half_n % 2 == 0 else 1
    lax.fori_loop(0, half_n, first_half, 0, unroll=unroll)
    lax.fori_loop(half_n, nch, second_half, 0, unroll=unroll)
    if final is not None:
        for hh in range(n_h):
            sf_ref[0, hh] = stf_ref[hh].T
            sf_ref[1, hh] = stb_ref[hh].T


GLA_CHUNK = 64


def _join_buffers(bufs, in_specs, args):
    shapes, aliases = [], {}
    for n, buf in enumerate(bufs):
        in_specs.append(pl.BlockSpec(memory_space=pl.ANY))
        args.append(buf)
        shapes.append(jax.ShapeDtypeStruct(buf.shape, buf.dtype))
        aliases[len(args) - 1] = n
    return shapes, aliases


def _final_state_specs(finals, slot, dims, heads_per_step, out_specs):
    n_seq, heads, d = dims
    if finals is None:
        return [], None
    if isinstance(finals, int):
        out_specs.append(pl.BlockSpec((None, finals, 2, heads_per_step, d, d),
                                      lambda n, h: (n, 0, 0, h, 0, 0)))
        return [], (slot, finals)
    out_specs.append(pl.BlockSpec((None, None, 2, heads_per_step, d, d),
                                  lambda n, h: (n, slot, 0, h, 0, 0)))
    return [finals], "join"


def _final_state_slot(sf_ref, final):
    if final == "join":
        return sf_ref
    slot, n_slots = final
    for s in range(n_slots):
        if s != slot:
            sf_ref[s] = jnp.zeros(sf_ref.shape[1:], sf_ref.dtype)
    return sf_ref.at[slot]


def _gla_heads_per_step(seq_len):
    for n_h in (4, 2, 1):
        if 2 * 5 * seq_len * n_h * A_HEAD_DIM * 4 + 4 * seq_len * n_h * A_HEAD_DIM * 2 <= VMEM_TILE_BUDGET:
            return n_h
    raise ValueError(f"sequence of {seq_len} rows does not fit VMEM")


def _gla_call(p, lower_bounds, layer, onorm, eidx, s0, n_seq, seq_len, row0, finals, mix):
    d = A_HEAD_DIM
    n_h = _gla_heads_per_step(seq_len)
    w = n_h * d
    a_width = lower_bounds.shape[-1]
    n_hb = a_width // w
    assert row0 % seq_len == 0 and seq_len % (2 * GLA_CHUNK) == 0 and a_width % w == 0
    sb = row0 // seq_len
    slab = lambda k: pl.BlockSpec((seq_len, w), lambda n, h: (n + sb, k * n_hb + h))
    in_specs = [slab(0), slab(1), slab(2), slab(3), slab(4),
                pl.BlockSpec((None, 1, w), lambda n, h: (layer, 0, h)),
                pl.BlockSpec((None, 1, d), lambda n, h: (eidx, 0, 0))]
    args = [p, p, p, p, p, lower_bounds.reshape(lower_bounds.shape[0], 1, a_width),
            onorm.reshape(onorm.shape[0], 1, d)]
    if s0 is not None:
        in_specs.append(pl.BlockSpec((None, None, 2, n_h, d, d), lambda n, h: (n, eidx, 0, h, 0, 0)))
        args.append(s0)
    out_specs = [pl.BlockSpec((seq_len, w), lambda n, h: (n + sb, h))]
    joined, final = _final_state_specs(finals, eidx, (n_seq, a_width // d, d), n_h, out_specs)
    out_shape, aliases = _join_buffers([mix] + joined, in_specs, args)
    if isinstance(final, tuple):
        out_shape.append(jax.ShapeDtypeStruct((n_seq, final[1], 2, a_width // d, d, d), F32))
    kern = functools.partial(_gla_kernel, seq_len=seq_len, chunk=GLA_CHUNK,
                             has_s0=s0 is not None, final=final)
    return pl.pallas_call(
        kern,
        grid=(n_seq, n_hb),
        in_specs=in_specs,
        out_specs=out_specs,
        out_shape=out_shape,
        input_output_aliases=aliases,
        scratch_shapes=[pltpu.VMEM((n_h, d, d), F32), pltpu.VMEM((n_h, d, d), F32),
                        pltpu.VMEM((seq_len, w), F32)],
        compiler_params=_params("arbitrary", "arbitrary"),
    )(*args)


def _ret_kernel(*refs, seq_len, chunk, has_s0, final, rope):
    refs = list(refs)
    q_ref, k_ref, v_ref, g_ref, dl_ref, gn_ref = refs[:6]
    del refs[:6]
    cos_ref = refs.pop(0) if rope else None
    sin_ref = refs.pop(0) if rope else None
    s0_ref = refs.pop(0) if has_s0 else None
    del refs[:2 if final == "join" else 1]
    o_ref = refs.pop(0)
    sf_ref = _final_state_slot(refs.pop(0), final) if final is not None else None
    stf_ref, stb_ref, oacc_ref = refs
    c = chunk
    nch = seq_len // c
    half_n = nch // 2
    d = q_ref.shape[-1]
    half = d // 2
    k_scale = d ** -0.5

    lg_f = _log_sigmoid(dl_ref[0])
    lg_b = _log_sigmoid(dl_ref[1])
    dist = (lax.broadcasted_iota(jnp.int32, (c, c), 0)
            - lax.broadcasted_iota(jnp.int32, (c, c), 1)).astype(F32)
    low = dist >= 0.0
    up = dist <= 0.0
    intra = (jnp.where(low, jnp.exp(jnp.where(low, dist, 0.0) * lg_f[:, :c]), 0.0)
             + jnp.where(up, jnp.exp(jnp.where(up, -dist, 0.0) * lg_b[:, :c]), 0.0))
    row = lax.broadcasted_iota(jnp.int32, (c, d), 0).astype(F32)
    q_dec_f = jnp.exp((row + 1.0) * lg_f)
    k_dec_f = jnp.exp((c - 1.0 - row) * lg_f)
    q_dec_b = jnp.exp((c - row) * lg_b)
    k_dec_b = jnp.exp(row * lg_b)
    c_dec_f = jnp.exp(c * lg_f)
    c_dec_b = jnp.exp(c * lg_b)

    def rotate(x, sl):
        if not rope:
            return x
        cos, sin = cos_ref[sl, :], sin_ref[sl, :]
        x1, x2 = x[:, :half], x[:, half:]
        return jnp.concatenate([x1 * cos - x2 * sin, x1 * sin + x2 * cos], axis=-1)

    def load(ci):
        sl = pl.ds(pl.multiple_of(ci * c, c), c)
        return sl, rotate(q_ref[sl, :], sl), rotate(k_ref[sl, :] * k_scale, sl), v_ref[sl, :].astype(BF16)

    def from_left(ci):
        sl, q, k, vb = load(ci)
        st = stf_ref[...]
        att = lax.dot_general(q.astype(BF16), k.astype(BF16), NT_DIMS, preferred_element_type=F32) * intra
        o = jnp.dot(att.astype(BF16), vb, preferred_element_type=F32)
        o = o + jnp.dot((q * q_dec_f).astype(BF16), st.astype(BF16), preferred_element_type=F32)
        stf_ref[...] = c_dec_f * st + lax.dot_general((k * k_dec_f).astype(BF16), vb, TN_DIMS,
                                                      preferred_element_type=F32)
        return sl, o

    def from_right(ci):
        sl, q, k, vb = load(ci)
        st = stb_ref[...]
        o = jnp.dot((q * q_dec_b).astype(BF16), st.astype(BF16), preferred_element_type=F32)
        stb_ref[...] = c_dec_b * st + lax.dot_general((k * k_dec_b).astype(BF16), vb, TN_DIMS,
                                                      preferred_element_type=F32)
        return sl, o

    def finish(sl, o):
        y = oacc_ref[sl, :] + o
        yc = y - jnp.mean(y, axis=-1, keepdims=True)
        yn = yc * lax.rsqrt(jnp.mean(yc * yc, axis=-1, keepdims=True) + EPS)
        o_ref[sl, :] = (yn * gn_ref[...] * _silu(g_ref[sl, :])).astype(o_ref.dtype)

    def first_half(t, carry):
        for sl, o in (from_left(t), from_right(nch - 1 - t)):
            oacc_ref[sl, :] = o
        return carry

    def second_half(t, carry):
        finish(*from_left(t))
        finish(*from_right(nch - 1 - t))
        return carry

    for st_ref, direction in ((stf_ref, 0), (stb_ref, 1)):
        st_ref[...] = s0_ref[direction] if has_s0 else jnp.zeros((d, d), F32)
    lax.fori_loop(0, half_n, first_half, 0)
    lax.fori_loop(half_n, nch, second_half, 0)
    if final is not None:
        sf_ref[0, 0] = stf_ref[...]
        sf_ref[1, 0] = stb_ref[...]


def _ret_call(p, col0, decay_logit, gnorm, eidx, rope_tabs, s0, n_seq, seq_len, row0, finals,
              mix, mix_col0, chunk=128):
    n_heads = decay_logit.shape[-1]
    b_width = gnorm.shape[-1]
    d = b_width // n_heads
    assert row0 % seq_len == 0 and seq_len % (2 * chunk) == 0 and col0 % d == 0 and chunk <= d
    assert mix_col0 % d == 0
    sb = row0 // seq_len
    cb0 = col0 // d
    ocb0 = mix_col0 // d
    slab = lambda k: pl.BlockSpec((seq_len, d), lambda n, h: (n + sb, cb0 + k * n_heads + h))
    dl = jnp.broadcast_to(decay_logit[:, :, :, None, None], decay_logit.shape + (1, d))
    in_specs = [slab(0), slab(1), slab(2), slab(3),
                pl.BlockSpec((None, 2, None, 1, d), lambda n, h: (eidx, 0, h, 0, 0)),
                pl.BlockSpec((None, 1, d), lambda n, h: (eidx, 0, h))]
    args = [p, p, p, p, dl, gnorm.reshape(gnorm.shape[0], 1, b_width)]
    if rope_tabs is not None:
        tab = pl.BlockSpec((seq_len, d // 2), lambda n, h: (0, 0))
        in_specs += [tab, tab]
        args += list(rope_tabs)
    if s0 is not None:
        in_specs.append(pl.BlockSpec((None, None, 2, None, d, d), lambda n, h: (n, eidx, 0, h, 0, 0)))
        args.append(s0)
    out_specs = [pl.BlockSpec((seq_len, d), lambda n, h: (n + sb, ocb0 + h))]
    joined, final = _final_state_specs(finals, eidx, (n_seq, n_heads, d), 1, out_specs)
    out_shape, aliases = _join_buffers([mix] + joined, in_specs, args)
    if isinstance(final, tuple):
        out_shape.append(jax.ShapeDtypeStruct((n_seq, final[1], 2, n_heads, d, d), F32))
    kern = functools.partial(_ret_kernel, seq_len=seq_len, chunk=chunk, has_s0=s0 is not None,
                             final=final, rope=rope_tabs is not None)
    return pl.pallas_call(
        kern,
        grid=(n_seq, n_heads),
        in_specs=in_specs,
        out_specs=out_specs,
        out_shape=out_shape,
        input_output_aliases=aliases,
        scratch_shapes=[pltpu.VMEM((d, d), F32), pltpu.VMEM((d, d), F32), pltpu.VMEM((seq_len, d), F32)],
        compiler_params=_params("arbitrary", "arbitrary"),
    )(*args)


Q_PRESCALE = ATTN_HEAD_DIM ** -0.5 * LOG2_E


def _head_norm(x, gain):
    return x * lax.rsqrt(jnp.mean(x * x, axis=-1, keepdims=True) + EPS) * gain


def _rotate(y, cos, sin):
    return y * cos + pltpu.roll(y, y.shape[-1] // 2, 1) * sin


def _mm_qproj_kernel(a_ref, w_ref, gain_ref, cos_ref, sin_ref, o_ref, wbf_ref):
    _cast_weights_once((w_ref,), (wbf_ref,))
    y = jnp.dot(a_ref[...], wbf_ref[...], preferred_element_type=F32)
    hd = ATTN_HEAD_DIM
    for h in range(y.shape[-1] // hd):
        cols = slice(h * hd, (h + 1) * hd)
        z = _rotate(_head_norm(y[:, cols], gain_ref[...]), cos_ref[...], sin_ref[...])
        o_ref[:, cols] = (z * Q_PRESCALE).astype(BF16)


def _mm_qproj_call(a, w, widx, n_cols, q_norm, rope_rows):
    m, k = a.shape
    hd = ATTN_HEAD_DIM
    tm, tn = _matmul_tiles(m, k, n_cols, 1, 2, False, m)
    tab = pl.BlockSpec((tm, hd), lambda j, i: (i, 0))
    return pl.pallas_call(
        _mm_qproj_kernel,
        grid=(n_cols // tn, m // tm),
        in_specs=[
            pl.BlockSpec((tm, k), lambda j, i: (i, 0)),
            pl.BlockSpec((None, k, tn), lambda j, i: (widx, 0, j)),
            pl.BlockSpec((None, 1, hd), lambda j, i: (widx, 0, 0)),
            tab, tab,
        ],
        out_specs=pl.BlockSpec((tm, tn), lambda j, i: (i, j)),
        out_shape=jax.ShapeDtypeStruct((m, n_cols), BF16),
        scratch_shapes=[pltpu.VMEM((k, tn), BF16)],
        compiler_params=_params("arbitrary", "arbitrary"),
    )(a, w, q_norm.reshape(-1, 1, hd), *rope_rows)


def _mm_kvproj_kernel(a_ref, w_ref, gain_ref, cos_ref, sin_ref, k_out, v_out, kf_out, vf_out, wbf_ref):
    _cast_weights_once((w_ref,), (wbf_ref,))
    y = jnp.dot(a_ref[...], wbf_ref[...], preferred_element_type=F32)
    hd = ATTN_HEAD_DIM
    kvw = y.shape[-1] // 2
    for h in range(kvw // hd):
        cols = slice(h * hd, (h + 1) * hd)
        z = _rotate(_head_norm(y[:, cols], gain_ref[...]), cos_ref[...], sin_ref[...])
        k_out[:, cols] = z.astype(BF16)
        kf_out[:, cols] = z
    v = y[:, kvw:]
    v_out[...] = v.astype(BF16)
    vf_out[...] = v


def _mm_kvproj_call(a, w, widx, col0, k_norm, rope_rows):
    m, k = a.shape
    hd = ATTN_HEAD_DIM
    kvw = ATTN_KV_HEADS * hd
    tn = 2 * kvw
    assert col0 % tn == 0
    tm, _ = _matmul_tiles(m, k, tn, 1, 2 + 4, False, m)
    tab = pl.BlockSpec((tm, hd), lambda j, i: (i, 0))
    row_spec = pl.BlockSpec((tm, kvw), lambda j, i: (i, 0))
    return pl.pallas_call(
        _mm_kvproj_kernel,
        grid=(1, m // tm),
        in_specs=[pl.BlockSpec((tm, k), lambda j, i: (i, 0)),
                  pl.BlockSpec((None, k, tn), lambda j, i: (widx, 0, col0 // tn)),
                  pl.BlockSpec((None, 1, hd), lambda j, i: (widx, 0, 0)),
                  tab, tab],
        out_specs=[row_spec] * 4,
        out_shape=[jax.ShapeDtypeStruct((m, kvw), BF16)] * 2 + [jax.ShapeDtypeStruct((m, kvw), F32)] * 2,
        scratch_shapes=[pltpu.VMEM((k, tn), BF16)],
        compiler_params=_params("arbitrary", "arbitrary"),
    )(a, w, k_norm.reshape(-1, 1, hd), *rope_rows)


def _attn_kernel(*refs, group):
    q_ref, k_ref, v_ref, _, o_ref = refs
    hd = ATTN_HEAD_DIM
    for kh in range(k_ref.shape[-1] // hd):
        k = k_ref[:, kh * hd:(kh + 1) * hd]
        v = v_ref[:, kh * hd:(kh + 1) * hd]
        v_ext = jnp.concatenate([v, jnp.ones_like(v)], axis=1)
        for g in range(group):
            cols = slice((kh * group + g) * hd, (kh * group + g + 1) * hd)
            s = lax.dot_general(q_ref[:, cols], k, NT_DIMS, preferred_element_type=F32)
            p = jnp.exp2(s - jnp.max(s, axis=-1, keepdims=True))
            o = jnp.dot(p.astype(BF16), v_ext, preferred_element_type=F32)
            o_ref[:, cols] = (o[:, :hd] / o[:, hd:]).astype(o_ref.dtype)


ATTN_SCORE_BUDGET = 4 * 1024 * 1024


def _attn_call(q, k, v, n_seq, seq_len, row0, mix, tq=256):
    kvw = k.shape[-1]
    qw = q.shape[-1]
    hd = ATTN_HEAD_DIM
    n_kv = kvw // hd
    group = qw // hd // n_kv
    assert row0 % tq == 0 and seq_len % tq == 0 and row0 % seq_len == 0
    per_seq = seq_len // tq
    rb0 = row0 // tq
    lk = seq_len if k.ndim == 2 else k.shape[1]
    kv_step = n_kv if tq * lk * 4 * n_kv <= ATTN_SCORE_BUDGET else 1
    q_spec = pl.BlockSpec((tq, kv_step * group * hd), lambda n, kh, i: (rb0 + n * per_seq + i, kh))
    if k.ndim == 2:
        kv_spec = pl.BlockSpec((seq_len, kv_step * hd), lambda n, kh, i: (row0 // seq_len + n, kh))
    else:
        kv_spec = pl.BlockSpec((None, lk, kv_step * hd), lambda n, kh, i: (n, 0, kh))
    in_specs = [q_spec, kv_spec, kv_spec]
    args = [q, k, v]
    (mix_shape,), aliases = _join_buffers([mix], in_specs, args)
    return pl.pallas_call(
        functools.partial(_attn_kernel, group=group),
        grid=(n_seq, n_kv // kv_step, per_seq),
        in_specs=in_specs,
        out_specs=q_spec,
        out_shape=mix_shape,
        input_output_aliases=aliases,
        compiler_params=_params("arbitrary", "arbitrary", "arbitrary"),
    )(*args)


def _grid_rope(n_tokens, head_dim):
    rows = n_tokens // GRID_W
    row = jnp.broadcast_to(jnp.arange(rows)[:, None], (rows, GRID_W)).reshape(-1).astype(F32)
    col = jnp.broadcast_to(jnp.arange(GRID_W)[None, :], (rows, GRID_W)).reshape(-1).astype(F32)
    per_axis = head_dim // 4
    freqs = ROPE_THETA ** (-jnp.arange(per_axis, dtype=F32) / per_axis)
    ang = jnp.concatenate([row[:, None] * freqs, col[:, None] * freqs], axis=-1)
    return jnp.cos(ang), jnp.sin(ang)


def kernel(x_prompt, x_sample, c, state_hgrn, state_ret, cache_k, cache_v, c_ctx, w_mod, b_mod, norm_mix, norm_ffn, w_in_even, w_out_even, hgrn_lb_logits, hgrn_onorm, ret_decay_logit, ret_gnorm, w_in_attn, w_out_attn, q_norm, k_norm, w_ffn_gate, w_ffn_up, w_ffn_down):
    n_p, l_p, d = x_prompt.shape
    n_s, l_s, _ = x_sample.shape
    depth = w_mod.shape[0]
    rows_p = n_p * l_p
    rows_s = n_s * l_s
    a_width = hgrn_lb_logits.shape[-1]
    b_width = ret_gnorm.shape[-1]
    kv_width = ATTN_KV_HEADS * ATTN_HEAD_DIM
    row_quantum = math.gcd(rows_p, l_s)

    def group_of_row(r):
        return jnp.where(r < rows_p, 0, 1 + (r - rows_p) // l_s)

    n_groups = 1 + n_s
    cond = jnp.concatenate([c_ctx[None, :], c, jnp.zeros((-(1 + n_s) % SUBLANES, d), F32)], axis=0)
    mods = _adaln_call(cond, w_mod, b_mod)
    mods = mods[:, :n_groups].reshape(depth, n_groups, 6, 1, d)

    cos_a, sin_a = _grid_rope(l_s, ATTN_HEAD_DIM)
    rope_attn = (jnp.concatenate([cos_a, cos_a], axis=-1), jnp.concatenate([-sin_a, sin_a], axis=-1))
    rope_rows = (jnp.concatenate([jnp.ones((rows_p, ATTN_HEAD_DIM), F32), jnp.tile(rope_attn[0], (n_s, 1))]),
                 jnp.concatenate([jnp.zeros((rows_p, ATTN_HEAD_DIM), F32), jnp.tile(rope_attn[1], (n_s, 1))]))
    rope_ret = _grid_rope(l_s, b_width // B_HEADS)
    lb_cum = jnp.cumsum(jax.nn.softmax(hgrn_lb_logits.astype(F32), axis=0), axis=0)
    lower_bounds = lb_cum - lb_cum[0]

    x = jnp.concatenate([x_prompt.reshape(rows_p, d), x_sample.reshape(rows_s, d)], axis=0)
    n_even = w_in_even.shape[0]
    out_hgrn, out_ret = n_even, n_even
    new_k, new_v = [], []
    for l in range(depth):
        mix = _modulate_call(x, norm_mix, mods, l, 0, group_of_row)
        if l % 2 == 0:
            e = l // 2
            p = _mm_plain_call(mix, w_in_even, e, F32)
            mix, out_hgrn = _gla_call(p, lower_bounds, l, hgrn_onorm, e, None, n_p, l_p, 0, out_hgrn, mix)
            (mix,) = _gla_call(p, lower_bounds, l, hgrn_onorm, e, state_hgrn, n_s, l_s, rows_p, None, mix)
            mix, out_ret = _ret_call(p, 5 * a_width, ret_decay_logit, ret_gnorm, e, None, None,
                                     n_p, l_p, 0, out_ret, mix, a_width)
            (mix,) = _ret_call(p, 5 * a_width, ret_decay_logit, ret_gnorm, e, rope_ret, state_ret,
                               n_s, l_s, rows_p, None, mix, a_width)
            w_out, widx = w_out_even, e
        else:
            o = l // 2
            q = _mm_qproj_call(mix, w_in_attn, o, d, q_norm, rope_rows)
            k, v, kf, vf = _mm_kvproj_call(mix, w_in_attn, o, d, k_norm, rope_rows)
            past = cache_k.shape[2]
            with_cache = lambda new, cache: jnp.concatenate(
                [new[rows_p:].reshape(n_s, l_s, kv_width),
                 cache[:, o].reshape(n_s, past, kv_width).astype(BF16)], axis=1)
            k_all, v_all = with_cache(k, cache_k), with_cache(v, cache_v)
            mix = _attn_call(q, k, v, n_p, l_p, 0, mix)
            mix = _attn_call(q, k_all, v_all, n_s, l_s, rows_p, mix)
            w_out, widx = w_out_attn, o
            new_k.append(kf[:rows_p].reshape(n_p, l_p, ATTN_KV_HEADS, ATTN_HEAD_DIM))
            new_v.append(vf[:rows_p].reshape(n_p, l_p, ATTN_KV_HEADS, ATTN_HEAD_DIM))
        x, h = _mm_residual_modulate_call(mix, w_out, widx, x, mods, l, 2, norm_ffn, 3, group_of_row,
                                          row_quantum)
        u = _mm_swiglu_call(h, w_ffn_gate, w_ffn_up, l)
        down = functools.partial(_mm_residual_call, u, w_ffn_down, l, x, mods, l, 5, group_of_row,
                                 row_quantum)
        if l < depth - 1:
            x = down()
    return (down(0, rows_p).reshape(n_p, l_p, d), down(rows_p, rows_s).reshape(n_s, l_s, d),
            out_hgrn, out_ret, jnp.stack(new_k, axis=1), jnp.stack(new_v, axis=1))
```

```python
import functools
import math

import jax
import jax.numpy as jnp
from jax import lax
from jax.experimental import pallas as pl
from jax.experimental.pallas import tpu as pltpu

F32 = jnp.float32
BF16 = jnp.bfloat16

EPS = 1e-6
ROPE_THETA = 10000.0
GRID_W = 64
A_HEAD_DIM = 128
B_HEADS = 4
ATTN_HEAD_DIM = 128
ATTN_KV_HEADS = 4
LANES = 128
SUBLANES = 8
VMEM_LIMIT = 56 * 1024 * 1024
VMEM_TILE_BUDGET = 46 * 1024 * 1024

LOG2_E = math.log2(math.e)

NT_DIMS = (((1,), (1,)), ((), ()))
TN_DIMS = (((0,), (0,)), ((), ()))


def _params(*sem):
    return pltpu.CompilerParams(dimension_semantics=sem, vmem_limit_bytes=VMEM_LIMIT)


def _silu(x):
    return x / (1.0 + jnp.exp(-x))


def _log_sigmoid(x):
    return jnp.minimum(x, 0.0) - jnp.log1p(jnp.exp(-jnp.abs(x)))


def _adaln_kernel(c_ref, w_ref, b_ref, o_ref):
    s = _silu(c_ref[...]).astype(BF16)
    o_ref[...] = jnp.dot(s, w_ref[...].astype(BF16), preferred_element_type=F32) + b_ref[...]


def _adaln_call(cond, w_mod, b_mod, tn=1024):
    depth, d, n = w_mod.shape
    rows = cond.shape[0]
    return pl.pallas_call(
        _adaln_kernel,
        grid=(depth, n // tn),
        in_specs=[
            pl.BlockSpec((rows, d), lambda l, j: (0, 0)),
            pl.BlockSpec((None, d, tn), lambda l, j: (l, 0, j)),
            pl.BlockSpec((None, 1, tn), lambda l, j: (l, 0, j)),
        ],
        out_specs=pl.BlockSpec((None, rows, tn), lambda l, j: (l, 0, j)),
        out_shape=jax.ShapeDtypeStruct((depth, rows, n), F32),
        compiler_params=_params("arbitrary", "arbitrary"),
    )(cond, w_mod, b_mod.reshape(depth, 1, n))


def _modulate_kernel(x_ref, g_ref, shift_ref, scale_ref, o_ref):
    x = x_ref[...]
    y = x * lax.rsqrt(jnp.mean(x * x, axis=-1, keepdims=True) + EPS) * g_ref[...]
    o_ref[...] = (y * (1.0 + scale_ref[...]) + shift_ref[...]).astype(BF16)


def _modulate_call(x, gains, mods, layer, shift_idx, group_of_row, tr=256):
    m, d = x.shape
    depth = gains.shape[0]
    grp = lambda i: group_of_row(i * tr)
    return pl.pallas_call(
        _modulate_kernel,
        grid=(m // tr,),
        in_specs=[
            pl.BlockSpec((tr, d), lambda i: (i, 0)),
            pl.BlockSpec((None, 1, d), lambda i: (layer, 0, 0)),
            pl.BlockSpec((None, None, None, 1, d), lambda i: (layer, grp(i), shift_idx, 0, 0)),
            pl.BlockSpec((None, None, None, 1, d), lambda i: (layer, grp(i), shift_idx + 1, 0, 0)),
        ],
        out_specs=pl.BlockSpec((tr, d), lambda i: (i, 0)),
        out_shape=jax.ShapeDtypeStruct((m, d), BF16),
        compiler_params=_params("arbitrary"),
    )(x, gains.reshape(depth, 1, d), mods, mods)


def _matmul_tiles(m, k, n, n_weights, out_bytes, residual, row_quantum):
    for tn in (2048, 1024, 512, 256, 128):
        if n % tn:
            continue
        for tm in (1024, 512, 256):
            if m % tm or row_quantum % tm:
                continue
            need = (2 * tm * k * 2 + n_weights * (2 * k * tn * 4 + k * tn * 2)
                    + 2 * tm * tn * out_bytes + (2 * tm * tn * 4 if residual else 0))
            if need <= VMEM_TILE_BUDGET:
                return tm, tn
    raise ValueError(f"no matmul tiling fits VMEM for {(m, k, n)}")


def _cast_weights_once(w_refs, wbf_refs):
    @pl.when(pl.program_id(1) == 0)
    def _():
        for w_ref, wbf_ref in zip(w_refs, wbf_refs):
            wbf_ref[...] = w_ref[...].astype(BF16)


def _mm_plain_kernel(a_ref, w_ref, o_ref, wbf_ref):
    _cast_weights_once((w_ref,), (wbf_ref,))
    o_ref[...] = jnp.dot(a_ref[...], wbf_ref[...], preferred_element_type=F32).astype(o_ref.dtype)


def _mm_swiglu_kernel(a_ref, wg_ref, wu_ref, o_ref, wgbf_ref, wubf_ref):
    _cast_weights_once((wg_ref, wu_ref), (wgbf_ref, wubf_ref))
    a = a_ref[...]
    g = jnp.dot(a, wgbf_ref[...], preferred_element_type=F32)
    u = jnp.dot(a, wubf_ref[...], preferred_element_type=F32)
    o_ref[...] = (_silu(g) * u).astype(o_ref.dtype)


def _mm_residual_kernel(a_ref, w_ref, x_ref, gate_ref, o_ref, wbf_ref):
    _cast_weights_once((w_ref,), (wbf_ref,))
    y = jnp.dot(a_ref[...], wbf_ref[...], preferred_element_type=F32)
    o_ref[...] = x_ref[...] + gate_ref[...] * y


def _mm_residual_modulate_kernel(a_ref, w_ref, x_ref, gate_ref, g_ref, shift_ref, scale_ref,
                                 ox_ref, oh_ref, wbf_ref):
    @pl.when(pl.program_id(0) == 0)
    def _():
        wbf_ref[...] = w_ref[...].astype(BF16)
    y = jnp.dot(a_ref[...], wbf_ref[...], preferred_element_type=F32)
    x = x_ref[...] + gate_ref[...] * y
    ox_ref[...] = x
    h = x * lax.rsqrt(jnp.mean(x * x, axis=-1, keepdims=True) + EPS) * g_ref[...]
    oh_ref[...] = (h * (1.0 + scale_ref[...]) + shift_ref[...]).astype(BF16)


def _mm_residual_modulate_call(a, w, widx, x, mods, layer, gate_idx, gains, shift_idx, group_of_row,
                               row_quantum):
    m, k = a.shape
    n = w.shape[-1]
    depth = gains.shape[0]
    for tm in (512, 256, 128):
        need = k * n * (4 + 2) + 2 * tm * (k * 2 + n * (4 + 4 + 2))
        if m % tm == 0 and row_quantum % tm == 0 and need <= VMEM_TILE_BUDGET:
            break
    else:
        raise ValueError(f"no row tile fits VMEM for {(m, k, n)}")
    grp = lambda i: group_of_row(i * tm)
    mod_spec = lambda which: pl.BlockSpec((None, None, None, 1, n), lambda i: (layer, grp(i), which, 0, 0))
    row_spec = pl.BlockSpec((tm, n), lambda i: (i, 0))
    return pl.pallas_call(
        _mm_residual_modulate_kernel,
        grid=(m // tm,),
        in_specs=[
            pl.BlockSpec((tm, k), lambda i: (i, 0)),
            pl.BlockSpec((None, k, n), lambda i: (widx, 0, 0), pipeline_mode=pl.Buffered(1)),
            row_spec,
            mod_spec(gate_idx),
            pl.BlockSpec((None, 1, n), lambda i: (layer, 0, 0)),
            mod_spec(shift_idx),
            mod_spec(shift_idx + 1),
        ],
        out_specs=[row_spec, row_spec],
        out_shape=[jax.ShapeDtypeStruct((m, n), F32), jax.ShapeDtypeStruct((m, n), BF16)],
        scratch_shapes=[pltpu.VMEM((k, n), BF16)],
        compiler_params=_params("arbitrary"),
    )(a, w, x, mods, gains.reshape(depth, 1, n), mods, mods)


def _mm_plain_call(a, w, widx, out_dtype):
    m, k = a.shape
    n = w.shape[-1]
    tm, tn = _matmul_tiles(m, k, n, 1, jnp.dtype(out_dtype).itemsize, False, m)
    return pl.pallas_call(
        _mm_plain_kernel,
        grid=(n // tn, m // tm),
        in_specs=[
            pl.BlockSpec((tm, k), lambda j, i: (i, 0)),
            pl.BlockSpec((None, k, tn), lambda j, i: (widx, 0, j)),
        ],
        out_specs=pl.BlockSpec((tm, tn), lambda j, i: (i, j)),
        out_shape=jax.ShapeDtypeStruct((m, n), out_dtype),
        scratch_shapes=[pltpu.VMEM((k, tn), BF16)],
        compiler_params=_params("arbitrary", "arbitrary"),
    )(a, w)


def _mm_swiglu_call(a, wg, wu, widx):
    m, k = a.shape
    n = wg.shape[-1]
    tm, tn = _matmul_tiles(m, k, n, 2, 2, False, m)
    wspec = pl.BlockSpec((None, k, tn), lambda j, i: (widx, 0, j))
    return pl.pallas_call(
        _mm_swiglu_kernel,
        grid=(n // tn, m // tm),
        in_specs=[pl.BlockSpec((tm, k), lambda j, i: (i, 0)), wspec, wspec],
        out_specs=pl.BlockSpec((tm, tn), lambda j, i: (i, j)),
        out_shape=jax.ShapeDtypeStruct((m, n), BF16),
        scratch_shapes=[pltpu.VMEM((k, tn), BF16), pltpu.VMEM((k, tn), BF16)],
        compiler_params=_params("arbitrary", "arbitrary"),
    )(a, wg, wu)


def _mm_residual_call(a, w, widx, x, mods, layer, gate_idx, group_of_row, row_quantum, row0=0, m=None):
    k = a.shape[1]
    m = a.shape[0] if m is None else m
    n = w.shape[-1]
    tm, tn = _matmul_tiles(m, k, n, 1, 4, True, row_quantum)
    assert row0 % tm == 0
    rb0 = row0 // tm
    grp = lambda i: group_of_row((i + rb0) * tm)
    return pl.pallas_call(
        _mm_residual_kernel,
        grid=(n // tn, m // tm),
        in_specs=[
            pl.BlockSpec((tm, k), lambda j, i: (i + rb0, 0)),
            pl.BlockSpec((None, k, tn), lambda j, i: (widx, 0, j)),
            pl.BlockSpec((tm, tn), lambda j, i: (i + rb0, j)),
            pl.BlockSpec((None, None, None, 1, tn), lambda j, i: (layer, grp(i), gate_idx, 0, j)),
        ],
        out_specs=pl.BlockSpec((tm, tn), lambda j, i: (i, j)),
        out_shape=jax.ShapeDtypeStruct((m, n), F32),
        scratch_shapes=[pltpu.VMEM((k, tn), BF16)],
        compiler_params=_params("arbitrary", "arbitrary"),
    )(a, w, x, mods)


def _bcast_row_in_blocks(x, block, r):
    c, d = x.shape
    x3 = x.reshape(c // block, block, d)
    return jnp.broadcast_to(x3[:, r:r + 1, :], x3.shape).reshape(c, d)


MXU_WIDTH = 256


def _gla_blocks(c):
    return [1 << b for b in range((c // 2).bit_length())] + [0]


def _gla_masks(c, fwd):
    i = lax.broadcasted_iota(jnp.int32, (c, LANES), 0)
    lane = lax.broadcasted_iota(jnp.int32, (c, LANES), 1)
    masks = []
    for b, h in enumerate(_gla_blocks(c)):
        j = lane - (b * c) % LANES
        in_block = jnp.where(j >= 0, jnp.where(j < c, 1.0, 0.0), 0.0)
        if h == 0:
            ok = jnp.where(i == j, 1.0, 0.0)
        else:
            same = (i // (2 * h)) == (j // (2 * h))
            i_hi = (i % (2 * h)) >= h
            j_hi = (j % (2 * h)) >= h
            if fwd:
                ok = jnp.where(same, jnp.where(i_hi, jnp.where(j_hi, 0.0, 1.0), 0.0), 0.0)
            else:
                ok = jnp.where(same, jnp.where(i_hi, 0.0, jnp.where(j_hi, 1.0, 0.0)), 0.0)
        masks.append((ok * in_block).astype(F32))
    return masks


def _split_roles(q, kg, h, fwd, rloc):
    c = q.shape[0]
    if h >= SUBLANES:
        parts = []
        for b in range(c // (2 * h)):
            lo = slice(2 * h * b, 2 * h * b + h)
            hi = slice(2 * h * b + h, 2 * h * (b + 1))
            parts += [kg[lo], q[hi]] if fwd else [q[lo], kg[hi]]
        return jnp.concatenate(parts, axis=0)
    in_hi = (rloc % (2 * h)) >= h
    return jnp.where(in_hi, q, kg) if fwd else jnp.where(in_hi, kg, q)


def _split3(x):
    hi = x.astype(BF16)
    r1 = x - hi.astype(F32)
    mid = r1.astype(BF16)
    lo = (r1 - mid.astype(F32)).astype(BF16)
    return jnp.concatenate([hi, mid, lo], axis=-1)


def _gla_chunks(chains, rloc):
    c, d = chains[0][0].shape
    per_group = MXU_WIDTH // c
    blocks = _gla_blocks(c)
    n_groups = -(-len(blocks) // per_group)
    n_kvregs = -(-len(blocks) * c // LANES)

    es = [None] * len(chains)
    for direction in (True, False):
        ids = [n for n, ch in enumerate(chains) if ch[6] == direction]
        if ids:
            e3 = jnp.dot(chains[ids[0]][7], jnp.concatenate([chains[n][4] for n in ids], axis=1),
                         preferred_element_type=F32)
            for pos, n in enumerate(ids):
                base = 3 * d * pos
                es[n] = e3[:, base:base + d] + e3[:, base + d:base + 2 * d] + e3[:, base + 2 * d:base + 3 * d]

    sides = []
    for (q, kg, fg, v, logf, st, fwd, tri_bf, masks), e in zip(chains, es):
        qs, ks = [], []
        for h in blocks:
            if h == 0:
                qs.append(q.astype(BF16))
                ks.append(kg.astype(BF16))
                continue
            if h == 1:
                in_hi = (rloc % 2) >= 1
                w = jnp.where(in_hi, q * fg, kg) if fwd else jnp.where(in_hi, kg, q * fg)
            else:
                r = h - 1 if fwd else h
                if 2 * h >= SUBLANES:
                    e_ref = _bcast_row_in_blocks(e, 2 * h, r)
                else:
                    e_ref = jnp.where(rloc < 2 * h, _bcast_row_in_blocks(e, SUBLANES, r),
                                      _bcast_row_in_blocks(e, SUBLANES, 2 * h + r))
                w = _split_roles(q, kg, h, fwd, rloc) * (1.0 / jnp.exp2(jnp.abs(e - e_ref)))
            qs.append(w.astype(BF16))
            ks.append(qs[-1])
        pad = [jnp.zeros((c, d), BF16)] * (n_groups * per_group - len(blocks))
        sides.append((qs + pad, ks + pad))

    atts = [[None] * n_kvregs for _ in chains]
    for g in range(n_groups):
        for ci, (qs, ks) in enumerate(sides):
            sl = slice(g * per_group, (g + 1) * per_group)
            prod = lax.dot_general(jnp.concatenate(qs[sl], axis=0), jnp.concatenate(ks[sl], axis=0),
                                   NT_DIMS, preferred_element_type=F32)
            masks = chains[ci][8]
            for b in range(g * per_group, min((g + 1) * per_group, len(blocks))):
                t = b * c // LANES
                col = (t * LANES) % MXU_WIDTH
                row = (b - g * per_group) * c
                term = prod[row:row + c, col:col + LANES] * masks[b]
                atts[ci][t] = term if atts[ci][t] is None else atts[ci][t] + term

    outs = []
    for (q, kg, fg, v, logf, st, fwd, tri_bf, masks), e, att in zip(chains, es, atts):
        vb = v.astype(BF16)
        att_sum = functools.reduce(lambda x, y: x + y, att)
        v_rep = jnp.concatenate([vb] * (LANES // c), axis=0)
        o = jnp.dot(att_sum.astype(BF16), v_rep, preferred_element_type=F32)
        o = o + lax.dot_general((q * jnp.exp2(e)).astype(BF16), st.astype(BF16), NT_DIMS,
                                preferred_element_type=F32)
        e_edge = e[c - 1:c] if fwd else e[0:1]
        kh = (kg * jnp.exp2(e_edge - e)).astype(BF16)
        st_new = st * jnp.exp2(e_edge) + lax.dot_general(vb, kh, TN_DIMS, preferred_element_type=F32)
        outs.append((o, st_new))
    return outs


def _gla_kernel(*refs, seq_len, chunk, has_s0, final):
    refs = list(refs)
    q_ref, ffw_ref, fbw_ref, v_ref, g_ref, lb_ref, on_ref = refs[:7]
    del refs[:7]
    s0_ref = refs.pop(0) if has_s0 else None
    del refs[:2 if final == "join" else 1]
    o_ref = refs.pop(0)
    sf_ref = _final_state_slot(refs.pop(0), final) if final is not None else None
    stf_ref, stb_ref, oacc_ref = refs
    c = chunk
    nch = seq_len // c
    half_n = nch // 2
    d = A_HEAD_DIM
    n_h = q_ref.shape[-1] // d

    i = lax.broadcasted_iota(jnp.int32, (c, c), 0)
    j = lax.broadcasted_iota(jnp.int32, (c, c), 1)
    tri_f = jnp.where(i >= j, 1.0, 0.0).astype(BF16)
    tri_b = jnp.where(i <= j, 1.0, 0.0).astype(BF16)
    masks_f = _gla_masks(c, True)
    masks_b = _gla_masks(c, False)
    rloc = lax.broadcasted_iota(jnp.int32, (c, d), 0) % SUBLANES
    q_scale = d ** -0.5

    def gates(x, lb):
        one_m_lb = 1.0 - lb
        t = jnp.exp(-jnp.abs(x))
        r = 1.0 / (1.0 + t)
        pos = x >= 0.0
        fg = lb + one_m_lb * jnp.where(pos, r, t * r)
        kg = one_m_lb * jnp.where(pos, t * r, r)
        log2f = jnp.maximum(jnp.log2(fg),
                            jnp.log2(one_m_lb) + jnp.minimum(x, 0.0) * LOG2_E - jnp.log2(1.0 + t))
        return fg, kg, log2f

    def chain_rows(t, fwd):
        return pl.ds(pl.multiple_of((t if fwd else nch - 1 - t) * c, c), c)

    chain_ids = [(hh, fwd) for hh in range(n_h) for fwd in (True, False)]

    def all_chains(t):
        where, chains = [], []
        for hh, fwd in chain_ids:
            sl, cols = chain_rows(t, fwd), slice(hh * d, (hh + 1) * d)
            fg, kg, log2f = gates((ffw_ref if fwd else fbw_ref)[sl, cols], lb_ref[:, cols])
            where.append((sl, cols))
            chains.append((_silu(q_ref[sl, cols]) * q_scale, kg, fg, v_ref[sl, cols], _split3(log2f),
                           (stf_ref if fwd else stb_ref)[hh], fwd,
                           tri_f if fwd else tri_b, masks_f if fwd else masks_b))
        outs = _gla_chunks(chains, rloc)
        for n, (o, st) in enumerate(outs):
            (stf_ref if n % 2 == 0 else stb_ref)[n // 2] = st
        return [(sl, cols, o) for (sl, cols), (o, st) in zip(where, outs)]

    def finish(sl, cols, o):
        y = oacc_ref[sl, cols] + o
        y = y * lax.rsqrt(jnp.mean(y * y, axis=-1, keepdims=True) + EPS) * on_ref[...]
        o_ref[sl, cols] = (y * _silu(g_ref[sl, cols])).astype(o_ref.dtype)

    def first_half(t, carry):
        for sl, cols, o in all_chains(t):
            oacc_ref[sl, cols] = o
        return carry

    def second_half(t, carry):
        for sl, cols, o in all_chains(t):
            finish(sl, cols, o)
        return carry

    for hh in range(n_h):
        for st_ref, direction in ((stf_ref, 0), (stb_ref, 1)):
            st_ref[hh] = s0_ref[direction, hh].T if has_s0 else jnp.zeros((d, d), F32)
    unroll = 2 if 2 * n_h < 8 and half_n % 2 == 0 else 1
    lax.fori_loop(0, half_n, first_half, 0, unroll=unroll)
    lax.fori_loop(half_n, nch, second_half, 0, unroll=unroll)
    if final is not None:
        for hh in range(n_h):
            sf_ref[0, hh] = stf_ref[hh].T
            sf_ref[1, hh] = stb_ref[hh].T


GLA_CHUNK = 64


def _join_buffers(bufs, in_specs, args):
    shapes, aliases = [], {}
    for n, buf in enumerate(bufs):
        in_specs.append(pl.BlockSpec(memory_space=pl.ANY))
        args.append(buf)
        shapes.append(jax.ShapeDtypeStruct(buf.shape, buf.dtype))
        aliases[len(args) - 1] = n
    return shapes, aliases


def _final_state_specs(finals, slot, dims, heads_per_step, out_specs):
    n_seq, heads, d = dims
    if finals is None:
        return [], None
    if isinstance(finals, int):
        out_specs.append(pl.BlockSpec((None, finals, 2, heads_per_step, d, d),
                                      lambda n, h: (n, 0, 0, h, 0, 0)))
        return [], (slot, finals)
    out_specs.append(pl.BlockSpec((None, None, 2, heads_per_step, d, d),
                                  lambda n, h: (n, slot, 0, h, 0, 0)))
    return [finals], "join"


def _final_state_slot(sf_ref, final):
    if final == "join":
        return sf_ref
    slot, n_slots = final
    for s in range(n_slots):
        if s != slot:
            sf_ref[s] = jnp.zeros(sf_ref.shape[1:], sf_ref.dtype)
    return sf_ref.at[slot]


def _gla_heads_per_step(seq_len):
    for n_h in (4, 2, 1):
        if 2 * 5 * seq_len * n_h * A_HEAD_DIM * 4 + 4 * seq_len * n_h * A_HEAD_DIM * 2 <= VMEM_TILE_BUDGET:
            return n_h
    raise ValueError(f"sequence of {seq_len} rows does not fit VMEM")


def _gla_call(p, lower_bounds, layer, onorm, eidx, s0, n_seq, seq_len, row0, finals, mix):
    d = A_HEAD_DIM
    n_h = _gla_heads_per_step(seq_len)
    w = n_h * d
    a_width = lower_bounds.shape[-1]
    n_hb = a_width // w
    assert row0 % seq_len == 0 and seq_len % (2 * GLA_CHUNK) == 0 and a_width % w == 0
    sb = row0 // seq_len
    slab = lambda k: pl.BlockSpec((seq_len, w), lambda n, h: (n + sb, k * n_hb + h))
    in_specs = [slab(0), slab(1), slab(2), slab(3), slab(4),
                pl.BlockSpec((None, 1, w), lambda n, h: (layer, 0, h)),
                pl.BlockSpec((None, 1, d), lambda n, h: (eidx, 0, 0))]
    args = [p, p, p, p, p, lower_bounds.reshape(lower_bounds.shape[0], 1, a_width),
            onorm.reshape(onorm.shape[0], 1, d)]
    if s0 is not None:
        in_specs.append(pl.BlockSpec((None, None, 2, n_h, d, d), lambda n, h: (n, eidx, 0, h, 0, 0)))
        args.append(s0)
    out_specs = [pl.BlockSpec((seq_len, w), lambda n, h: (n + sb, h))]
    joined, final = _final_state_specs(finals, eidx, (n_seq, a_width // d, d), n_h, out_specs)
    out_shape, aliases = _join_buffers([mix] + joined, in_specs, args)
    if isinstance(final, tuple):
        out_shape.append(jax.ShapeDtypeStruct((n_seq, final[1], 2, a_width // d, d, d), F32))
    kern = functools.partial(_gla_kernel, seq_len=seq_len, chunk=GLA_CHUNK,
                             has_s0=s0 is not None, final=final)
    return pl.pallas_call(
        kern,
        grid=(n_seq, n_hb),
        in_specs=in_specs,
        out_specs=out_specs,
        out_shape=out_shape,
        input_output_aliases=aliases,
        scratch_shapes=[pltpu.VMEM((n_h, d, d), F32), pltpu.VMEM((n_h, d, d), F32),
                        pltpu.VMEM((seq_len, w), F32)],
        compiler_params=_params("arbitrary", "arbitrary"),
    )(*args)


def _ret_kernel(*refs, seq_len, chunk, has_s0, final, rope):
    refs = list(refs)
    q_ref, k_ref, v_ref, g_ref, dl_ref, gn_ref = refs[:6]
    del refs[:6]
    cos_ref = refs.pop(0) if rope else None
    sin_ref = refs.pop(0) if rope else None
    s0_ref = refs.pop(0) if has_s0 else None
    del refs[:2 if final == "join" else 1]
    o_ref = refs.pop(0)
    sf_ref = _final_state_slot(refs.pop(0), final) if final is not None else None
    stf_ref, stb_ref, oacc_ref = refs
    c = chunk
    nch = seq_len // c
    half_n = nch // 2
    d = dl_ref.shape[-1]
    n_h = q_ref.shape[-1] // d
    half = d // 2
    k_scale = d ** -0.5

    dist = (lax.broadcasted_iota(jnp.int32, (c, c), 0)
            - lax.broadcasted_iota(jnp.int32, (c, c), 1)).astype(F32)
    low = dist >= 0.0
    up = dist <= 0.0
    row = lax.broadcasted_iota(jnp.int32, (c, d), 0).astype(F32)

    def decay_tables(hh):
        lg_f = _log_sigmoid(dl_ref[0, hh])
        lg_b = _log_sigmoid(dl_ref[1, hh])
        intra = (jnp.where(low, jnp.exp(jnp.where(low, dist, 0.0) * lg_f[:, :c]), 0.0)
                 + jnp.where(up, jnp.exp(jnp.where(up, -dist, 0.0) * lg_b[:, :c]), 0.0))
        fwd = (jnp.exp((row + 1.0) * lg_f), jnp.exp((c - 1.0 - row) * lg_f), jnp.exp(c * lg_f))
        bwd = (jnp.exp((c - row) * lg_b), jnp.exp(row * lg_b), jnp.exp(c * lg_b))
        return intra, fwd, bwd

    tables = [decay_tables(hh) for hh in range(n_h)]

    def rotate(x, sl):
        if not rope:
            return x
        cos, sin = cos_ref[sl, :], sin_ref[sl, :]
        x1, x2 = x[:, :half], x[:, half:]
        return jnp.concatenate([x1 * cos - x2 * sin, x1 * sin + x2 * cos], axis=-1)

    def load(ci, hh):
        sl, cols = pl.ds(pl.multiple_of(ci * c, c), c), slice(hh * d, (hh + 1) * d)
        return (sl, cols, rotate(q_ref[sl, cols], sl), rotate(k_ref[sl, cols] * k_scale, sl),
                v_ref[sl, cols].astype(BF16))

    def carried(q, k, vb, st_ref, hh, decs):
        q_dec, k_dec, c_dec = decs
        st = st_ref[hh]
        o = jnp.dot((q * q_dec).astype(BF16), st.astype(BF16), preferred_element_type=F32)
        st_ref[hh] = c_dec * st + lax.dot_general((k * k_dec).astype(BF16), vb, TN_DIMS,
                                                  preferred_element_type=F32)
        return o

    def trip(t):
        left = [load(t, hh) for hh in range(n_h)]
        right = [load(nch - 1 - t, hh) for hh in range(n_h)]
        scores = [lax.dot_general(q.astype(BF16), k.astype(BF16), NT_DIMS, preferred_element_type=F32)
                  for _, _, q, k, _ in left]
        o_left = [carried(q, k, vb, stf_ref, hh, tables[hh][1]) for hh, (_, _, q, k, vb) in enumerate(left)]
        o_right = [carried(q, k, vb, stb_ref, hh, tables[hh][2]) for hh, (_, _, q, k, vb) in enumerate(right)]
        for hh, (_, _, _, _, vb) in enumerate(left):
            o_left[hh] = o_left[hh] + jnp.dot((scores[hh] * tables[hh][0]).astype(BF16), vb,
                                              preferred_element_type=F32)
        return ([(sl, cols, o) for (sl, cols, *_), o in zip(left, o_left)]
                + [(sl, cols, o) for (sl, cols, *_), o in zip(right, o_right)])

    def finish(sl, cols, o):
        y = oacc_ref[sl, cols] + o
        yc = y - jnp.mean(y, axis=-1, keepdims=True)
        yn = yc * lax.rsqrt(jnp.mean(yc * yc, axis=-1, keepdims=True) + EPS)
        o_ref[sl, cols] = (yn * gn_ref[:, cols] * _silu(g_ref[sl, cols])).astype(o_ref.dtype)

    def first_half(t, carry):
        for sl, cols, o in trip(t):
            oacc_ref[sl, cols] = o
        return carry

    def second_half(t, carry):
        for sl, cols, o in trip(t):
            finish(sl, cols, o)
        return carry

    for hh in range(n_h):
        for st_ref, direction in ((stf_ref, 0), (stb_ref, 1)):
            st_ref[hh] = s0_ref[direction, hh] if has_s0 else jnp.zeros((d, d), F32)
    lax.fori_loop(0, half_n, first_half, 0)
    lax.fori_loop(half_n, nch, second_half, 0)
    if final is not None:
        for hh in range(n_h):
            sf_ref[0, hh] = stf_ref[hh]
            sf_ref[1, hh] = stb_ref[hh]


def _ret_call(p, col0, decay_logit, gnorm, eidx, rope_tabs, s0, n_seq, seq_len, row0, finals,
              mix, mix_col0, chunk=128):
    n_heads = decay_logit.shape[-1]
    b_width = gnorm.shape[-1]
    d = b_width // n_heads
    n_h = 2 if n_heads % 2 == 0 else 1
    w = n_h * d
    n_hb = n_heads // n_h
    assert row0 % seq_len == 0 and seq_len % (2 * chunk) == 0 and col0 % w == 0 and chunk <= d
    assert mix_col0 % w == 0
    sb = row0 // seq_len
    cb0 = col0 // w
    ocb0 = mix_col0 // w
    slab = lambda k: pl.BlockSpec((seq_len, w), lambda n, h: (n + sb, cb0 + k * n_hb + h))
    dl = jnp.broadcast_to(decay_logit[:, :, :, None, None], decay_logit.shape + (1, d))
    in_specs = [slab(0), slab(1), slab(2), slab(3),
                pl.BlockSpec((None, 2, n_h, 1, d), lambda n, h: (eidx, 0, h, 0, 0)),
                pl.BlockSpec((None, 1, w), lambda n, h: (eidx, 0, h))]
    args = [p, p, p, p, dl, gnorm.reshape(gnorm.shape[0], 1, b_width)]
    if rope_tabs is not None:
        tab = pl.BlockSpec((seq_len, d // 2), lambda n, h: (0, 0))
        in_specs += [tab, tab]
        args += list(rope_tabs)
    if s0 is not None:
        in_specs.append(pl.BlockSpec((None, None, 2, n_h, d, d), lambda n, h: (n, eidx, 0, h, 0, 0)))
        args.append(s0)
    out_specs = [pl.BlockSpec((seq_len, w), lambda n, h: (n + sb, ocb0 + h))]
    joined, final = _final_state_specs(finals, eidx, (n_seq, n_heads, d), n_h, out_specs)
    out_shape, aliases = _join_buffers([mix] + joined, in_specs, args)
    if isinstance(final, tuple):
        out_shape.append(jax.ShapeDtypeStruct((n_seq, final[1], 2, n_heads, d, d), F32))
    kern = functools.partial(_ret_kernel, seq_len=seq_len, chunk=chunk, has_s0=s0 is not None,
                             final=final, rope=rope_tabs is not None)
    return pl.pallas_call(
        kern,
        grid=(n_seq, n_hb),
        in_specs=in_specs,
        out_specs=out_specs,
        out_shape=out_shape,
        input_output_aliases=aliases,
        scratch_shapes=[pltpu.VMEM((n_h, d, d), F32), pltpu.VMEM((n_h, d, d), F32),
                        pltpu.VMEM((seq_len, w), F32)],
        compiler_params=_params("arbitrary", "arbitrary"),
    )(*args)


Q_PRESCALE = ATTN_HEAD_DIM ** -0.5 * LOG2_E


def _head_norm(x, gain):
    return x * lax.rsqrt(jnp.mean(x * x, axis=-1, keepdims=True) + EPS) * gain


def _rotate(y, cos, sin):
    return y * cos + pltpu.roll(y, y.shape[-1] // 2, 1) * sin


def _mm_qproj_kernel(a_ref, w_ref, gain_ref, cos_ref, sin_ref, o_ref, wbf_ref):
    _cast_weights_once((w_ref,), (wbf_ref,))
    y = jnp.dot(a_ref[...], wbf_ref[...], preferred_element_type=F32)
    hd = ATTN_HEAD_DIM
    for h in range(y.shape[-1] // hd):
        cols = slice(h * hd, (h + 1) * hd)
        z = _rotate(_head_norm(y[:, cols], gain_ref[...]), cos_ref[...], sin_ref[...])
        o_ref[:, cols] = (z * Q_PRESCALE).astype(BF16)


def _mm_qproj_call(a, w, widx, n_cols, q_norm, rope_rows):
    m, k = a.shape
    hd = ATTN_HEAD_DIM
    tm, tn = _matmul_tiles(m, k, n_cols, 1, 2, False, m)
    tab = pl.BlockSpec((tm, hd), lambda j, i: (i, 0))
    return pl.pallas_call(
        _mm_qproj_kernel,
        grid=(n_cols // tn, m // tm),
        in_specs=[
            pl.BlockSpec((tm, k), lambda j, i: (i, 0)),
            pl.BlockSpec((None, k, tn), lambda j, i: (widx, 0, j)),
            pl.BlockSpec((None, 1, hd), lambda j, i: (widx, 0, 0)),
            tab, tab,
        ],
        out_specs=pl.BlockSpec((tm, tn), lambda j, i: (i, j)),
        out_shape=jax.ShapeDtypeStruct((m, n_cols), BF16),
        scratch_shapes=[pltpu.VMEM((k, tn), BF16)],
        compiler_params=_params("arbitrary", "arbitrary"),
    )(a, w, q_norm.reshape(-1, 1, hd), *rope_rows)


def _mm_kvproj_kernel(a_ref, w_ref, gain_ref, cos_ref, sin_ref, k_out, v_out, kf_out, vf_out, wbf_ref):
    _cast_weights_once((w_ref,), (wbf_ref,))
    y = jnp.dot(a_ref[...], wbf_ref[...], preferred_element_type=F32)
    hd = ATTN_HEAD_DIM
    kvw = y.shape[-1] // 2
    for h in range(kvw // hd):
        cols = slice(h * hd, (h + 1) * hd)
        z = _rotate(_head_norm(y[:, cols], gain_ref[...]), cos_ref[...], sin_ref[...])
        k_out[:, cols] = z.astype(BF16)
        kf_out[:, cols] = z
    v = y[:, kvw:]
    v_out[...] = v.astype(BF16)
    vf_out[...] = v


def _mm_kvproj_call(a, w, widx, col0, k_norm, rope_rows):
    m, k = a.shape
    hd = ATTN_HEAD_DIM
    kvw = ATTN_KV_HEADS * hd
    tn = 2 * kvw
    assert col0 % tn == 0
    tm, _ = _matmul_tiles(m, k, tn, 1, 2 + 4, False, m)
    tab = pl.BlockSpec((tm, hd), lambda j, i: (i, 0))
    row_spec = pl.BlockSpec((tm, kvw), lambda j, i: (i, 0))
    return pl.pallas_call(
        _mm_kvproj_kernel,
        grid=(1, m // tm),
        in_specs=[pl.BlockSpec((tm, k), lambda j, i: (i, 0)),
                  pl.BlockSpec((None, k, tn), lambda j, i: (widx, 0, col0 // tn)),
                  pl.BlockSpec((None, 1, hd), lambda j, i: (widx, 0, 0)),
                  tab, tab],
        out_specs=[row_spec] * 4,
        out_shape=[jax.ShapeDtypeStruct((m, kvw), BF16)] * 2 + [jax.ShapeDtypeStruct((m, kvw), F32)] * 2,
        scratch_shapes=[pltpu.VMEM((k, tn), BF16)],
        compiler_params=_params("arbitrary", "arbitrary"),
    )(a, w, k_norm.reshape(-1, 1, hd), *rope_rows)


def _attn_kernel(*refs, group):
    q_ref, k_ref, v_ref, _, o_ref = refs
    hd = ATTN_HEAD_DIM
    for kh in range(k_ref.shape[-1] // hd):
        k = k_ref[:, kh * hd:(kh + 1) * hd]
        v = v_ref[:, kh * hd:(kh + 1) * hd]
        v_ext = jnp.concatenate([v, jnp.ones_like(v)], axis=1)
        for g in range(group):
            cols = slice((kh * group + g) * hd, (kh * group + g + 1) * hd)
            s = lax.dot_general(q_ref[:, cols], k, NT_DIMS, preferred_element_type=F32)
            p = jnp.exp2(s - jnp.max(s, axis=-1, keepdims=True))
            o = jnp.dot(p.astype(BF16), v_ext, preferred_element_type=F32)
            o_ref[:, cols] = (o[:, :hd] / o[:, hd:]).astype(o_ref.dtype)


ATTN_SCORE_BUDGET = 4 * 1024 * 1024


ATTN_Q_ROWS = 1024


def _attn_call(q, k, v, n_seq, seq_len, row0, mix):
    kvw = k.shape[-1]
    qw = q.shape[-1]
    hd = ATTN_HEAD_DIM
    n_kv = kvw // hd
    group = qw // hd // n_kv
    tq = min(ATTN_Q_ROWS, seq_len)
    assert row0 % tq == 0 and seq_len % tq == 0 and row0 % seq_len == 0
    per_seq = seq_len // tq
    rb0 = row0 // tq
    lk = seq_len if k.ndim == 2 else k.shape[1]
    kv_step = n_kv if tq * lk * 4 * n_kv <= ATTN_SCORE_BUDGET else 1
    q_spec = pl.BlockSpec((tq, kv_step * group * hd), lambda n, kh, i: (rb0 + n * per_seq + i, kh))
    if k.ndim == 2:
        kv_spec = pl.BlockSpec((seq_len, kv_step * hd), lambda n, kh, i: (row0 // seq_len + n, kh))
    else:
        kv_spec = pl.BlockSpec((None, lk, kv_step * hd), lambda n, kh, i: (n, 0, kh))
    in_specs = [q_spec, kv_spec, kv_spec]
    args = [q, k, v]
    (mix_shape,), aliases = _join_buffers([mix], in_specs, args)
    return pl.pallas_call(
        functools.partial(_attn_kernel, group=group),
        grid=(n_seq, n_kv // kv_step, per_seq),
        in_specs=in_specs,
        out_specs=q_spec,
        out_shape=mix_shape,
        input_output_aliases=aliases,
        compiler_params=_params("arbitrary", "arbitrary", "arbitrary"),
    )(*args)


def _grid_rope(n_tokens, head_dim):
    rows = n_tokens // GRID_W
    row = jnp.broadcast_to(jnp.arange(rows)[:, None], (rows, GRID_W)).reshape(-1).astype(F32)
    col = jnp.broadcast_to(jnp.arange(GRID_W)[None, :], (rows, GRID_W)).reshape(-1).astype(F32)
    per_axis = head_dim // 4
    freqs = ROPE_THETA ** (-jnp.arange(per_axis, dtype=F32) / per_axis)
    ang = jnp.concatenate([row[:, None] * freqs, col[:, None] * freqs], axis=-1)
    return jnp.cos(ang), jnp.sin(ang)


def kernel(x_prompt, x_sample, c, state_hgrn, state_ret, cache_k, cache_v, c_ctx, w_mod, b_mod, norm_mix, norm_ffn, w_in_even, w_out_even, hgrn_lb_logits, hgrn_onorm, ret_decay_logit, ret_gnorm, w_in_attn, w_out_attn, q_norm, k_norm, w_ffn_gate, w_ffn_up, w_ffn_down):
    n_p, l_p, d = x_prompt.shape
    n_s, l_s, _ = x_sample.shape
    depth = w_mod.shape[0]
    rows_p = n_p * l_p
    rows_s = n_s * l_s
    a_width = hgrn_lb_logits.shape[-1]
    b_width = ret_gnorm.shape[-1]
    kv_width = ATTN_KV_HEADS * ATTN_HEAD_DIM
    row_quantum = math.gcd(rows_p, l_s)

    def group_of_row(r):
        return jnp.where(r < rows_p, 0, 1 + (r - rows_p) // l_s)

    n_groups = 1 + n_s
    cond = jnp.concatenate([c_ctx[None, :], c, jnp.zeros((-(1 + n_s) % SUBLANES, d), F32)], axis=0)
    mods = _adaln_call(cond, w_mod, b_mod)
    mods = mods[:, :n_groups].reshape(depth, n_groups, 6, 1, d)

    cos_a, sin_a = _grid_rope(l_s, ATTN_HEAD_DIM)
    rope_attn = (jnp.concatenate([cos_a, cos_a], axis=-1), jnp.concatenate([-sin_a, sin_a], axis=-1))
    rope_rows = (jnp.concatenate([jnp.ones((rows_p, ATTN_HEAD_DIM), F32), jnp.tile(rope_attn[0], (n_s, 1))]),
                 jnp.concatenate([jnp.zeros((rows_p, ATTN_HEAD_DIM), F32), jnp.tile(rope_attn[1], (n_s, 1))]))
    rope_ret = _grid_rope(l_s, b_width // B_HEADS)
    lb_cum = jnp.cumsum(jax.nn.softmax(hgrn_lb_logits.astype(F32), axis=0), axis=0)
    lower_bounds = lb_cum - lb_cum[0]

    x = jnp.concatenate([x_prompt.reshape(rows_p, d), x_sample.reshape(rows_s, d)], axis=0)
    n_even = w_in_even.shape[0]
    out_hgrn, out_ret = n_even, n_even
    new_k, new_v = [], []
    for l in range(depth):
        mix = _modulate_call(x, norm_mix, mods, l, 0, group_of_row)
        if l % 2 == 0:
            e = l // 2
            p = _mm_plain_call(mix, w_in_even, e, F32)
            mix, out_hgrn = _gla_call(p, lower_bounds, l, hgrn_onorm, e, None, n_p, l_p, 0, out_hgrn, mix)
            (mix,) = _gla_call(p, lower_bounds, l, hgrn_onorm, e, state_hgrn, n_s, l_s, rows_p, None, mix)
            mix, out_ret = _ret_call(p, 5 * a_width, ret_decay_logit, ret_gnorm, e, None, None,
                                     n_p, l_p, 0, out_ret, mix, a_width)
            (mix,) = _ret_call(p, 5 * a_width, ret_decay_logit, ret_gnorm, e, rope_ret, state_ret,
                               n_s, l_s, rows_p, None, mix, a_width)
            w_out, widx = w_out_even, e
        else:
            o = l // 2
            q = _mm_qproj_call(mix, w_in_attn, o, d, q_norm, rope_rows)
            k, v, kf, vf = _mm_kvproj_call(mix, w_in_attn, o, d, k_norm, rope_rows)
            past = cache_k.shape[2]
            with_cache = lambda new, cache: jnp.concatenate(
                [new[rows_p:].reshape(n_s, l_s, kv_width),
                 cache[:, o].reshape(n_s, past, kv_width).astype(BF16)], axis=1)
            k_all, v_all = with_cache(k, cache_k), with_cache(v, cache_v)
            mix = _attn_call(q, k, v, n_p, l_p, 0, mix)
            mix = _attn_call(q, k_all, v_all, n_s, l_s, rows_p, mix)
            w_out, widx = w_out_attn, o
            new_k.append(kf[:rows_p].reshape(n_p, l_p, ATTN_KV_HEADS, ATTN_HEAD_DIM))
            new_v.append(vf[:rows_p].reshape(n_p, l_p, ATTN_KV_HEADS, ATTN_HEAD_DIM))
        x, h = _mm_residual_modulate_call(mix, w_out, widx, x, mods, l, 2, norm_ffn, 3, group_of_row,
                                          row_quantum)
        u = _mm_swiglu_call(h, w_ffn_gate, w_ffn_up, l)
        down = functools.partial(_mm_residual_call, u, w_ffn_down, l, x, mods, l, 5, group_of_row,
                                 row_quantum)
        if l < depth - 1:
            x = down()
    return (down(0, rows_p).reshape(n_p, l_p, d), down(rows_p, rows_s).reshape(n_s, l_s, d),
            out_hgrn, out_ret, jnp.stack(new_k, axis=1), jnp.stack(new_v, axis=1))
```

```python
import functools
import math

import jax
import jax.numpy as jnp
from jax import lax
from jax.experimental import pallas as pl
from jax.experimental.pallas import tpu as pltpu

F32 = jnp.float32
BF16 = jnp.bfloat16

EPS = 1e-6
ROPE_THETA = 10000.0
GRID_W = 64
A_HEAD_DIM = 128
B_HEADS = 4
ATTN_HEAD_DIM = 128
ATTN_KV_HEADS = 4
LANES = 128
SUBLANES = 8
VMEM_LIMIT = 56 * 1024 * 1024
VMEM_TILE_BUDGET = 46 * 1024 * 1024

LOG2_E = math.log2(math.e)

NT_DIMS = (((1,), (1,)), ((), ()))
TN_DIMS = (((0,), (0,)), ((), ()))


def _params(*sem):
    return pltpu.CompilerParams(dimension_semantics=sem, vmem_limit_bytes=VMEM_LIMIT)


def _silu(x):
    return x / (1.0 + jnp.exp(-x))


def _log_sigmoid(x):
    return jnp.minimum(x, 0.0) - jnp.log1p(jnp.exp(-jnp.abs(x)))


def _adaln_tile(c_ref, w_ref, b_ref):
    s = _silu(c_ref[...]).astype(BF16)
    return jnp.dot(s, w_ref[...].astype(BF16), preferred_element_type=F32) + b_ref[...]


def _adaln_kernel(c_ref, w_ref, b_ref, o_ref):
    o_ref[...] = _adaln_tile(c_ref, w_ref, b_ref)


def _adaln_call(cond, w_mod, b_mod, layer, tn=1024):
    depth, d, n = w_mod.shape
    rows = cond.shape[0]
    return pl.pallas_call(
        _adaln_kernel,
        grid=(n // tn,),
        in_specs=[
            pl.BlockSpec((rows, d), lambda j: (0, 0)),
            pl.BlockSpec((None, d, tn), lambda j: (layer, 0, j)),
            pl.BlockSpec((None, 1, tn), lambda j: (layer, 0, j)),
        ],
        out_specs=pl.BlockSpec((rows, tn), lambda j: (0, j)),
        out_shape=jax.ShapeDtypeStruct((rows, n), F32),
        compiler_params=_params("arbitrary"),
    )(cond, w_mod, b_mod.reshape(depth, 1, n))


def _modulate_kernel(x_ref, g_ref, shift_ref, scale_ref, o_ref):
    x = x_ref[...]
    y = x * lax.rsqrt(jnp.mean(x * x, axis=-1, keepdims=True) + EPS) * g_ref[...]
    o_ref[...] = (y * (1.0 + scale_ref[...]) + shift_ref[...]).astype(BF16)


def _modulate_join_kernel(xa_ref, xb_ref, g_ref, shift_ref, scale_ref, o_ref, ox_ref, *, a_blocks):
    def from_ref(x_ref):
        ox_ref[...] = x_ref[...]
        _modulate_kernel(x_ref, g_ref, shift_ref, scale_ref, o_ref)

    pl.when(pl.program_id(0) < a_blocks)(lambda: from_ref(xa_ref))
    pl.when(pl.program_id(0) >= a_blocks)(lambda: from_ref(xb_ref))


def _modulate_join_call(xa, xb, gains, mods, layer, shift_idx, group_of_row, tr=512):
    (ma, d), mb = xa.shape, xb.shape[0]
    depth = gains.shape[0]
    assert ma % tr == 0 and mb % tr == 0
    a_blocks = ma // tr
    grp = lambda i: group_of_row(i * tr)
    row_spec = pl.BlockSpec((tr, d), lambda i: (i, 0))
    return pl.pallas_call(
        functools.partial(_modulate_join_kernel, a_blocks=a_blocks),
        grid=((ma + mb) // tr,),
        in_specs=[
            pl.BlockSpec((tr, d), lambda i: (jnp.minimum(i, a_blocks - 1), 0)),
            pl.BlockSpec((tr, d), lambda i: (jnp.maximum(i - a_blocks, 0), 0)),
            pl.BlockSpec((None, 1, d), lambda i: (layer, 0, 0)),
            pl.BlockSpec((None, None, 1, d), lambda i: (grp(i), shift_idx, 0, 0)),
            pl.BlockSpec((None, None, 1, d), lambda i: (grp(i), shift_idx + 1, 0, 0)),
        ],
        out_specs=[row_spec, row_spec],
        out_shape=[jax.ShapeDtypeStruct((ma + mb, d), BF16), jax.ShapeDtypeStruct((ma + mb, d), F32)],
        compiler_params=_params("arbitrary"),
    )(xa, xb, gains.reshape(depth, 1, d), mods, mods)


def _modulate_call(x, gains, mods, layer, shift_idx, group_of_row, tr=512):
    m, d = x.shape
    depth = gains.shape[0]
    grp = lambda i: group_of_row(i * tr)
    return pl.pallas_call(
        _modulate_kernel,
        grid=(m // tr,),
        in_specs=[
            pl.BlockSpec((tr, d), lambda i: (i, 0)),
            pl.BlockSpec((None, 1, d), lambda i: (layer, 0, 0)),
            pl.BlockSpec((None, None, 1, d), lambda i: (grp(i), shift_idx, 0, 0)),
            pl.BlockSpec((None, None, 1, d), lambda i: (grp(i), shift_idx + 1, 0, 0)),
        ],
        out_specs=pl.BlockSpec((tr, d), lambda i: (i, 0)),
        out_shape=jax.ShapeDtypeStruct((m, d), BF16),
        compiler_params=_params("arbitrary"),
    )(x, gains.reshape(depth, 1, d), mods, mods)


def _matmul_tiles(m, k, n, n_weights, out_bytes, residual, row_quantum):
    for tn in (2048, 1024, 512, 256, 128):
        if n % tn:
            continue
        for tm in (1024, 512, 256):
            if m % tm or row_quantum % tm:
                continue
            need = (2 * tm * k * 2 + n_weights * (2 * k * tn * 4 + k * tn * 2)
                    + 2 * tm * tn * out_bytes + (2 * tm * tn * 4 if residual else 0))
            if need <= VMEM_TILE_BUDGET:
                return tm, tn
    raise ValueError(f"no matmul tiling fits VMEM for {(m, k, n)}")


def _cast_weights_once(w_refs, wbf_refs):
    @pl.when(pl.program_id(1) == 0)
    def _():
        for w_ref, wbf_ref in zip(w_refs, wbf_refs):
            wbf_ref[...] = w_ref[...].astype(BF16)


def _mm_plain_kernel(a_ref, w_ref, o_ref, wbf_ref):
    _cast_weights_once((w_ref,), (wbf_ref,))
    o_ref[...] = jnp.dot(a_ref[...], wbf_ref[...], preferred_element_type=F32).astype(o_ref.dtype)


def _mm_swiglu_kernel(*refs, adaln_steps):
    if adaln_steps:
        a_ref, wg_ref, wu_ref, c_ref, wm_ref, bm_ref, o_ref, om_ref, wgbf_ref, wubf_ref = refs
    else:
        a_ref, wg_ref, wu_ref, o_ref, wgbf_ref, wubf_ref = refs
    _cast_weights_once((wg_ref, wu_ref), (wgbf_ref, wubf_ref))
    a = a_ref[...]
    g = jnp.dot(a, wgbf_ref[...], preferred_element_type=F32)
    u = jnp.dot(a, wubf_ref[...], preferred_element_type=F32)
    o_ref[...] = (_silu(g) * u).astype(o_ref.dtype)
    if adaln_steps:
        @pl.when(pl.program_id(0) * pl.num_programs(1) + pl.program_id(1) < adaln_steps)
        def _():
            om_ref[...] = _adaln_tile(c_ref, wm_ref, bm_ref)


def _mm_residual_kernel(a_ref, w_ref, x_ref, gate_ref, o_ref, wbf_ref):
    _cast_weights_once((w_ref,), (wbf_ref,))
    y = jnp.dot(a_ref[...], wbf_ref[...], preferred_element_type=F32)
    o_ref[...] = x_ref[...] + gate_ref[...] * y


def _mm_residual_modulate_kernel(a_ref, w_ref, x_ref, gate_ref, g_ref, shift_ref, scale_ref,
                                 ox_ref, oh_ref, wbf_ref):
    @pl.when(pl.program_id(0) == 0)
    def _():
        wbf_ref[...] = w_ref[...].astype(BF16)
    y = jnp.dot(a_ref[...], wbf_ref[...], preferred_element_type=F32)
    x = x_ref[...] + gate_ref[...] * y
    ox_ref[...] = x
    h = x * lax.rsqrt(jnp.mean(x * x, axis=-1, keepdims=True) + EPS) * g_ref[...]
    oh_ref[...] = (h * (1.0 + scale_ref[...]) + shift_ref[...]).astype(BF16)


def _mm_residual_modulate_call(a, w, widx, x, mods, layer, gate_idx, gains, shift_idx, group_of_row,
                               row_quantum):
    m, k = a.shape
    n = w.shape[-1]
    depth = gains.shape[0]
    for tm in (512, 256, 128):
        need = k * n * (4 + 2) + 2 * tm * (k * 2 + n * (4 + 4 + 2))
        if m % tm == 0 and row_quantum % tm == 0 and need <= VMEM_TILE_BUDGET:
            break
    else:
        raise ValueError(f"no row tile fits VMEM for {(m, k, n)}")
    grp = lambda i: group_of_row(i * tm)
    mod_spec = lambda which: pl.BlockSpec((None, None, 1, n), lambda i: (grp(i), which, 0, 0))
    row_spec = pl.BlockSpec((tm, n), lambda i: (i, 0))
    return pl.pallas_call(
        _mm_residual_modulate_kernel,
        grid=(m // tm,),
        in_specs=[
            pl.BlockSpec((tm, k), lambda i: (i, 0)),
            pl.BlockSpec((None, k, n), lambda i: (widx, 0, 0), pipeline_mode=pl.Buffered(1)),
            row_spec,
            mod_spec(gate_idx),
            pl.BlockSpec((None, 1, n), lambda i: (layer, 0, 0)),
            mod_spec(shift_idx),
            mod_spec(shift_idx + 1),
        ],
        out_specs=[row_spec, row_spec],
        out_shape=[jax.ShapeDtypeStruct((m, n), F32), jax.ShapeDtypeStruct((m, n), BF16)],
        scratch_shapes=[pltpu.VMEM((k, n), BF16)],
        compiler_params=_params("arbitrary"),
    )(a, w, x, mods, gains.reshape(depth, 1, n), mods, mods)


def _mm_plain_call(a, w, widx, out_dtype):
    m, k = a.shape
    n = w.shape[-1]
    tm, tn = _matmul_tiles(m, k, n, 1, jnp.dtype(out_dtype).itemsize, False, m)
    return pl.pallas_call(
        _mm_plain_kernel,
        grid=(n // tn, m // tm),
        in_specs=[
            pl.BlockSpec((tm, k), lambda j, i: (i, 0)),
            pl.BlockSpec((None, k, tn), lambda j, i: (widx, 0, j)),
        ],
        out_specs=pl.BlockSpec((tm, tn), lambda j, i: (i, j)),
        out_shape=jax.ShapeDtypeStruct((m, n), out_dtype),
        scratch_shapes=[pltpu.VMEM((k, tn), BF16)],
        compiler_params=_params("arbitrary", "arbitrary"),
    )(a, w)


ADALN_TILE = 256


def _mm_swiglu_call(a, wg, wu, widx, adaln=None):
    m, k = a.shape
    n = wg.shape[-1]
    tm, tn = _matmul_tiles(m, k, n, 2, 2, False, m)
    wspec = pl.BlockSpec((None, k, tn), lambda j, i: (widx, 0, j))
    in_specs = [pl.BlockSpec((tm, k), lambda j, i: (i, 0)), wspec, wspec]
    args = [a, wg, wu]
    out_specs = [pl.BlockSpec((tm, tn), lambda j, i: (i, j))]
    out_shape = [jax.ShapeDtypeStruct((m, n), BF16)]
    adaln_steps = 0
    if adaln is not None:
        cond, w_mod, b_mod, layer = adaln
        depth, d, n_mod = w_mod.shape
        rows = cond.shape[0]
        adaln_steps = n_mod // ADALN_TILE
        per_j = m // tm
        assert n_mod % ADALN_TILE == 0 and adaln_steps <= (n // tn) * per_j
        tile = lambda j, i: jnp.minimum(j * per_j + i, adaln_steps - 1)
        in_specs += [pl.BlockSpec((rows, d), lambda j, i: (0, 0)),
                     pl.BlockSpec((None, d, ADALN_TILE), lambda j, i: (layer, 0, tile(j, i))),
                     pl.BlockSpec((None, 1, ADALN_TILE), lambda j, i: (layer, 0, tile(j, i)))]
        args += [cond, w_mod, b_mod.reshape(depth, 1, n_mod)]
        out_specs.append(pl.BlockSpec((rows, ADALN_TILE), lambda j, i: (0, tile(j, i))))
        out_shape.append(jax.ShapeDtypeStruct((rows, n_mod), F32))
    outs = pl.pallas_call(
        functools.partial(_mm_swiglu_kernel, adaln_steps=adaln_steps),
        grid=(n // tn, m // tm),
        in_specs=in_specs,
        out_specs=out_specs,
        out_shape=out_shape,
        scratch_shapes=[pltpu.VMEM((k, tn), BF16), pltpu.VMEM((k, tn), BF16)],
        compiler_params=_params("arbitrary", "arbitrary"),
    )(*args)
    return outs if adaln is not None else outs[0]


def _mm_residual_call(a, w, widx, x, mods, gate_idx, group_of_row, row_quantum, row0=0, m=None):
    k = a.shape[1]
    m = a.shape[0] if m is None else m
    n = w.shape[-1]
    tm, tn = _matmul_tiles(m, k, n, 1, 4, True, row_quantum)
    assert row0 % tm == 0
    rb0 = row0 // tm
    grp = lambda i: group_of_row((i + rb0) * tm)
    return pl.pallas_call(
        _mm_residual_kernel,
        grid=(n // tn, m // tm),
        in_specs=[
            pl.BlockSpec((tm, k), lambda j, i: (i + rb0, 0)),
            pl.BlockSpec((None, k, tn), lambda j, i: (widx, 0, j)),
            pl.BlockSpec((tm, tn), lambda j, i: (i + rb0, j)),
            pl.BlockSpec((None, None, 1, tn), lambda j, i: (grp(i), gate_idx, 0, j)),
        ],
        out_specs=pl.BlockSpec((tm, tn), lambda j, i: (i, j)),
        out_shape=jax.ShapeDtypeStruct((m, n), F32),
        scratch_shapes=[pltpu.VMEM((k, tn), BF16)],
        compiler_params=_params("arbitrary", "arbitrary"),
    )(a, w, x, mods)


def _bcast_row_in_blocks(x, block, r):
    c, d = x.shape
    x3 = x.reshape(c // block, block, d)
    return jnp.broadcast_to(x3[:, r:r + 1, :], x3.shape).reshape(c, d)


MXU_WIDTH = 256


def _gla_blocks(c):
    return [1 << b for b in range((c // 2).bit_length())] + [0]


def _gla_masks(c, fwd):
    i = lax.broadcasted_iota(jnp.int32, (c, LANES), 0)
    lane = lax.broadcasted_iota(jnp.int32, (c, LANES), 1)
    masks = []
    for b, h in enumerate(_gla_blocks(c)):
        j = lane - (b * c) % LANES
        in_block = jnp.where(j >= 0, jnp.where(j < c, 1.0, 0.0), 0.0)
        if h == 0:
            ok = jnp.where(i == j, 1.0, 0.0)
        else:
            same = (i // (2 * h)) == (j // (2 * h))
            i_hi = (i % (2 * h)) >= h
            j_hi = (j % (2 * h)) >= h
            if fwd:
                ok = jnp.where(same, jnp.where(i_hi, jnp.where(j_hi, 0.0, 1.0), 0.0), 0.0)
            else:
                ok = jnp.where(same, jnp.where(i_hi, 0.0, jnp.where(j_hi, 1.0, 0.0)), 0.0)
        masks.append((ok * in_block).astype(F32))
    return masks


def _split_roles(q, kg, h, fwd, rloc):
    c = q.shape[0]
    if h >= SUBLANES:
        parts = []
        for b in range(c // (2 * h)):
            lo = slice(2 * h * b, 2 * h * b + h)
            hi = slice(2 * h * b + h, 2 * h * (b + 1))
            parts += [kg[lo], q[hi]] if fwd else [q[lo], kg[hi]]
        return jnp.concatenate(parts, axis=0)
    in_hi = (rloc % (2 * h)) >= h
    return jnp.where(in_hi, q, kg) if fwd else jnp.where(in_hi, kg, q)


def _split3(x):
    hi = x.astype(BF16)
    r1 = x - hi.astype(F32)
    mid = r1.astype(BF16)
    lo = (r1 - mid.astype(F32)).astype(BF16)
    return jnp.concatenate([hi, mid, lo], axis=-1)


def _gla_chunks(chains, rloc):
    c, d = chains[0][0].shape
    per_group = MXU_WIDTH // c
    blocks = _gla_blocks(c)
    n_groups = -(-len(blocks) // per_group)
    n_kvregs = -(-len(blocks) * c // LANES)

    es = [None] * len(chains)
    for direction in (True, False):
        ids = [n for n, ch in enumerate(chains) if ch[6] == direction]
        if ids:
            e3 = jnp.dot(chains[ids[0]][7], jnp.concatenate([chains[n][4] for n in ids], axis=1),
                         preferred_element_type=F32)
            for pos, n in enumerate(ids):
                base = 3 * d * pos
                es[n] = e3[:, base:base + d] + e3[:, base + d:base + 2 * d] + e3[:, base + 2 * d:base + 3 * d]

    sides = []
    for (q, kg, fg, v, logf, st, fwd, tri_bf, masks), e in zip(chains, es):
        qs, ks = [], []
        for h in blocks:
            if h == 0:
                qs.append(q.astype(BF16))
                ks.append(kg.astype(BF16))
                continue
            if h == 1:
                in_hi = (rloc % 2) >= 1
                w = jnp.where(in_hi, q * fg, kg) if fwd else jnp.where(in_hi, kg, q * fg)
            else:
                r = h - 1 if fwd else h
                if 2 * h >= SUBLANES:
                    e_ref = _bcast_row_in_blocks(e, 2 * h, r)
                else:
                    e_ref = jnp.where(rloc < 2 * h, _bcast_row_in_blocks(e, SUBLANES, r),
                                      _bcast_row_in_blocks(e, SUBLANES, 2 * h + r))
                w = _split_roles(q, kg, h, fwd, rloc) * (1.0 / jnp.exp2(jnp.abs(e - e_ref)))
            qs.append(w.astype(BF16))
            ks.append(qs[-1])
        pad = [jnp.zeros((c, d), BF16)] * (n_groups * per_group - len(blocks))
        sides.append((qs + pad, ks + pad))

    atts = [[None] * n_kvregs for _ in chains]
    for g in range(n_groups):
        for ci, (qs, ks) in enumerate(sides):
            sl = slice(g * per_group, (g + 1) * per_group)
            prod = lax.dot_general(jnp.concatenate(qs[sl], axis=0), jnp.concatenate(ks[sl], axis=0),
                                   NT_DIMS, preferred_element_type=F32)
            masks = chains[ci][8]
            for b in range(g * per_group, min((g + 1) * per_group, len(blocks))):
                t = b * c // LANES
                col = (t * LANES) % MXU_WIDTH
                row = (b - g * per_group) * c
                term = prod[row:row + c, col:col + LANES] * masks[b]
                atts[ci][t] = term if atts[ci][t] is None else atts[ci][t] + term

    outs = []
    for (q, kg, fg, v, logf, st, fwd, tri_bf, masks), e, att in zip(chains, es, atts):
        vb = v.astype(BF16)
        att_sum = functools.reduce(lambda x, y: x + y, att)
        v_rep = jnp.concatenate([vb] * (LANES // c), axis=0)
        o = jnp.dot(att_sum.astype(BF16), v_rep, preferred_element_type=F32)
        o = o + lax.dot_general((q * jnp.exp2(e)).astype(BF16), st.astype(BF16), NT_DIMS,
                                preferred_element_type=F32)
        e_edge = e[c - 1:c] if fwd else e[0:1]
        kh = (kg * jnp.exp2(e_edge - e)).astype(BF16)
        st_new = st * jnp.exp2(e_edge) + lax.dot_general(vb, kh, TN_DIMS, preferred_element_type=F32)
        outs.append((o, st_new))
    return outs


def _gla_kernel(*refs, seq_len, chunk, has_s0, final):
    refs = list(refs)
    q_ref, ffw_ref, fbw_ref, v_ref, g_ref, lb_ref, on_ref = refs[:7]
    del refs[:7]
    s0_ref = refs.pop(0) if has_s0 else None
    del refs[:2 if final == "join" else 1]
    o_ref = refs.pop(0)
    sf_ref = _final_state_slot(refs.pop(0), final) if final is not None else None
    stf_ref, stb_ref, oacc_ref = refs
    c = chunk
    nch = seq_len // c
    half_n = nch // 2
    d = A_HEAD_DIM
    n_h = q_ref.shape[-1] // d

    i = lax.broadcasted_iota(jnp.int32, (c, c), 0)
    j = lax.broadcasted_iota(jnp.int32, (c, c), 1)
    tri_f = jnp.where(i >= j, 1.0, 0.0).astype(BF16)
    tri_b = jnp.where(i <= j, 1.0, 0.0).astype(BF16)
    masks_f = _gla_masks(c, True)
    masks_b = _gla_masks(c, False)
    rloc = lax.broadcasted_iota(jnp.int32, (c, d), 0) % SUBLANES
    q_scale = d ** -0.5

    def gates(x, lb):
        one_m_lb = 1.0 - lb
        t = jnp.exp(-jnp.abs(x))
        r = 1.0 / (1.0 + t)
        pos = x >= 0.0
        fg = lb + one_m_lb * jnp.where(pos, r, t * r)
        kg = one_m_lb * jnp.where(pos, t * r, r)
        log2f = jnp.maximum(jnp.log2(fg),
                            jnp.log2(one_m_lb) + jnp.minimum(x, 0.0) * LOG2_E - jnp.log2(1.0 + t))
        return fg, kg, log2f

    def chain_rows(t, fwd):
        return pl.ds(pl.multiple_of((t if fwd else nch - 1 - t) * c, c), c)

    chain_ids = [(hh, fwd) for hh in range(n_h) for fwd in (True, False)]

    def all_chains(t):
        where, chains = [], []
        for hh, fwd in chain_ids:
            sl, cols = chain_rows(t, fwd), slice(hh * d, (hh + 1) * d)
            fg, kg, log2f = gates((ffw_ref if fwd else fbw_ref)[sl, cols], lb_ref[:, cols])
            where.append((sl, cols))
            chains.append((_silu(q_ref[sl, cols]) * q_scale, kg, fg, v_ref[sl, cols], _split3(log2f),
                           (stf_ref if fwd else stb_ref)[hh], fwd,
                           tri_f if fwd else tri_b, masks_f if fwd else masks_b))
        outs = _gla_chunks(chains, rloc)
        for n, (o, st) in enumerate(outs):
            (stf_ref if n % 2 == 0 else stb_ref)[n // 2] = st
        return [(sl, cols, o) for (sl, cols), (o, st) in zip(where, outs)]

    def finish(sl, cols, o):
        y = oacc_ref[sl, cols] + o
        y = y * lax.rsqrt(jnp.mean(y * y, axis=-1, keepdims=True) + EPS) * on_ref[...]
        o_ref[sl, cols] = (y * _silu(g_ref[sl, cols])).astype(o_ref.dtype)

    def first_half(t, carry):
        for sl, cols, o in all_chains(t):
            oacc_ref[sl, cols] = o
        return carry

    def second_half(t, carry):
        for sl, cols, o in all_chains(t):
            finish(sl, cols, o)
        return carry

    for hh in range(n_h):
        for st_ref, direction in ((stf_ref, 0), (stb_ref, 1)):
            st_ref[hh] = s0_ref[direction, hh].T if has_s0 else jnp.zeros((d, d), F32)
    unroll = 2 if 2 * n_h < 8 and half_n % 2 == 0 else 1
    lax.fori_loop(0, half_n, first_half, 0, unroll=unroll)
    lax.fori_loop(half_n, nch, second_half, 0, unroll=unroll)
    if final is not None:
        for hh in range(n_h):
            sf_ref[0, hh] = stf_ref[hh].T
            sf_ref[1, hh] = stb_ref[hh].T


GLA_CHUNK = 64


def _join_buffers(bufs, in_specs, args):
    shapes, aliases = [], {}
    for n, buf in enumerate(bufs):
        in_specs.append(pl.BlockSpec(memory_space=pl.ANY))
        args.append(buf)
        shapes.append(jax.ShapeDtypeStruct(buf.shape, buf.dtype))
        aliases[len(args) - 1] = n
    return shapes, aliases


def _final_state_specs(finals, slot, dims, heads_per_step, out_specs):
    n_seq, heads, d = dims
    if finals is None:
        return [], None
    if isinstance(finals, int):
        out_specs.append(pl.BlockSpec((None, finals, 2, heads_per_step, d, d),
                                      lambda n, h: (n, 0, 0, h, 0, 0)))
        return [], (slot, finals)
    out_specs.append(pl.BlockSpec((None, None, 2, heads_per_step, d, d),
                                  lambda n, h: (n, slot, 0, h, 0, 0)))
    return [finals], "join"


def _final_state_slot(sf_ref, final):
    if final == "join":
        return sf_ref
    slot, n_slots = final
    for s in range(n_slots):
        if s != slot:
            sf_ref[s] = jnp.zeros(sf_ref.shape[1:], sf_ref.dtype)
    return sf_ref.at[slot]


def _gla_heads_per_step(seq_len):
    for n_h in (4, 2, 1):
        if 2 * 5 * seq_len * n_h * A_HEAD_DIM * 4 + 4 * seq_len * n_h * A_HEAD_DIM * 2 <= VMEM_TILE_BUDGET:
            return n_h
    raise ValueError(f"sequence of {seq_len} rows does not fit VMEM")


def _gla_call(p, lower_bounds, layer, onorm, eidx, s0, n_seq, seq_len, row0, finals, mix):
    d = A_HEAD_DIM
    n_h = _gla_heads_per_step(seq_len)
    w = n_h * d
    a_width = lower_bounds.shape[-1]
    n_hb = a_width // w
    assert row0 % seq_len == 0 and seq_len % (2 * GLA_CHUNK) == 0 and a_width % w == 0
    sb = row0 // seq_len
    slab = lambda k: pl.BlockSpec((seq_len, w), lambda n, h: (n + sb, k * n_hb + h))
    in_specs = [slab(0), slab(1), slab(2), slab(3), slab(4),
                pl.BlockSpec((None, 1, w), lambda n, h: (layer, 0, h)),
                pl.BlockSpec((None, 1, d), lambda n, h: (eidx, 0, 0))]
    args = [p, p, p, p, p, lower_bounds.reshape(lower_bounds.shape[0], 1, a_width),
            onorm.reshape(onorm.shape[0], 1, d)]
    if s0 is not None:
        in_specs.append(pl.BlockSpec((None, None, 2, n_h, d, d), lambda n, h: (n, eidx, 0, h, 0, 0)))
        args.append(s0)
    out_specs = [pl.BlockSpec((seq_len, w), lambda n, h: (n + sb, h))]
    joined, final = _final_state_specs(finals, eidx, (n_seq, a_width // d, d), n_h, out_specs)
    out_shape, aliases = _join_buffers([mix] + joined, in_specs, args)
    if isinstance(final, tuple):
        out_shape.append(jax.ShapeDtypeStruct((n_seq, final[1], 2, a_width // d, d, d), F32))
    kern = functools.partial(_gla_kernel, seq_len=seq_len, chunk=GLA_CHUNK,
                             has_s0=s0 is not None, final=final)
    return pl.pallas_call(
        kern,
        grid=(n_seq, n_hb),
        in_specs=in_specs,
        out_specs=out_specs,
        out_shape=out_shape,
        input_output_aliases=aliases,
        scratch_shapes=[pltpu.VMEM((n_h, d, d), F32), pltpu.VMEM((n_h, d, d), F32),
                        pltpu.VMEM((seq_len, w), F32)],
        compiler_params=_params("arbitrary", "arbitrary"),
    )(*args)


def _ret_kernel(*refs, seq_len, chunk, has_s0, final, rope):
    refs = list(refs)
    q_ref, k_ref, v_ref, g_ref, dl_ref, gn_ref = refs[:6]
    del refs[:6]
    cos_ref = refs.pop(0) if rope else None
    sin_ref = refs.pop(0) if rope else None
    s0_ref = refs.pop(0) if has_s0 else None
    del refs[:2 if final == "join" else 1]
    o_ref = refs.pop(0)
    sf_ref = _final_state_slot(refs.pop(0), final) if final is not None else None
    stf_ref, stb_ref, oacc_ref = refs
    c = chunk
    nch = seq_len // c
    half_n = nch // 2
    d = dl_ref.shape[-1]
    n_h = q_ref.shape[-1] // d
    half = d // 2
    k_scale = d ** -0.5

    dist = (lax.broadcasted_iota(jnp.int32, (c, c), 0)
            - lax.broadcasted_iota(jnp.int32, (c, c), 1)).astype(F32)
    low = dist >= 0.0
    up = dist <= 0.0
    row = lax.broadcasted_iota(jnp.int32, (c, d), 0).astype(F32)

    def decay_tables(hh):
        lg_f = _log_sigmoid(dl_ref[0, hh])
        lg_b = _log_sigmoid(dl_ref[1, hh])
        intra = (jnp.where(low, jnp.exp(jnp.where(low, dist, 0.0) * lg_f[:, :c]), 0.0)
                 + jnp.where(up, jnp.exp(jnp.where(up, -dist, 0.0) * lg_b[:, :c]), 0.0))
        fwd = (jnp.exp((row + 1.0) * lg_f), jnp.exp((c - 1.0 - row) * lg_f), jnp.exp(c * lg_f))
        bwd = (jnp.exp((c - row) * lg_b), jnp.exp(row * lg_b), jnp.exp(c * lg_b))
        return intra, fwd, bwd

    tables = [decay_tables(hh) for hh in range(n_h)]

    def rotate(x, sl):
        if not rope:
            return x
        cos, sin = cos_ref[sl, :], sin_ref[sl, :]
        x1, x2 = x[:, :half], x[:, half:]
        return jnp.concatenate([x1 * cos - x2 * sin, x1 * sin + x2 * cos], axis=-1)

    def load(ci, hh):
        sl, cols = pl.ds(pl.multiple_of(ci * c, c), c), slice(hh * d, (hh + 1) * d)
        return (sl, cols, rotate(q_ref[sl, cols], sl), rotate(k_ref[sl, cols] * k_scale, sl),
                v_ref[sl, cols].astype(BF16))

    def carried(q, k, vb, st_ref, hh, decs):
        q_dec, k_dec, c_dec = decs
        st = st_ref[hh]
        o = jnp.dot((q * q_dec).astype(BF16), st.astype(BF16), preferred_element_type=F32)
        st_ref[hh] = c_dec * st + lax.dot_general((k * k_dec).astype(BF16), vb, TN_DIMS,
                                                  preferred_element_type=F32)
        return o

    def trip(t):
        left = [load(t, hh) for hh in range(n_h)]
        right = [load(nch - 1 - t, hh) for hh in range(n_h)]
        scores = [lax.dot_general(q.astype(BF16), k.astype(BF16), NT_DIMS, preferred_element_type=F32)
                  for _, _, q, k, _ in left]
        o_left = [carried(q, k, vb, stf_ref, hh, tables[hh][1]) for hh, (_, _, q, k, vb) in enumerate(left)]
        o_right = [carried(q, k, vb, stb_ref, hh, tables[hh][2]) for hh, (_, _, q, k, vb) in enumerate(right)]
        for hh, (_, _, _, _, vb) in enumerate(left):
            o_left[hh] = o_left[hh] + jnp.dot((scores[hh] * tables[hh][0]).astype(BF16), vb,
                                              preferred_element_type=F32)
        return ([(sl, cols, o) for (sl, cols, *_), o in zip(left, o_left)]
                + [(sl, cols, o) for (sl, cols, *_), o in zip(right, o_right)])

    def finish(sl, cols, o):
        y = oacc_ref[sl, cols] + o
        yc = y - jnp.mean(y, axis=-1, keepdims=True)
        yn = yc * lax.rsqrt(jnp.mean(yc * yc, axis=-1, keepdims=True) + EPS)
        o_ref[sl, cols] = (yn * gn_ref[:, cols] * _silu(g_ref[sl, cols])).astype(o_ref.dtype)

    def first_half(t, carry):
        for sl, cols, o in trip(t):
            oacc_ref[sl, cols] = o
        return carry

    def second_half(t, carry):
        for sl, cols, o in trip(t):
            finish(sl, cols, o)
        return carry

    for hh in range(n_h):
        for st_ref, direction in ((stf_ref, 0), (stb_ref, 1)):
            st_ref[hh] = s0_ref[direction, hh] if has_s0 else jnp.zeros((d, d), F32)
    lax.fori_loop(0, half_n, first_half, 0)
    lax.fori_loop(half_n, nch, second_half, 0)
    if final is not None:
        for hh in range(n_h):
            sf_ref[0, hh] = stf_ref[hh]
            sf_ref[1, hh] = stb_ref[hh]


def _ret_call(p, col0, decay_logit, gnorm, eidx, rope_tabs, s0, n_seq, seq_len, row0, finals,
              mix, mix_col0, chunk=128):
    n_heads = decay_logit.shape[-1]
    b_width = gnorm.shape[-1]
    d = b_width // n_heads
    n_h = 2 if n_heads % 2 == 0 else 1
    w = n_h * d
    n_hb = n_heads // n_h
    assert row0 % seq_len == 0 and seq_len % (2 * chunk) == 0 and col0 % w == 0 and chunk <= d
    assert mix_col0 % w == 0
    sb = row0 // seq_len
    cb0 = col0 // w
    ocb0 = mix_col0 // w
    slab = lambda k: pl.BlockSpec((seq_len, w), lambda n, h: (n + sb, cb0 + k * n_hb + h))
    dl = jnp.broadcast_to(decay_logit[:, :, :, None, None], decay_logit.shape + (1, d))
    in_specs = [slab(0), slab(1), slab(2), slab(3),
                pl.BlockSpec((None, 2, n_h, 1, d), lambda n, h: (eidx, 0, h, 0, 0)),
                pl.BlockSpec((None, 1, w), lambda n, h: (eidx, 0, h))]
    args = [p, p, p, p, dl, gnorm.reshape(gnorm.shape[0], 1, b_width)]
    if rope_tabs is not None:
        tab = pl.BlockSpec((seq_len, d // 2), lambda n, h: (0, 0))
        in_specs += [tab, tab]
        args += list(rope_tabs)
    if s0 is not None:
        in_specs.append(pl.BlockSpec((None, None, 2, n_h, d, d), lambda n, h: (n, eidx, 0, h, 0, 0)))
        args.append(s0)
    out_specs = [pl.BlockSpec((seq_len, w), lambda n, h: (n + sb, ocb0 + h))]
    joined, final = _final_state_specs(finals, eidx, (n_seq, n_heads, d), n_h, out_specs)
    out_shape, aliases = _join_buffers([mix] + joined, in_specs, args)
    if isinstance(final, tuple):
        out_shape.append(jax.ShapeDtypeStruct((n_seq, final[1], 2, n_heads, d, d), F32))
    kern = functools.partial(_ret_kernel, seq_len=seq_len, chunk=chunk, has_s0=s0 is not None,
                             final=final, rope=rope_tabs is not None)
    return pl.pallas_call(
        kern,
        grid=(n_seq, n_hb),
        in_specs=in_specs,
        out_specs=out_specs,
        out_shape=out_shape,
        input_output_aliases=aliases,
        scratch_shapes=[pltpu.VMEM((n_h, d, d), F32), pltpu.VMEM((n_h, d, d), F32),
                        pltpu.VMEM((seq_len, w), F32)],
        compiler_params=_params("arbitrary", "arbitrary"),
    )(*args)


Q_PRESCALE = ATTN_HEAD_DIM ** -0.5 * LOG2_E


def _head_norm(x, gain):
    return x * lax.rsqrt(jnp.mean(x * x, axis=-1, keepdims=True) + EPS) * gain


def _rotate(y, cos, sin):
    return y * cos + pltpu.roll(y, y.shape[-1] // 2, 1) * sin


def _mm_qproj_kernel(a_ref, w_ref, gain_ref, cos_ref, sin_ref, o_ref, wbf_ref):
    _cast_weights_once((w_ref,), (wbf_ref,))
    y = jnp.dot(a_ref[...], wbf_ref[...], preferred_element_type=F32)
    hd = ATTN_HEAD_DIM
    for h in range(y.shape[-1] // hd):
        cols = slice(h * hd, (h + 1) * hd)
        z = _rotate(_head_norm(y[:, cols], gain_ref[...]), cos_ref[...], sin_ref[...])
        o_ref[:, cols] = (z * Q_PRESCALE).astype(BF16)


def _mm_qproj_call(a, w, widx, n_cols, q_norm, rope_rows):
    m, k = a.shape
    hd = ATTN_HEAD_DIM
    tm, tn = _matmul_tiles(m, k, n_cols, 1, 2, False, m)
    tab = pl.BlockSpec((tm, hd), lambda j, i: (i, 0))
    return pl.pallas_call(
        _mm_qproj_kernel,
        grid=(n_cols // tn, m // tm),
        in_specs=[
            pl.BlockSpec((tm, k), lambda j, i: (i, 0)),
            pl.BlockSpec((None, k, tn), lambda j, i: (widx, 0, j)),
            pl.BlockSpec((None, 1, hd), lambda j, i: (widx, 0, 0)),
            tab, tab,
        ],
        out_specs=pl.BlockSpec((tm, tn), lambda j, i: (i, j)),
        out_shape=jax.ShapeDtypeStruct((m, n_cols), BF16),
        scratch_shapes=[pltpu.VMEM((k, tn), BF16)],
        compiler_params=_params("arbitrary", "arbitrary"),
    )(a, w, q_norm.reshape(-1, 1, hd), *rope_rows)


def _mm_kvproj_kernel(a_ref, w_ref, gain_ref, cos_ref, sin_ref, k_out, v_out, kf_out, vf_out, wbf_ref):
    _cast_weights_once((w_ref,), (wbf_ref,))
    y = jnp.dot(a_ref[...], wbf_ref[...], preferred_element_type=F32)
    hd = ATTN_HEAD_DIM
    kvw = y.shape[-1] // 2
    for h in range(kvw // hd):
        cols = slice(h * hd, (h + 1) * hd)
        z = _rotate(_head_norm(y[:, cols], gain_ref[...]), cos_ref[...], sin_ref[...])
        k_out[:, cols] = z.astype(BF16)
        kf_out[:, cols] = z
    v = y[:, kvw:]
    v_out[...] = v.astype(BF16)
    vf_out[...] = v


def _mm_kvproj_call(a, w, widx, col0, k_norm, rope_rows):
    m, k = a.shape
    hd = ATTN_HEAD_DIM
    kvw = ATTN_KV_HEADS * hd
    tn = 2 * kvw
    assert col0 % tn == 0
    tm, _ = _matmul_tiles(m, k, tn, 1, 2 + 4, False, m)
    tab = pl.BlockSpec((tm, hd), lambda j, i: (i, 0))
    row_spec = pl.BlockSpec((tm, kvw), lambda j, i: (i, 0))
    return pl.pallas_call(
        _mm_kvproj_kernel,
        grid=(1, m // tm),
        in_specs=[pl.BlockSpec((tm, k), lambda j, i: (i, 0)),
                  pl.BlockSpec((None, k, tn), lambda j, i: (widx, 0, col0 // tn)),
                  pl.BlockSpec((None, 1, hd), lambda j, i: (widx, 0, 0)),
                  tab, tab],
        out_specs=[row_spec] * 4,
        out_shape=[jax.ShapeDtypeStruct((m, kvw), BF16)] * 2 + [jax.ShapeDtypeStruct((m, kvw), F32)] * 2,
        scratch_shapes=[pltpu.VMEM((k, tn), BF16)],
        compiler_params=_params("arbitrary", "arbitrary"),
    )(a, w, k_norm.reshape(-1, 1, hd), *rope_rows)


def _attn_kernel(*refs, group):
    q_ref, k_ref, v_ref, _, o_ref = refs
    hd = ATTN_HEAD_DIM
    for kh in range(k_ref.shape[-1] // hd):
        k = k_ref[:, kh * hd:(kh + 1) * hd]
        v = v_ref[:, kh * hd:(kh + 1) * hd]
        v_ext = jnp.concatenate([v, jnp.ones_like(v)], axis=1)
        for g in range(group):
            cols = slice((kh * group + g) * hd, (kh * group + g + 1) * hd)
            s = lax.dot_general(q_ref[:, cols], k, NT_DIMS, preferred_element_type=F32)
            p = jnp.exp2(s - jnp.max(s, axis=-1, keepdims=True))
            o = jnp.dot(p.astype(BF16), v_ext, preferred_element_type=F32)
            o_ref[:, cols] = (o[:, :hd] / o[:, hd:]).astype(o_ref.dtype)


ATTN_SCORE_BUDGET = 4 * 1024 * 1024


ATTN_Q_ROWS = 1024


def _attn_call(q, k, v, n_seq, seq_len, row0, mix):
    kvw = k.shape[-1]
    qw = q.shape[-1]
    hd = ATTN_HEAD_DIM
    n_kv = kvw // hd
    group = qw // hd // n_kv
    tq = min(ATTN_Q_ROWS, seq_len)
    assert row0 % tq == 0 and seq_len % tq == 0 and row0 % seq_len == 0
    per_seq = seq_len // tq
    rb0 = row0 // tq
    lk = seq_len if k.ndim == 2 else k.shape[1]
    kv_step = n_kv if tq * lk * 4 * n_kv <= ATTN_SCORE_BUDGET else 1
    q_spec = pl.BlockSpec((tq, kv_step * group * hd), lambda n, kh, i: (rb0 + n * per_seq + i, kh))
    if k.ndim == 2:
        kv_spec = pl.BlockSpec((seq_len, kv_step * hd), lambda n, kh, i: (row0 // seq_len + n, kh))
    else:
        kv_spec = pl.BlockSpec((None, lk, kv_step * hd), lambda n, kh, i: (n, 0, kh))
    in_specs = [q_spec, kv_spec, kv_spec]
    args = [q, k, v]
    (mix_shape,), aliases = _join_buffers([mix], in_specs, args)
    return pl.pallas_call(
        functools.partial(_attn_kernel, group=group),
        grid=(n_seq, n_kv // kv_step, per_seq),
        in_specs=in_specs,
        out_specs=q_spec,
        out_shape=mix_shape,
        input_output_aliases=aliases,
        compiler_params=_params("arbitrary", "arbitrary", "arbitrary"),
    )(*args)


def _grid_rope(n_tokens, head_dim):
    rows = n_tokens // GRID_W
    row = jnp.broadcast_to(jnp.arange(rows)[:, None], (rows, GRID_W)).reshape(-1).astype(F32)
    col = jnp.broadcast_to(jnp.arange(GRID_W)[None, :], (rows, GRID_W)).reshape(-1).astype(F32)
    per_axis = head_dim // 4
    freqs = ROPE_THETA ** (-jnp.arange(per_axis, dtype=F32) / per_axis)
    ang = jnp.concatenate([row[:, None] * freqs, col[:, None] * freqs], axis=-1)
    return jnp.cos(ang), jnp.sin(ang)


def kernel(x_prompt, x_sample, c, state_hgrn, state_ret, cache_k, cache_v, c_ctx, w_mod, b_mod, norm_mix, norm_ffn, w_in_even, w_out_even, hgrn_lb_logits, hgrn_onorm, ret_decay_logit, ret_gnorm, w_in_attn, w_out_attn, q_norm, k_norm, w_ffn_gate, w_ffn_up, w_ffn_down):
    n_p, l_p, d = x_prompt.shape
    n_s, l_s, _ = x_sample.shape
    depth = w_mod.shape[0]
    rows_p = n_p * l_p
    rows_s = n_s * l_s
    a_width = hgrn_lb_logits.shape[-1]
    b_width = ret_gnorm.shape[-1]
    kv_width = ATTN_KV_HEADS * ATTN_HEAD_DIM
    row_quantum = math.gcd(rows_p, l_s)

    def group_of_row(r):
        return jnp.where(r < rows_p, 0, 1 + (r - rows_p) // l_s)

    cond = jnp.concatenate([c_ctx[None, :], c, jnp.zeros((-(1 + n_s) % SUBLANES, d), F32)], axis=0)
    as_mods = lambda m: m.reshape(cond.shape[0], 6, 1, d)
    mods = as_mods(_adaln_call(cond, w_mod, b_mod, 0))

    cos_a, sin_a = _grid_rope(l_s, ATTN_HEAD_DIM)
    rope_attn = (jnp.concatenate([cos_a, cos_a], axis=-1), jnp.concatenate([-sin_a, sin_a], axis=-1))
    rope_rows = (jnp.concatenate([jnp.ones((rows_p, ATTN_HEAD_DIM), F32), jnp.tile(rope_attn[0], (n_s, 1))]),
                 jnp.concatenate([jnp.zeros((rows_p, ATTN_HEAD_DIM), F32), jnp.tile(rope_attn[1], (n_s, 1))]))
    rope_ret = _grid_rope(l_s, b_width // B_HEADS)
    lb_cum = jnp.cumsum(jax.nn.softmax(hgrn_lb_logits.astype(F32), axis=0), axis=0)
    lower_bounds = lb_cum - lb_cum[0]

    n_even = w_in_even.shape[0]
    out_hgrn, out_ret = n_even, n_even
    new_k, new_v = [], []
    for l in range(depth):
        if l == 0:
            mix, x = _modulate_join_call(x_prompt.reshape(rows_p, d), x_sample.reshape(rows_s, d),
                                         norm_mix, mods, l, 0, group_of_row)
        else:
            mix = _modulate_call(x, norm_mix, mods, l, 0, group_of_row)
        if l % 2 == 0:
            e = l // 2
            p = _mm_plain_call(mix, w_in_even, e, F32)
            mix, out_hgrn = _gla_call(p, lower_bounds, l, hgrn_onorm, e, None, n_p, l_p, 0, out_hgrn, mix)
            (mix,) = _gla_call(p, lower_bounds, l, hgrn_onorm, e, state_hgrn, n_s, l_s, rows_p, None, mix)
            mix, out_ret = _ret_call(p, 5 * a_width, ret_decay_logit, ret_gnorm, e, None, None,
                                     n_p, l_p, 0, out_ret, mix, a_width)
            (mix,) = _ret_call(p, 5 * a_width, ret_decay_logit, ret_gnorm, e, rope_ret, state_ret,
                               n_s, l_s, rows_p, None, mix, a_width)
            w_out, widx = w_out_even, e
        else:
            o = l // 2
            q = _mm_qproj_call(mix, w_in_attn, o, d, q_norm, rope_rows)
            k, v, kf, vf = _mm_kvproj_call(mix, w_in_attn, o, d, k_norm, rope_rows)
            past = cache_k.shape[2]
            with_cache = lambda new, cache: jnp.concatenate(
                [new[rows_p:].reshape(n_s, l_s, kv_width),
                 cache[:, o].reshape(n_s, past, kv_width).astype(BF16)], axis=1)
            k_all, v_all = with_cache(k, cache_k), with_cache(v, cache_v)
            mix = _attn_call(q, k, v, n_p, l_p, 0, mix)
            mix = _attn_call(q, k_all, v_all, n_s, l_s, rows_p, mix)
            w_out, widx = w_out_attn, o
            new_k.append(kf[:rows_p].reshape(n_p, l_p, ATTN_KV_HEADS, ATTN_HEAD_DIM))
            new_v.append(vf[:rows_p].reshape(n_p, l_p, ATTN_KV_HEADS, ATTN_HEAD_DIM))
        x, h = _mm_residual_modulate_call(mix, w_out, widx, x, mods, l, 2, norm_ffn, 3, group_of_row,
                                          row_quantum)
        if l < depth - 1:
            u, next_mods = _mm_swiglu_call(h, w_ffn_gate, w_ffn_up, l, (cond, w_mod, b_mod, l + 1))
        else:
            u = _mm_swiglu_call(h, w_ffn_gate, w_ffn_up, l)
        down = functools.partial(_mm_residual_call, u, w_ffn_down, l, x, mods, 5, group_of_row,
                                 row_quantum)
        if l < depth - 1:
            x = down()
            mods = as_mods(next_mods)
    return (down(0, rows_p).reshape(n_p, l_p, d), down(rows_p, rows_s).reshape(n_s, l_s, d),
            out_hgrn, out_ret, jnp.stack(new_k, axis=1), jnp.stack(new_v, axis=1))
```

```python
import functools
import math

import jax
import jax.numpy as jnp
from jax import lax
from jax.experimental import pallas as pl
from jax.experimental.pallas import tpu as pltpu

F32 = jnp.float32
BF16 = jnp.bfloat16

EPS = 1e-6
ROPE_THETA = 10000.0
GRID_W = 64
A_HEAD_DIM = 128
B_HEADS = 4
ATTN_HEAD_DIM = 128
ATTN_KV_HEADS = 4
LANES = 128
SUBLANES = 8
VMEM_LIMIT = 56 * 1024 * 1024
VMEM_TILE_BUDGET = 50 * 1024 * 1024

LOG2_E = math.log2(math.e)

NT_DIMS = (((1,), (1,)), ((), ()))
TN_DIMS = (((0,), (0,)), ((), ()))


def _params(*sem):
    return pltpu.CompilerParams(dimension_semantics=sem, vmem_limit_bytes=VMEM_LIMIT)


def _silu(x):
    return x / (1.0 + jnp.exp(-x))


def _log_sigmoid(x):
    return jnp.minimum(x, 0.0) - jnp.log1p(jnp.exp(-jnp.abs(x)))


def _adaln_tile(c_ref, w_ref, b_ref):
    s = _silu(c_ref[...]).astype(BF16)
    return jnp.dot(s, w_ref[...].astype(BF16), preferred_element_type=F32) + b_ref[...]


def _adaln_kernel(c_ref, w_ref, b_ref, o_ref):
    o_ref[...] = _adaln_tile(c_ref, w_ref, b_ref)


def _adaln_call(cond, w_mod, b_mod, layer, tn=1024):
    depth, d, n = w_mod.shape
    rows = cond.shape[0]
    return pl.pallas_call(
        _adaln_kernel,
        grid=(n // tn,),
        in_specs=[
            pl.BlockSpec((rows, d), lambda j: (0, 0)),
            pl.BlockSpec((None, d, tn), lambda j: (layer, 0, j)),
            pl.BlockSpec((None, 1, tn), lambda j: (layer, 0, j)),
        ],
        out_specs=pl.BlockSpec((rows, tn), lambda j: (0, j)),
        out_shape=jax.ShapeDtypeStruct((rows, n), F32),
        compiler_params=_params("arbitrary"),
    )(cond, w_mod, b_mod.reshape(depth, 1, n))


def _modulate_kernel(x_ref, g_ref, shift_ref, scale_ref, o_ref):
    x = x_ref[...]
    y = x * lax.rsqrt(jnp.mean(x * x, axis=-1, keepdims=True) + EPS) * g_ref[...]
    o_ref[...] = (y * (1.0 + scale_ref[...]) + shift_ref[...]).astype(BF16)


def _modulate_join_kernel(xa_ref, xb_ref, g_ref, shift_ref, scale_ref, o_ref, ox_ref, *, a_blocks):
    def from_ref(x_ref):
        ox_ref[...] = x_ref[...]
        _modulate_kernel(x_ref, g_ref, shift_ref, scale_ref, o_ref)

    pl.when(pl.program_id(0) < a_blocks)(lambda: from_ref(xa_ref))
    pl.when(pl.program_id(0) >= a_blocks)(lambda: from_ref(xb_ref))


def _modulate_join_call(xa, xb, gains, mods, layer, shift_idx, group_of_row, tr=512):
    (ma, d), mb = xa.shape, xb.shape[0]
    depth = gains.shape[0]
    assert ma % tr == 0 and mb % tr == 0
    a_blocks = ma // tr
    grp = lambda i: group_of_row(i * tr)
    row_spec = pl.BlockSpec((tr, d), lambda i: (i, 0))
    return pl.pallas_call(
        functools.partial(_modulate_join_kernel, a_blocks=a_blocks),
        grid=((ma + mb) // tr,),
        in_specs=[
            pl.BlockSpec((tr, d), lambda i: (jnp.minimum(i, a_blocks - 1), 0)),
            pl.BlockSpec((tr, d), lambda i: (jnp.maximum(i - a_blocks, 0), 0)),
            pl.BlockSpec((None, 1, d), lambda i: (layer, 0, 0)),
            pl.BlockSpec((None, None, 1, d), lambda i: (grp(i), shift_idx, 0, 0)),
            pl.BlockSpec((None, None, 1, d), lambda i: (grp(i), shift_idx + 1, 0, 0)),
        ],
        out_specs=[row_spec, row_spec],
        out_shape=[jax.ShapeDtypeStruct((ma + mb, d), BF16), jax.ShapeDtypeStruct((ma + mb, d), F32)],
        compiler_params=_params("arbitrary"),
    )(xa, xb, gains.reshape(depth, 1, d), mods, mods)


def _modulate_call(x, gains, mods, layer, shift_idx, group_of_row, tr=512):
    m, d = x.shape
    depth = gains.shape[0]
    grp = lambda i: group_of_row(i * tr)
    return pl.pallas_call(
        _modulate_kernel,
        grid=(m // tr,),
        in_specs=[
            pl.BlockSpec((tr, d), lambda i: (i, 0)),
            pl.BlockSpec((None, 1, d), lambda i: (layer, 0, 0)),
            pl.BlockSpec((None, None, 1, d), lambda i: (grp(i), shift_idx, 0, 0)),
            pl.BlockSpec((None, None, 1, d), lambda i: (grp(i), shift_idx + 1, 0, 0)),
        ],
        out_specs=pl.BlockSpec((tr, d), lambda i: (i, 0)),
        out_shape=jax.ShapeDtypeStruct((m, d), BF16),
        compiler_params=_params("arbitrary"),
    )(x, gains.reshape(depth, 1, d), mods, mods)


def _matmul_tiles(m, k, n, n_weights, out_bytes, residual, row_quantum):
    for tn in (2048, 1024, 512, 256, 128):
        if n % tn:
            continue
        for tm in (1024, 512, 256):
            if m % tm or row_quantum % tm:
                continue
            need = (2 * tm * k * 2 + n_weights * (2 * k * tn * 4 + k * tn * 2 + tm * tn * 4)
                    + 2 * tm * tn * out_bytes + (2 * tm * tn * 4 if residual else 0))
            if need <= VMEM_TILE_BUDGET:
                return tm, tn
    raise ValueError(f"no matmul tiling fits VMEM for {(m, k, n)}")


def _cast_weights_once(w_refs, wbf_refs):
    @pl.when(pl.program_id(1) == 0)
    def _():
        for w_ref, wbf_ref in zip(w_refs, wbf_refs):
            wbf_ref[...] = w_ref[...].astype(BF16)


def _mm_plain_kernel(a_ref, w_ref, o_ref, wbf_ref):
    _cast_weights_once((w_ref,), (wbf_ref,))
    o_ref[...] = jnp.dot(a_ref[...], wbf_ref[...], preferred_element_type=F32).astype(o_ref.dtype)


def _mm_swiglu_kernel(*refs, adaln_steps):
    if adaln_steps:
        a_ref, wg_ref, wu_ref, c_ref, wm_ref, bm_ref, o_ref, om_ref, wgbf_ref, wubf_ref = refs
    else:
        a_ref, wg_ref, wu_ref, o_ref, wgbf_ref, wubf_ref = refs
    _cast_weights_once((wg_ref, wu_ref), (wgbf_ref, wubf_ref))
    a = a_ref[...]
    g = jnp.dot(a, wgbf_ref[...], preferred_element_type=F32)
    u = jnp.dot(a, wubf_ref[...], preferred_element_type=F32)
    o_ref[...] = (_silu(g) * u).astype(o_ref.dtype)
    if adaln_steps:
        @pl.when(pl.program_id(0) * pl.num_programs(1) + pl.program_id(1) < adaln_steps)
        def _():
            om_ref[...] = _adaln_tile(c_ref, wm_ref, bm_ref)


def _mm_residual_kernel(a_ref, w_ref, x_ref, gate_ref, o_ref, wbf_ref):
    _cast_weights_once((w_ref,), (wbf_ref,))
    y = jnp.dot(a_ref[...], wbf_ref[...], preferred_element_type=F32)
    o_ref[...] = x_ref[...] + gate_ref[...] * y


def _mm_residual_modulate_kernel(a_ref, w_ref, x_ref, gate_ref, g_ref, shift_ref, scale_ref,
                                 ox_ref, oh_ref, wbf_ref):
    @pl.when(pl.program_id(0) == 0)
    def _():
        wbf_ref[...] = w_ref[...].astype(BF16)
    y = jnp.dot(a_ref[...], wbf_ref[...], preferred_element_type=F32)
    x = x_ref[...] + gate_ref[...] * y
    ox_ref[...] = x
    h = x * lax.rsqrt(jnp.mean(x * x, axis=-1, keepdims=True) + EPS) * g_ref[...]
    oh_ref[...] = (h * (1.0 + scale_ref[...]) + shift_ref[...]).astype(BF16)


def _mm_residual_modulate_call(a, w, widx, x, mods, layer, gate_idx, gains, shift_idx, group_of_row,
                               row_quantum):
    m, k = a.shape
    n = w.shape[-1]
    depth = gains.shape[0]
    for tm in (512, 256, 128):
        need = k * n * (4 + 2) + 2 * tm * (k * 2 + n * (4 + 4 + 2)) + 2 * tm * n * 4
        if m % tm == 0 and row_quantum % tm == 0 and need <= VMEM_TILE_BUDGET:
            break
    else:
        raise ValueError(f"no row tile fits VMEM for {(m, k, n)}")
    grp = lambda i: group_of_row(i * tm)
    mod_spec = lambda which: pl.BlockSpec((None, None, 1, n), lambda i: (grp(i), which, 0, 0))
    row_spec = pl.BlockSpec((tm, n), lambda i: (i, 0))
    return pl.pallas_call(
        _mm_residual_modulate_kernel,
        grid=(m // tm,),
        in_specs=[
            pl.BlockSpec((tm, k), lambda i: (i, 0)),
            pl.BlockSpec((None, k, n), lambda i: (widx, 0, 0), pipeline_mode=pl.Buffered(1)),
            row_spec,
            mod_spec(gate_idx),
            pl.BlockSpec((None, 1, n), lambda i: (layer, 0, 0)),
            mod_spec(shift_idx),
            mod_spec(shift_idx + 1),
        ],
        out_specs=[row_spec, row_spec],
        out_shape=[jax.ShapeDtypeStruct((m, n), F32), jax.ShapeDtypeStruct((m, n), BF16)],
        scratch_shapes=[pltpu.VMEM((k, n), BF16)],
        compiler_params=_params("arbitrary"),
    )(a, w, x, mods, gains.reshape(depth, 1, n), mods, mods)


def _mm_plain_call(a, w, widx, out_dtype):
    m, k = a.shape
    n = w.shape[-1]
    tm, tn = _matmul_tiles(m, k, n, 1, jnp.dtype(out_dtype).itemsize, False, m)
    return pl.pallas_call(
        _mm_plain_kernel,
        grid=(n // tn, m // tm),
        in_specs=[
            pl.BlockSpec((tm, k), lambda j, i: (i, 0)),
            pl.BlockSpec((None, k, tn), lambda j, i: (widx, 0, j)),
        ],
        out_specs=pl.BlockSpec((tm, tn), lambda j, i: (i, j)),
        out_shape=jax.ShapeDtypeStruct((m, n), out_dtype),
        scratch_shapes=[pltpu.VMEM((k, tn), BF16)],
        compiler_params=_params("arbitrary", "arbitrary"),
    )(a, w)


ADALN_TILE = 256


def _mm_swiglu_call(a, wg, wu, widx, adaln=None):
    m, k = a.shape
    n = wg.shape[-1]
    tm, tn = _matmul_tiles(m, k, n, 2, 2, False, m)
    wspec = pl.BlockSpec((None, k, tn), lambda j, i: (widx, 0, j))
    in_specs = [pl.BlockSpec((tm, k), lambda j, i: (i, 0)), wspec, wspec]
    args = [a, wg, wu]
    out_specs = [pl.BlockSpec((tm, tn), lambda j, i: (i, j))]
    out_shape = [jax.ShapeDtypeStruct((m, n), BF16)]
    adaln_steps = 0
    if adaln is not None:
        cond, w_mod, b_mod, layer = adaln
        depth, d, n_mod = w_mod.shape
        rows = cond.shape[0]
        adaln_steps = n_mod // ADALN_TILE
        per_j = m // tm
        assert n_mod % ADALN_TILE == 0 and adaln_steps <= (n // tn) * per_j
        tile = lambda j, i: jnp.minimum(j * per_j + i, adaln_steps - 1)
        in_specs += [pl.BlockSpec((rows, d), lambda j, i: (0, 0)),
                     pl.BlockSpec((None, d, ADALN_TILE), lambda j, i: (layer, 0, tile(j, i))),
                     pl.BlockSpec((None, 1, ADALN_TILE), lambda j, i: (layer, 0, tile(j, i)))]
        args += [cond, w_mod, b_mod.reshape(depth, 1, n_mod)]
        out_specs.append(pl.BlockSpec((rows, ADALN_TILE), lambda j, i: (0, tile(j, i))))
        out_shape.append(jax.ShapeDtypeStruct((rows, n_mod), F32))
    outs = pl.pallas_call(
        functools.partial(_mm_swiglu_kernel, adaln_steps=adaln_steps),
        grid=(n // tn, m // tm),
        in_specs=in_specs,
        out_specs=out_specs,
        out_shape=out_shape,
        scratch_shapes=[pltpu.VMEM((k, tn), BF16), pltpu.VMEM((k, tn), BF16)],
        compiler_params=_params("arbitrary", "arbitrary"),
    )(*args)
    return outs if adaln is not None else outs[0]


def _mm_residual_call(a, w, widx, x, mods, gate_idx, group_of_row, row_quantum, row0=0, m=None):
    k = a.shape[1]
    m = a.shape[0] if m is None else m
    n = w.shape[-1]
    tm, tn = _matmul_tiles(m, k, n, 1, 4, True, row_quantum)
    assert row0 % tm == 0
    rb0 = row0 // tm
    grp = lambda i: group_of_row((i + rb0) * tm)
    return pl.pallas_call(
        _mm_residual_kernel,
        grid=(n // tn, m // tm),
        in_specs=[
            pl.BlockSpec((tm, k), lambda j, i: (i + rb0, 0)),
            pl.BlockSpec((None, k, tn), lambda j, i: (widx, 0, j)),
            pl.BlockSpec((tm, tn), lambda j, i: (i + rb0, j)),
            pl.BlockSpec((None, None, 1, tn), lambda j, i: (grp(i), gate_idx, 0, j)),
        ],
        out_specs=pl.BlockSpec((tm, tn), lambda j, i: (i, j)),
        out_shape=jax.ShapeDtypeStruct((m, n), F32),
        scratch_shapes=[pltpu.VMEM((k, tn), BF16)],
        compiler_params=_params("arbitrary", "arbitrary"),
    )(a, w, x, mods)


def _bcast_row_in_blocks(x, block, r):
    c, d = x.shape
    x3 = x.reshape(c // block, block, d)
    return jnp.broadcast_to(x3[:, r:r + 1, :], x3.shape).reshape(c, d)


MXU_WIDTH = 256


def _gla_blocks(c):
    return [1 << b for b in range((c // 2).bit_length())] + [0]


def _gla_masks(c, fwd):
    i = lax.broadcasted_iota(jnp.int32, (c, LANES), 0)
    lane = lax.broadcasted_iota(jnp.int32, (c, LANES), 1)
    masks = []
    for b, h in enumerate(_gla_blocks(c)):
        j = lane - (b * c) % LANES
        in_block = jnp.where(j >= 0, jnp.where(j < c, 1.0, 0.0), 0.0)
        if h == 0:
            ok = jnp.where(i == j, 1.0, 0.0)
        else:
            same = (i // (2 * h)) == (j // (2 * h))
            i_hi = (i % (2 * h)) >= h
            j_hi = (j % (2 * h)) >= h
            if fwd:
                ok = jnp.where(same, jnp.where(i_hi, jnp.where(j_hi, 0.0, 1.0), 0.0), 0.0)
            else:
                ok = jnp.where(same, jnp.where(i_hi, 0.0, jnp.where(j_hi, 1.0, 0.0)), 0.0)
        masks.append((ok * in_block).astype(F32))
    return masks


def _split_roles(q, kg, h, fwd, rloc):
    c = q.shape[0]
    if h >= SUBLANES:
        parts = []
        for b in range(c // (2 * h)):
            lo = slice(2 * h * b, 2 * h * b + h)
            hi = slice(2 * h * b + h, 2 * h * (b + 1))
            parts += [kg[lo], q[hi]] if fwd else [q[lo], kg[hi]]
        return jnp.concatenate(parts, axis=0)
    in_hi = (rloc % (2 * h)) >= h
    return jnp.where(in_hi, q, kg) if fwd else jnp.where(in_hi, kg, q)


def _split3(x):
    hi = x.astype(BF16)
    r1 = x - hi.astype(F32)
    mid = r1.astype(BF16)
    lo = (r1 - mid.astype(F32)).astype(BF16)
    return jnp.concatenate([hi, mid, lo], axis=-1)


def _gla_chunks(chains, rloc):
    c, d = chains[0][0].shape
    per_group = MXU_WIDTH // c
    blocks = _gla_blocks(c)
    n_groups = -(-len(blocks) // per_group)
    n_kvregs = -(-len(blocks) * c // LANES)

    es = [None] * len(chains)
    for direction in (True, False):
        ids = [n for n, ch in enumerate(chains) if ch[6] == direction]
        if ids:
            e3 = jnp.dot(chains[ids[0]][7], jnp.concatenate([chains[n][4] for n in ids], axis=1),
                         preferred_element_type=F32)
            for pos, n in enumerate(ids):
                base = 3 * d * pos
                es[n] = e3[:, base:base + d] + e3[:, base + d:base + 2 * d] + e3[:, base + 2 * d:base + 3 * d]

    sides = []
    for (q, kg, fg, v, logf, st, fwd, tri_bf, masks), e in zip(chains, es):
        qs, ks = [], []
        for h in blocks:
            if h == 0:
                qs.append(q.astype(BF16))
                ks.append(kg.astype(BF16))
                continue
            if h == 1:
                in_hi = (rloc % 2) >= 1
                w = jnp.where(in_hi, q * fg, kg) if fwd else jnp.where(in_hi, kg, q * fg)
            else:
                r = h - 1 if fwd else h
                if 2 * h >= SUBLANES:
                    e_ref = _bcast_row_in_blocks(e, 2 * h, r)
                else:
                    e_ref = jnp.where(rloc < 2 * h, _bcast_row_in_blocks(e, SUBLANES, r),
                                      _bcast_row_in_blocks(e, SUBLANES, 2 * h + r))
                w = _split_roles(q, kg, h, fwd, rloc) * (1.0 / jnp.exp2(jnp.abs(e - e_ref)))
            qs.append(w.astype(BF16))
            ks.append(qs[-1])
        pad = [jnp.zeros((c, d), BF16)] * (n_groups * per_group - len(blocks))
        sides.append((qs + pad, ks + pad))

    atts = [[None] * n_kvregs for _ in chains]
    for g in range(n_groups):
        for ci, (qs, ks) in enumerate(sides):
            sl = slice(g * per_group, (g + 1) * per_group)
            prod = lax.dot_general(jnp.concatenate(qs[sl], axis=0), jnp.concatenate(ks[sl], axis=0),
                                   NT_DIMS, preferred_element_type=F32)
            masks = chains[ci][8]
            for b in range(g * per_group, min((g + 1) * per_group, len(blocks))):
                t = b * c // LANES
                col = (t * LANES) % MXU_WIDTH
                row = (b - g * per_group) * c
                term = prod[row:row + c, col:col + LANES] * masks[b]
                atts[ci][t] = term if atts[ci][t] is None else atts[ci][t] + term

    outs = []
    for (q, kg, fg, v, logf, st, fwd, tri_bf, masks), e, att in zip(chains, es, atts):
        vb = v.astype(BF16)
        att_sum = functools.reduce(lambda x, y: x + y, att)
        v_rep = jnp.concatenate([vb] * (LANES // c), axis=0)
        o = jnp.dot(att_sum.astype(BF16), v_rep, preferred_element_type=F32)
        o = o + lax.dot_general((q * jnp.exp2(e)).astype(BF16), st.astype(BF16), NT_DIMS,
                                preferred_element_type=F32)
        e_edge = e[c - 1:c] if fwd else e[0:1]
        kh = (kg * jnp.exp2(e_edge - e)).astype(BF16)
        st_new = st * jnp.exp2(e_edge) + lax.dot_general(vb, kh, TN_DIMS, preferred_element_type=F32)
        outs.append((o, st_new))
    return outs


def _gla_kernel(*refs, seq_len, chunk, has_s0, final):
    refs = list(refs)
    q_ref, ffw_ref, fbw_ref, v_ref, g_ref, lb_ref, on_ref = refs[:7]
    del refs[:7]
    s0_ref = refs.pop(0) if has_s0 else None
    del refs[:2 if final == "join" else 1]
    o_ref = refs.pop(0)
    sf_ref = _final_state_slot(refs.pop(0), final) if final is not None else None
    stf_ref, stb_ref, oacc_ref = refs
    c = chunk
    nch = seq_len // c
    half_n = nch // 2
    d = A_HEAD_DIM
    n_h = q_ref.shape[-1] // d

    i = lax.broadcasted_iota(jnp.int32, (c, c), 0)
    j = lax.broadcasted_iota(jnp.int32, (c, c), 1)
    tri_f = jnp.where(i >= j, 1.0, 0.0).astype(BF16)
    tri_b = jnp.where(i <= j, 1.0, 0.0).astype(BF16)
    masks_f = _gla_masks(c, True)
    masks_b = _gla_masks(c, False)
    rloc = lax.broadcasted_iota(jnp.int32, (c, d), 0) % SUBLANES
    q_scale = d ** -0.5

    def gates(x, lb):
        one_m_lb = 1.0 - lb
        t = jnp.exp(-jnp.abs(x))
        r = 1.0 / (1.0 + t)
        pos = x >= 0.0
        fg = lb + one_m_lb * jnp.where(pos, r, t * r)
        kg = one_m_lb * jnp.where(pos, t * r, r)
        log2f = jnp.maximum(jnp.log2(fg),
                            jnp.log2(one_m_lb) + jnp.minimum(x, 0.0) * LOG2_E - jnp.log2(1.0 + t))
        return fg, kg, log2f

    def chain_rows(t, fwd):
        return pl.ds(pl.multiple_of((t if fwd else nch - 1 - t) * c, c), c)

    chain_ids = [(hh, fwd) for hh in range(n_h) for fwd in (True, False)]

    def all_chains(t):
        where, chains = [], []
        for hh, fwd in chain_ids:
            sl, cols = chain_rows(t, fwd), slice(hh * d, (hh + 1) * d)
            fg, kg, log2f = gates((ffw_ref if fwd else fbw_ref)[sl, cols], lb_ref[:, cols])
            where.append((sl, cols))
            chains.append((_silu(q_ref[sl, cols]) * q_scale, kg, fg, v_ref[sl, cols], _split3(log2f),
                           (stf_ref if fwd else stb_ref)[hh], fwd,
                           tri_f if fwd else tri_b, masks_f if fwd else masks_b))
        outs = _gla_chunks(chains, rloc)
        for n, (o, st) in enumerate(outs):
            (stf_ref if n % 2 == 0 else stb_ref)[n // 2] = st
        return [(sl, cols, o) for (sl, cols), (o, st) in zip(where, outs)]

    def finish(sl, cols, o):
        y = oacc_ref[sl, cols] + o
        y = y * lax.rsqrt(jnp.mean(y * y, axis=-1, keepdims=True) + EPS) * on_ref[...]
        o_ref[sl, cols] = (y * _silu(g_ref[sl, cols])).astype(o_ref.dtype)

    def first_half(t, carry):
        for sl, cols, o in all_chains(t):
            oacc_ref[sl, cols] = o
        return carry

    def second_half(t, carry):
        for sl, cols, o in all_chains(t):
            finish(sl, cols, o)
        return carry

    for hh in range(n_h):
        for st_ref, direction in ((stf_ref, 0), (stb_ref, 1)):
            st_ref[hh] = s0_ref[direction, hh].T if has_s0 else jnp.zeros((d, d), F32)
    unroll = 2 if 2 * n_h < 8 and half_n % 2 == 0 else 1
    lax.fori_loop(0, half_n, first_half, 0, unroll=unroll)
    lax.fori_loop(half_n, nch, second_half, 0, unroll=unroll)
    if final is not None:
        for hh in range(n_h):
            sf_ref[0, hh] = stf_ref[hh].T
            sf_ref[1, hh] = stb_ref[hh].T


GLA_CHUNK = 128


def _join_buffers(bufs, in_specs, args):
    shapes, aliases = [], {}
    for n, buf in enumerate(bufs):
        in_specs.append(pl.BlockSpec(memory_space=pl.ANY))
        args.append(buf)
        shapes.append(jax.ShapeDtypeStruct(buf.shape, buf.dtype))
        aliases[len(args) - 1] = n
    return shapes, aliases


def _final_state_specs(finals, slot, dims, heads_per_step, out_specs):
    n_seq, heads, d = dims
    if finals is None:
        return [], None
    if isinstance(finals, int):
        out_specs.append(pl.BlockSpec((None, finals, 2, heads_per_step, d, d),
                                      lambda n, h: (n, 0, 0, h, 0, 0)))
        return [], (slot, finals)
    out_specs.append(pl.BlockSpec((None, None, 2, heads_per_step, d, d),
                                  lambda n, h: (n, slot, 0, h, 0, 0)))
    return [finals], "join"


def _final_state_slot(sf_ref, final):
    if final == "join":
        return sf_ref
    slot, n_slots = final
    for s in range(n_slots):
        if s != slot:
            sf_ref[s] = jnp.zeros(sf_ref.shape[1:], sf_ref.dtype)
    return sf_ref.at[slot]


def _gla_heads_per_step(seq_len):
    for n_h in (4, 2, 1):
        if 2 * 5 * seq_len * n_h * A_HEAD_DIM * 4 + 4 * seq_len * n_h * A_HEAD_DIM * 2 <= VMEM_TILE_BUDGET:
            return n_h
    raise ValueError(f"sequence of {seq_len} rows does not fit VMEM")


def _gla_call(p, lower_bounds, layer, onorm, eidx, s0, n_seq, seq_len, row0, finals, mix):
    d = A_HEAD_DIM
    n_h = _gla_heads_per_step(seq_len)
    w = n_h * d
    a_width = lower_bounds.shape[-1]
    n_hb = a_width // w
    assert row0 % seq_len == 0 and seq_len % (2 * GLA_CHUNK) == 0 and a_width % w == 0
    sb = row0 // seq_len
    slab = lambda k: pl.BlockSpec((seq_len, w), lambda n, h: (n + sb, k * n_hb + h))
    in_specs = [slab(0), slab(1), slab(2), slab(3), slab(4),
                pl.BlockSpec((None, 1, w), lambda n, h: (layer, 0, h)),
                pl.BlockSpec((None, 1, d), lambda n, h: (eidx, 0, 0))]
    args = [p, p, p, p, p, lower_bounds.reshape(lower_bounds.shape[0], 1, a_width),
            onorm.reshape(onorm.shape[0], 1, d)]
    if s0 is not None:
        in_specs.append(pl.BlockSpec((None, None, 2, n_h, d, d), lambda n, h: (n, eidx, 0, h, 0, 0)))
        args.append(s0)
    out_specs = [pl.BlockSpec((seq_len, w), lambda n, h: (n + sb, h))]
    joined, final = _final_state_specs(finals, eidx, (n_seq, a_width // d, d), n_h, out_specs)
    out_shape, aliases = _join_buffers([mix] + joined, in_specs, args)
    if isinstance(final, tuple):
        out_shape.append(jax.ShapeDtypeStruct((n_seq, final[1], 2, a_width // d, d, d), F32))
    kern = functools.partial(_gla_kernel, seq_len=seq_len, chunk=GLA_CHUNK,
                             has_s0=s0 is not None, final=final)
    return pl.pallas_call(
        kern,
        grid=(n_seq, n_hb),
        in_specs=in_specs,
        out_specs=out_specs,
        out_shape=out_shape,
        input_output_aliases=aliases,
        scratch_shapes=[pltpu.VMEM((n_h, d, d), F32), pltpu.VMEM((n_h, d, d), F32),
                        pltpu.VMEM((seq_len, w), F32)],
        compiler_params=_params("arbitrary", "arbitrary"),
    )(*args)


def _ret_kernel(*refs, seq_len, chunk, has_s0, final, rope):
    refs = list(refs)
    q_ref, k_ref, v_ref, g_ref, dl_ref, gn_ref = refs[:6]
    del refs[:6]
    cos_ref = refs.pop(0) if rope else None
    sin_ref = refs.pop(0) if rope else None
    s0_ref = refs.pop(0) if has_s0 else None
    del refs[:2 if final == "join" else 1]
    o_ref = refs.pop(0)
    sf_ref = _final_state_slot(refs.pop(0), final) if final is not None else None
    stf_ref, stb_ref, oacc_ref = refs
    c = chunk
    nch = seq_len // c
    half_n = nch // 2
    d = dl_ref.shape[-1]
    n_h = q_ref.shape[-1] // d
    half = d // 2
    k_scale = d ** -0.5

    dist = (lax.broadcasted_iota(jnp.int32, (c, c), 0)
            - lax.broadcasted_iota(jnp.int32, (c, c), 1)).astype(F32)
    low = dist >= 0.0
    up = dist <= 0.0
    row = lax.broadcasted_iota(jnp.int32, (c, d), 0).astype(F32)

    def decay_tables(hh):
        lg_f = _log_sigmoid(dl_ref[0, hh])
        lg_b = _log_sigmoid(dl_ref[1, hh])
        intra = (jnp.where(low, jnp.exp(jnp.where(low, dist, 0.0) * lg_f[:, :c]), 0.0)
                 + jnp.where(up, jnp.exp(jnp.where(up, -dist, 0.0) * lg_b[:, :c]), 0.0))
        fwd = (jnp.exp((row + 1.0) * lg_f), jnp.exp((c - 1.0 - row) * lg_f), jnp.exp(c * lg_f))
        bwd = (jnp.exp((c - row) * lg_b), jnp.exp(row * lg_b), jnp.exp(c * lg_b))
        return intra, fwd, bwd

    tables = [decay_tables(hh) for hh in range(n_h)]

    def rotate(x, sl):
        if not rope:
            return x
        cos, sin = cos_ref[sl, :], sin_ref[sl, :]
        x1, x2 = x[:, :half], x[:, half:]
        return jnp.concatenate([x1 * cos - x2 * sin, x1 * sin + x2 * cos], axis=-1)

    def load(ci, hh):
        sl, cols = pl.ds(pl.multiple_of(ci * c, c), c), slice(hh * d, (hh + 1) * d)
        return (sl, cols, rotate(q_ref[sl, cols], sl), rotate(k_ref[sl, cols] * k_scale, sl),
                v_ref[sl, cols].astype(BF16))

    def carried(q, k, vb, st_ref, hh, decs):
        q_dec, k_dec, c_dec = decs
        st = st_ref[hh]
        o = jnp.dot((q * q_dec).astype(BF16), st.astype(BF16), preferred_element_type=F32)
        st_ref[hh] = c_dec * st + lax.dot_general((k * k_dec).astype(BF16), vb, TN_DIMS,
                                                  preferred_element_type=F32)
        return o

    def trip(t):
        left = [load(t, hh) for hh in range(n_h)]
        right = [load(nch - 1 - t, hh) for hh in range(n_h)]
        scores = [lax.dot_general(q.astype(BF16), k.astype(BF16), NT_DIMS, preferred_element_type=F32)
                  for _, _, q, k, _ in left]
        o_left = [carried(q, k, vb, stf_ref, hh, tables[hh][1]) for hh, (_, _, q, k, vb) in enumerate(left)]
        o_right = [carried(q, k, vb, stb_ref, hh, tables[hh][2]) for hh, (_, _, q, k, vb) in enumerate(right)]
        for hh, (_, _, _, _, vb) in enumerate(left):
            o_left[hh] = o_left[hh] + jnp.dot((scores[hh] * tables[hh][0]).astype(BF16), vb,
                                              preferred_element_type=F32)
        return ([(sl, cols, o) for (sl, cols, *_), o in zip(left, o_left)]
                + [(sl, cols, o) for (sl, cols, *_), o in zip(right, o_right)])

    def finish(sl, cols, o):
        y = oacc_ref[sl, cols] + o
        yc = y - jnp.mean(y, axis=-1, keepdims=True)
        yn = yc * lax.rsqrt(jnp.mean(yc * yc, axis=-1, keepdims=True) + EPS)
        o_ref[sl, cols] = (yn * gn_ref[:, cols] * _silu(g_ref[sl, cols])).astype(o_ref.dtype)

    def first_half(t, carry):
        for sl, cols, o in trip(t):
            oacc_ref[sl, cols] = o
        return carry

    def second_half(t, carry):
        for sl, cols, o in trip(t):
            finish(sl, cols, o)
        return carry

    for hh in range(n_h):
        for st_ref, direction in ((stf_ref, 0), (stb_ref, 1)):
            st_ref[hh] = s0_ref[direction, hh] if has_s0 else jnp.zeros((d, d), F32)
    lax.fori_loop(0, half_n, first_half, 0)
    lax.fori_loop(half_n, nch, second_half, 0)
    if final is not None:
        for hh in range(n_h):
            sf_ref[0, hh] = stf_ref[hh]
            sf_ref[1, hh] = stb_ref[hh]


def _ret_call(p, col0, decay_logit, gnorm, eidx, rope_tabs, s0, n_seq, seq_len, row0, finals,
              mix, mix_col0, chunk=128):
    n_heads = decay_logit.shape[-1]
    b_width = gnorm.shape[-1]
    d = b_width // n_heads
    n_h = 2 if n_heads % 2 == 0 else 1
    w = n_h * d
    n_hb = n_heads // n_h
    assert row0 % seq_len == 0 and seq_len % (2 * chunk) == 0 and col0 % w == 0 and chunk <= d
    assert mix_col0 % w == 0
    sb = row0 // seq_len
    cb0 = col0 // w
    ocb0 = mix_col0 // w
    slab = lambda k: pl.BlockSpec((seq_len, w), lambda n, h: (n + sb, cb0 + k * n_hb + h))
    dl = jnp.broadcast_to(decay_logit[:, :, :, None, None], decay_logit.shape + (1, d))
    in_specs = [slab(0), slab(1), slab(2), slab(3),
                pl.BlockSpec((None, 2, n_h, 1, d), lambda n, h: (eidx, 0, h, 0, 0)),
                pl.BlockSpec((None, 1, w), lambda n, h: (eidx, 0, h))]
    args = [p, p, p, p, dl, gnorm.reshape(gnorm.shape[0], 1, b_width)]
    if rope_tabs is not None:
        tab = pl.BlockSpec((seq_len, d // 2), lambda n, h: (0, 0))
        in_specs += [tab, tab]
        args += list(rope_tabs)
    if s0 is not None:
        in_specs.append(pl.BlockSpec((None, None, 2, n_h, d, d), lambda n, h: (n, eidx, 0, h, 0, 0)))
        args.append(s0)
    out_specs = [pl.BlockSpec((seq_len, w), lambda n, h: (n + sb, ocb0 + h))]
    joined, final = _final_state_specs(finals, eidx, (n_seq, n_heads, d), n_h, out_specs)
    out_shape, aliases = _join_buffers([mix] + joined, in_specs, args)
    if isinstance(final, tuple):
        out_shape.append(jax.ShapeDtypeStruct((n_seq, final[1], 2, n_heads, d, d), F32))
    kern = functools.partial(_ret_kernel, seq_len=seq_len, chunk=chunk, has_s0=s0 is not None,
                             final=final, rope=rope_tabs is not None)
    return pl.pallas_call(
        kern,
        grid=(n_seq, n_hb),
        in_specs=in_specs,
        out_specs=out_specs,
        out_shape=out_shape,
        input_output_aliases=aliases,
        scratch_shapes=[pltpu.VMEM((n_h, d, d), F32), pltpu.VMEM((n_h, d, d), F32),
                        pltpu.VMEM((seq_len, w), F32)],
        compiler_params=_params("arbitrary", "arbitrary"),
    )(*args)


Q_PRESCALE = ATTN_HEAD_DIM ** -0.5 * LOG2_E


def _head_norm(x, gain):
    return x * lax.rsqrt(jnp.mean(x * x, axis=-1, keepdims=True) + EPS) * gain


def _rotate(y, cos, sin):
    return y * cos + pltpu.roll(y, y.shape[-1] // 2, 1) * sin


def _mm_qproj_kernel(a_ref, w_ref, gain_ref, cos_ref, sin_ref, o_ref, wbf_ref):
    _cast_weights_once((w_ref,), (wbf_ref,))
    y = jnp.dot(a_ref[...], wbf_ref[...], preferred_element_type=F32)
    hd = ATTN_HEAD_DIM
    for h in range(y.shape[-1] // hd):
        cols = slice(h * hd, (h + 1) * hd)
        z = _rotate(_head_norm(y[:, cols], gain_ref[...]), cos_ref[...], sin_ref[...])
        o_ref[:, cols] = (z * Q_PRESCALE).astype(BF16)


def _mm_qproj_call(a, w, widx, n_cols, q_norm, rope_rows):
    m, k = a.shape
    hd = ATTN_HEAD_DIM
    tm, tn = _matmul_tiles(m, k, n_cols, 1, 2, False, m)
    tab = pl.BlockSpec((tm, hd), lambda j, i: (i, 0))
    return pl.pallas_call(
        _mm_qproj_kernel,
        grid=(n_cols // tn, m // tm),
        in_specs=[
            pl.BlockSpec((tm, k), lambda j, i: (i, 0)),
            pl.BlockSpec((None, k, tn), lambda j, i: (widx, 0, j)),
            pl.BlockSpec((None, 1, hd), lambda j, i: (widx, 0, 0)),
            tab, tab,
        ],
        out_specs=pl.BlockSpec((tm, tn), lambda j, i: (i, j)),
        out_shape=jax.ShapeDtypeStruct((m, n_cols), BF16),
        scratch_shapes=[pltpu.VMEM((k, tn), BF16)],
        compiler_params=_params("arbitrary", "arbitrary"),
    )(a, w, q_norm.reshape(-1, 1, hd), *rope_rows)


def _mm_kvproj_kernel(a_ref, w_ref, gain_ref, cos_ref, sin_ref, k_out, v_out, kf_out, vf_out, wbf_ref):
    _cast_weights_once((w_ref,), (wbf_ref,))
    y = jnp.dot(a_ref[...], wbf_ref[...], preferred_element_type=F32)
    hd = ATTN_HEAD_DIM
    kvw = y.shape[-1] // 2
    for h in range(kvw // hd):
        cols = slice(h * hd, (h + 1) * hd)
        z = _rotate(_head_norm(y[:, cols], gain_ref[...]), cos_ref[...], sin_ref[...])
        k_out[:, cols] = z.astype(BF16)
        kf_out[:, cols] = z
    v = y[:, kvw:]
    v_out[...] = v.astype(BF16)
    vf_out[...] = v


def _mm_kvproj_call(a, w, widx, col0, k_norm, rope_rows):
    m, k = a.shape
    hd = ATTN_HEAD_DIM
    kvw = ATTN_KV_HEADS * hd
    tn = 2 * kvw
    assert col0 % tn == 0
    tm, _ = _matmul_tiles(m, k, tn, 1, 2 + 4, False, m)
    tab = pl.BlockSpec((tm, hd), lambda j, i: (i, 0))
    row_spec = pl.BlockSpec((tm, kvw), lambda j, i: (i, 0))
    return pl.pallas_call(
        _mm_kvproj_kernel,
        grid=(1, m // tm),
        in_specs=[pl.BlockSpec((tm, k), lambda j, i: (i, 0)),
                  pl.BlockSpec((None, k, tn), lambda j, i: (widx, 0, col0 // tn)),
                  pl.BlockSpec((None, 1, hd), lambda j, i: (widx, 0, 0)),
                  tab, tab],
        out_specs=[row_spec] * 4,
        out_shape=[jax.ShapeDtypeStruct((m, kvw), BF16)] * 2 + [jax.ShapeDtypeStruct((m, kvw), F32)] * 2,
        scratch_shapes=[pltpu.VMEM((k, tn), BF16)],
        compiler_params=_params("arbitrary", "arbitrary"),
    )(a, w, k_norm.reshape(-1, 1, hd), *rope_rows)


def _attn_kernel(*refs, group):
    q_ref, k_ref, v_ref, _, o_ref = refs
    hd = ATTN_HEAD_DIM
    for kh in range(k_ref.shape[-1] // hd):
        k = k_ref[:, kh * hd:(kh + 1) * hd]
        v = v_ref[:, kh * hd:(kh + 1) * hd]
        v_ext = jnp.concatenate([v, jnp.ones_like(v)], axis=1)
        for g in range(group):
            cols = slice((kh * group + g) * hd, (kh * group + g + 1) * hd)
            s = lax.dot_general(q_ref[:, cols], k, NT_DIMS, preferred_element_type=F32)
            p = jnp.exp2(s - jnp.max(s, axis=-1, keepdims=True))
            o = jnp.dot(p.astype(BF16), v_ext, preferred_element_type=F32)
            o_ref[:, cols] = (o[:, :hd] / o[:, hd:]).astype(o_ref.dtype)


ATTN_SCORE_BUDGET = 4 * 1024 * 1024


ATTN_Q_ROWS = 1024


def _attn_call(q, k, v, n_seq, seq_len, row0, mix):
    kvw = k.shape[-1]
    qw = q.shape[-1]
    hd = ATTN_HEAD_DIM
    n_kv = kvw // hd
    group = qw // hd // n_kv
    tq = min(ATTN_Q_ROWS, seq_len)
    assert row0 % tq == 0 and seq_len % tq == 0 and row0 % seq_len == 0
    per_seq = seq_len // tq
    rb0 = row0 // tq
    lk = seq_len if k.ndim == 2 else k.shape[1]
    kv_step = n_kv if tq * lk * 4 * n_kv <= ATTN_SCORE_BUDGET else 1
    q_spec = pl.BlockSpec((tq, kv_step * group * hd), lambda n, kh, i: (rb0 + n * per_seq + i, kh))
    if k.ndim == 2:
        kv_spec = pl.BlockSpec((seq_len, kv_step * hd), lambda n, kh, i: (row0 // seq_len + n, kh))
    else:
        kv_spec = pl.BlockSpec((None, lk, kv_step * hd), lambda n, kh, i: (n, 0, kh))
    in_specs = [q_spec, kv_spec, kv_spec]
    args = [q, k, v]
    (mix_shape,), aliases = _join_buffers([mix], in_specs, args)
    return pl.pallas_call(
        functools.partial(_attn_kernel, group=group),
        grid=(n_seq, n_kv // kv_step, per_seq),
        in_specs=in_specs,
        out_specs=q_spec,
        out_shape=mix_shape,
        input_output_aliases=aliases,
        compiler_params=_params("arbitrary", "arbitrary", "arbitrary"),
    )(*args)


def _grid_rope(n_tokens, head_dim):
    rows = n_tokens // GRID_W
    row = jnp.broadcast_to(jnp.arange(rows)[:, None], (rows, GRID_W)).reshape(-1).astype(F32)
    col = jnp.broadcast_to(jnp.arange(GRID_W)[None, :], (rows, GRID_W)).reshape(-1).astype(F32)
    per_axis = head_dim // 4
    freqs = ROPE_THETA ** (-jnp.arange(per_axis, dtype=F32) / per_axis)
    ang = jnp.concatenate([row[:, None] * freqs, col[:, None] * freqs], axis=-1)
    return jnp.cos(ang), jnp.sin(ang)


def kernel(x_prompt, x_sample, c, state_hgrn, state_ret, cache_k, cache_v, c_ctx, w_mod, b_mod, norm_mix, norm_ffn, w_in_even, w_out_even, hgrn_lb_logits, hgrn_onorm, ret_decay_logit, ret_gnorm, w_in_attn, w_out_attn, q_norm, k_norm, w_ffn_gate, w_ffn_up, w_ffn_down):
    n_p, l_p, d = x_prompt.shape
    n_s, l_s, _ = x_sample.shape
    depth = w_mod.shape[0]
    rows_p = n_p * l_p
    rows_s = n_s * l_s
    a_width = hgrn_lb_logits.shape[-1]
    b_width = ret_gnorm.shape[-1]
    kv_width = ATTN_KV_HEADS * ATTN_HEAD_DIM
    row_quantum = math.gcd(rows_p, l_s)

    def group_of_row(r):
        return jnp.where(r < rows_p, 0, 1 + (r - rows_p) // l_s)

    cond = jnp.concatenate([c_ctx[None, :], c, jnp.zeros((-(1 + n_s) % SUBLANES, d), F32)], axis=0)
    as_mods = lambda m: m.reshape(cond.shape[0], 6, 1, d)
    mods = as_mods(_adaln_call(cond, w_mod, b_mod, 0))

    cos_a, sin_a = _grid_rope(l_s, ATTN_HEAD_DIM)
    rope_attn = (jnp.concatenate([cos_a, cos_a], axis=-1), jnp.concatenate([-sin_a, sin_a], axis=-1))
    rope_rows = (jnp.concatenate([jnp.ones((rows_p, ATTN_HEAD_DIM), F32), jnp.tile(rope_attn[0], (n_s, 1))]),
                 jnp.concatenate([jnp.zeros((rows_p, ATTN_HEAD_DIM), F32), jnp.tile(rope_attn[1], (n_s, 1))]))
    rope_ret = _grid_rope(l_s, b_width // B_HEADS)
    lb_cum = jnp.cumsum(jax.nn.softmax(hgrn_lb_logits.astype(F32), axis=0), axis=0)
    lower_bounds = lb_cum - lb_cum[0]

    n_even = w_in_even.shape[0]
    out_hgrn, out_ret = n_even, n_even
    new_k, new_v = [], []
    for l in range(depth):
        if l == 0:
            mix, x = _modulate_join_call(x_prompt.reshape(rows_p, d), x_sample.reshape(rows_s, d),
                                         norm_mix, mods, l, 0, group_of_row)
        else:
            mix = _modulate_call(x, norm_mix, mods, l, 0, group_of_row)
        if l % 2 == 0:
            e = l // 2
            p = _mm_plain_call(mix, w_in_even, e, F32)
            mix, out_hgrn = _gla_call(p, lower_bounds, l, hgrn_onorm, e, None, n_p, l_p, 0, out_hgrn, mix)
            (mix,) = _gla_call(p, lower_bounds, l, hgrn_onorm, e, state_hgrn, n_s, l_s, rows_p, None, mix)
            mix, out_ret = _ret_call(p, 5 * a_width, ret_decay_logit, ret_gnorm, e, None, None,
                                     n_p, l_p, 0, out_ret, mix, a_width)
            (mix,) = _ret_call(p, 5 * a_width, ret_decay_logit, ret_gnorm, e, rope_ret, state_ret,
                               n_s, l_s, rows_p, None, mix, a_width)
            w_out, widx = w_out_even, e
        else:
            o = l // 2
            q = _mm_qproj_call(mix, w_in_attn, o, d, q_norm, rope_rows)
            k, v, kf, vf = _mm_kvproj_call(mix, w_in_attn, o, d, k_norm, rope_rows)
            past = cache_k.shape[2]
            with_cache = lambda new, cache: jnp.concatenate(
                [new[rows_p:].reshape(n_s, l_s, kv_width),
                 cache[:, o].reshape(n_s, past, kv_width).astype(BF16)], axis=1)
            k_all, v_all = with_cache(k, cache_k), with_cache(v, cache_v)
            mix = _attn_call(q, k, v, n_p, l_p, 0, mix)
            mix = _attn_call(q, k_all, v_all, n_s, l_s, rows_p, mix)
            w_out, widx = w_out_attn, o
            new_k.append(kf[:rows_p].reshape(n_p, l_p, ATTN_KV_HEADS, ATTN_HEAD_DIM))
            new_v.append(vf[:rows_p].reshape(n_p, l_p, ATTN_KV_HEADS, ATTN_HEAD_DIM))
        x, h = _mm_residual_modulate_call(mix, w_out, widx, x, mods, l, 2, norm_ffn, 3, group_of_row,
                                          row_quantum)
        if l < depth - 1:
            u, next_mods = _mm_swiglu_call(h, w_ffn_gate, w_ffn_up, l, (cond, w_mod, b_mod, l + 1))
        else:
            u = _mm_swiglu_call(h, w_ffn_gate, w_ffn_up, l)
        down = functools.partial(_mm_residual_call, u, w_ffn_down, l, x, mods, 5, group_of_row,
                                 row_quantum)
        if l < depth - 1:
            x = down()
            mods = as_mods(next_mods)
    return (down(0, rows_p).reshape(n_p, l_p, d), down(rows_p, rows_s).reshape(n_s, l_s, d),
            out_hgrn, out_ret, jnp.stack(new_k, axis=1), jnp.stack(new_v, axis=1))
```

```python
import functools
import math

import jax
import jax.numpy as jnp
from jax import lax
from jax.experimental import pallas as pl
from jax.experimental.pallas import tpu as pltpu

F32 = jnp.float32
BF16 = jnp.bfloat16

EPS = 1e-6
ROPE_THETA = 10000.0
GRID_W = 64
A_HEAD_DIM = 128
B_HEADS = 4
ATTN_HEAD_DIM = 128
ATTN_KV_HEADS = 4
LANES = 128
SUBLANES = 8
VMEM_LIMIT = 56 * 1024 * 1024
VMEM_TILE_BUDGET = 50 * 1024 * 1024

LOG2_E = math.log2(math.e)

NT_DIMS = (((1,), (1,)), ((), ()))
TN_DIMS = (((0,), (0,)), ((), ()))


def _params(*sem):
    return pltpu.CompilerParams(dimension_semantics=sem, vmem_limit_bytes=VMEM_LIMIT)


def _silu(x):
    return x / (1.0 + jnp.exp(-x))


def _log_sigmoid(x):
    return jnp.minimum(x, 0.0) - jnp.log1p(jnp.exp(-jnp.abs(x)))


def _adaln_tile(c_ref, w_ref, b_ref):
    s = _silu(c_ref[...]).astype(BF16)
    return jnp.dot(s, w_ref[...].astype(BF16), preferred_element_type=F32) + b_ref[...]


def _adaln_kernel(c_ref, w_ref, b_ref, o_ref):
    o_ref[...] = _adaln_tile(c_ref, w_ref, b_ref)


def _adaln_call(cond, w_mod, b_mod, layer, tn=1024):
    depth, d, n = w_mod.shape
    rows = cond.shape[0]
    return pl.pallas_call(
        _adaln_kernel,
        grid=(n // tn,),
        in_specs=[
            pl.BlockSpec((rows, d), lambda j: (0, 0)),
            pl.BlockSpec((None, d, tn), lambda j: (layer, 0, j)),
            pl.BlockSpec((None, 1, tn), lambda j: (layer, 0, j)),
        ],
        out_specs=pl.BlockSpec((rows, tn), lambda j: (0, j)),
        out_shape=jax.ShapeDtypeStruct((rows, n), F32),
        compiler_params=_params("arbitrary"),
    )(cond, w_mod, b_mod.reshape(depth, 1, n))


def _modulate_kernel(x_ref, g_ref, shift_ref, scale_ref, o_ref):
    x = x_ref[...]
    y = x * lax.rsqrt(jnp.mean(x * x, axis=-1, keepdims=True) + EPS) * g_ref[...]
    o_ref[...] = (y * (1.0 + scale_ref[...]) + shift_ref[...]).astype(BF16)


def _modulate_join_kernel(xa_ref, xb_ref, g_ref, shift_ref, scale_ref, o_ref, ox_ref, *, a_blocks):
    def from_ref(x_ref):
        ox_ref[...] = x_ref[...]
        _modulate_kernel(x_ref, g_ref, shift_ref, scale_ref, o_ref)

    pl.when(pl.program_id(0) < a_blocks)(lambda: from_ref(xa_ref))
    pl.when(pl.program_id(0) >= a_blocks)(lambda: from_ref(xb_ref))


def _modulate_join_call(xa, xb, gains, mods, layer, shift_idx, group_of_row, tr=512):
    (ma, d), mb = xa.shape, xb.shape[0]
    depth = gains.shape[0]
    assert ma % tr == 0 and mb % tr == 0
    a_blocks = ma // tr
    grp = lambda i: group_of_row(i * tr)
    row_spec = pl.BlockSpec((tr, d), lambda i: (i, 0))
    return pl.pallas_call(
        functools.partial(_modulate_join_kernel, a_blocks=a_blocks),
        grid=((ma + mb) // tr,),
        in_specs=[
            pl.BlockSpec((tr, d), lambda i: (jnp.minimum(i, a_blocks - 1), 0)),
            pl.BlockSpec((tr, d), lambda i: (jnp.maximum(i - a_blocks, 0), 0)),
            pl.BlockSpec((None, 1, d), lambda i: (layer, 0, 0)),
            pl.BlockSpec((None, None, 1, d), lambda i: (grp(i), shift_idx, 0, 0)),
            pl.BlockSpec((None, None, 1, d), lambda i: (grp(i), shift_idx + 1, 0, 0)),
        ],
        out_specs=[row_spec, row_spec],
        out_shape=[jax.ShapeDtypeStruct((ma + mb, d), BF16), jax.ShapeDtypeStruct((ma + mb, d), F32)],
        compiler_params=_params("arbitrary"),
    )(xa, xb, gains.reshape(depth, 1, d), mods, mods)


def _modulate_call(x, gains, mods, layer, shift_idx, group_of_row, tr=512):
    m, d = x.shape
    depth = gains.shape[0]
    grp = lambda i: group_of_row(i * tr)
    return pl.pallas_call(
        _modulate_kernel,
        grid=(m // tr,),
        in_specs=[
            pl.BlockSpec((tr, d), lambda i: (i, 0)),
            pl.BlockSpec((None, 1, d), lambda i: (layer, 0, 0)),
            pl.BlockSpec((None, None, 1, d), lambda i: (grp(i), shift_idx, 0, 0)),
            pl.BlockSpec((None, None, 1, d), lambda i: (grp(i), shift_idx + 1, 0, 0)),
        ],
        out_specs=pl.BlockSpec((tr, d), lambda i: (i, 0)),
        out_shape=jax.ShapeDtypeStruct((m, d), BF16),
        compiler_params=_params("arbitrary"),
    )(x, gains.reshape(depth, 1, d), mods, mods)


def _matmul_tiles(m, k, n, n_weights, out_bytes, residual, row_quantum):
    for tn in (2048, 1024, 512, 256, 128):
        if n % tn:
            continue
        for tm in (1024, 512, 256):
            if m % tm or row_quantum % tm:
                continue
            need = (2 * tm * k * 2 + n_weights * (2 * k * tn * 4 + k * tn * 2 + tm * tn * 4)
                    + 2 * tm * tn * out_bytes + (2 * tm * tn * 4 if residual else 0))
            if need <= VMEM_TILE_BUDGET:
                return tm, tn
    raise ValueError(f"no matmul tiling fits VMEM for {(m, k, n)}")


def _cast_weights_once(w_refs, wbf_refs):
    @pl.when(pl.program_id(1) == 0)
    def _():
        for w_ref, wbf_ref in zip(w_refs, wbf_refs):
            wbf_ref[...] = w_ref[...].astype(BF16)


def _mm_plain_kernel(a_ref, w_ref, o_ref, wbf_ref):
    _cast_weights_once((w_ref,), (wbf_ref,))
    o_ref[...] = jnp.dot(a_ref[...], wbf_ref[...], preferred_element_type=F32).astype(o_ref.dtype)


def _mm_swiglu_kernel(*refs, adaln_steps):
    if adaln_steps:
        a_ref, wg_ref, wu_ref, c_ref, wm_ref, bm_ref, o_ref, om_ref, wgbf_ref, wubf_ref = refs
    else:
        a_ref, wg_ref, wu_ref, o_ref, wgbf_ref, wubf_ref = refs
    _cast_weights_once((wg_ref, wu_ref), (wgbf_ref, wubf_ref))
    a = a_ref[...]
    g = jnp.dot(a, wgbf_ref[...], preferred_element_type=F32)
    u = jnp.dot(a, wubf_ref[...], preferred_element_type=F32)
    o_ref[...] = (_silu(g) * u).astype(o_ref.dtype)
    if adaln_steps:
        @pl.when(pl.program_id(0) * pl.num_programs(1) + pl.program_id(1) < adaln_steps)
        def _():
            om_ref[...] = _adaln_tile(c_ref, wm_ref, bm_ref)


def _mm_residual_kernel(a_ref, w_ref, x_ref, gate_ref, o_ref, wbf_ref):
    _cast_weights_once((w_ref,), (wbf_ref,))
    y = jnp.dot(a_ref[...], wbf_ref[...], preferred_element_type=F32)
    o_ref[...] = x_ref[...] + gate_ref[...] * y


def _mm_residual_modulate_kernel(a_ref, w_ref, x_ref, gate_ref, g_ref, shift_ref, scale_ref,
                                 ox_ref, oh_ref, wbf_ref):
    @pl.when(pl.program_id(0) == 0)
    def _():
        wbf_ref[...] = w_ref[...].astype(BF16)
    y = jnp.dot(a_ref[...], wbf_ref[...], preferred_element_type=F32)
    x = x_ref[...] + gate_ref[...] * y
    ox_ref[...] = x
    h = x * lax.rsqrt(jnp.mean(x * x, axis=-1, keepdims=True) + EPS) * g_ref[...]
    oh_ref[...] = (h * (1.0 + scale_ref[...]) + shift_ref[...]).astype(BF16)


def _mm_residual_modulate_call(a, w, widx, x, mods, layer, gate_idx, gains, shift_idx, group_of_row,
                               row_quantum):
    m, k = a.shape
    n = w.shape[-1]
    depth = gains.shape[0]
    for tm in (512, 256, 128):
        need = k * n * (4 + 2) + 2 * tm * (k * 2 + n * (4 + 4 + 2)) + 2 * tm * n * 4
        if m % tm == 0 and row_quantum % tm == 0 and need <= VMEM_TILE_BUDGET:
            break
    else:
        raise ValueError(f"no row tile fits VMEM for {(m, k, n)}")
    grp = lambda i: group_of_row(i * tm)
    mod_spec = lambda which: pl.BlockSpec((None, None, 1, n), lambda i: (grp(i), which, 0, 0))
    row_spec = pl.BlockSpec((tm, n), lambda i: (i, 0))
    return pl.pallas_call(
        _mm_residual_modulate_kernel,
        grid=(m // tm,),
        in_specs=[
            pl.BlockSpec((tm, k), lambda i: (i, 0)),
            pl.BlockSpec((None, k, n), lambda i: (widx, 0, 0), pipeline_mode=pl.Buffered(1)),
            row_spec,
            mod_spec(gate_idx),
            pl.BlockSpec((None, 1, n), lambda i: (layer, 0, 0)),
            mod_spec(shift_idx),
            mod_spec(shift_idx + 1),
        ],
        out_specs=[row_spec, row_spec],
        out_shape=[jax.ShapeDtypeStruct((m, n), F32), jax.ShapeDtypeStruct((m, n), BF16)],
        scratch_shapes=[pltpu.VMEM((k, n), BF16)],
        compiler_params=_params("arbitrary"),
    )(a, w, x, mods, gains.reshape(depth, 1, n), mods, mods)


def _mm_plain_call(a, w, widx, out_dtype):
    m, k = a.shape
    n = w.shape[-1]
    tm, tn = _matmul_tiles(m, k, n, 1, jnp.dtype(out_dtype).itemsize, False, m)
    return pl.pallas_call(
        _mm_plain_kernel,
        grid=(n // tn, m // tm),
        in_specs=[
            pl.BlockSpec((tm, k), lambda j, i: (i, 0)),
            pl.BlockSpec((None, k, tn), lambda j, i: (widx, 0, j)),
        ],
        out_specs=pl.BlockSpec((tm, tn), lambda j, i: (i, j)),
        out_shape=jax.ShapeDtypeStruct((m, n), out_dtype),
        scratch_shapes=[pltpu.VMEM((k, tn), BF16)],
        compiler_params=_params("arbitrary", "arbitrary"),
    )(a, w)


ADALN_TILE = 256


def _mm_swiglu_call(a, wg, wu, widx, adaln=None):
    m, k = a.shape
    n = wg.shape[-1]
    tm, tn = _matmul_tiles(m, k, n, 2, 2, False, m)
    wspec = pl.BlockSpec((None, k, tn), lambda j, i: (widx, 0, j))
    in_specs = [pl.BlockSpec((tm, k), lambda j, i: (i, 0)), wspec, wspec]
    args = [a, wg, wu]
    out_specs = [pl.BlockSpec((tm, tn), lambda j, i: (i, j))]
    out_shape = [jax.ShapeDtypeStruct((m, n), BF16)]
    adaln_steps = 0
    if adaln is not None:
        cond, w_mod, b_mod, layer = adaln
        depth, d, n_mod = w_mod.shape
        rows = cond.shape[0]
        adaln_steps = n_mod // ADALN_TILE
        per_j = m // tm
        assert n_mod % ADALN_TILE == 0 and adaln_steps <= (n // tn) * per_j
        tile = lambda j, i: jnp.minimum(j * per_j + i, adaln_steps - 1)
        in_specs += [pl.BlockSpec((rows, d), lambda j, i: (0, 0)),
                     pl.BlockSpec((None, d, ADALN_TILE), lambda j, i: (layer, 0, tile(j, i))),
                     pl.BlockSpec((None, 1, ADALN_TILE), lambda j, i: (layer, 0, tile(j, i)))]
        args += [cond, w_mod, b_mod.reshape(depth, 1, n_mod)]
        out_specs.append(pl.BlockSpec((rows, ADALN_TILE), lambda j, i: (0, tile(j, i))))
        out_shape.append(jax.ShapeDtypeStruct((rows, n_mod), F32))
    outs = pl.pallas_call(
        functools.partial(_mm_swiglu_kernel, adaln_steps=adaln_steps),
        grid=(n // tn, m // tm),
        in_specs=in_specs,
        out_specs=out_specs,
        out_shape=out_shape,
        scratch_shapes=[pltpu.VMEM((k, tn), BF16), pltpu.VMEM((k, tn), BF16)],
        compiler_params=_params("arbitrary", "arbitrary"),
    )(*args)
    return outs if adaln is not None else outs[0]


def _mm_residual_call(a, w, widx, x, mods, gate_idx, group_of_row, row_quantum, row0=0, m=None):
    k = a.shape[1]
    m = a.shape[0] if m is None else m
    n = w.shape[-1]
    tm, tn = _matmul_tiles(m, k, n, 1, 4, True, row_quantum)
    assert row0 % tm == 0
    rb0 = row0 // tm
    grp = lambda i: group_of_row((i + rb0) * tm)
    return pl.pallas_call(
        _mm_residual_kernel,
        grid=(n // tn, m // tm),
        in_specs=[
            pl.BlockSpec((tm, k), lambda j, i: (i + rb0, 0)),
            pl.BlockSpec((None, k, tn), lambda j, i: (widx, 0, j)),
            pl.BlockSpec((tm, tn), lambda j, i: (i + rb0, j)),
            pl.BlockSpec((None, None, 1, tn), lambda j, i: (grp(i), gate_idx, 0, j)),
        ],
        out_specs=pl.BlockSpec((tm, tn), lambda j, i: (i, j)),
        out_shape=jax.ShapeDtypeStruct((m, n), F32),
        scratch_shapes=[pltpu.VMEM((k, tn), BF16)],
        compiler_params=_params("arbitrary", "arbitrary"),
    )(a, w, x, mods)


def _bcast_row_in_blocks(x, block, r):
    c, d = x.shape
    x3 = x.reshape(c // block, block, d)
    return jnp.broadcast_to(x3[:, r:r + 1, :], x3.shape).reshape(c, d)


MXU_WIDTH = 256


def _gla_blocks(c):
    return [1 << b for b in range((c // 2).bit_length())] + [0]


def _gla_masks(c, fwd):
    i = lax.broadcasted_iota(jnp.int32, (c, LANES), 0)
    lane = lax.broadcasted_iota(jnp.int32, (c, LANES), 1)
    masks = []
    for b, h in enumerate(_gla_blocks(c)):
        j = lane - (b * c) % LANES
        in_block = jnp.where(j >= 0, jnp.where(j < c, 1.0, 0.0), 0.0)
        if h == 0:
            ok = jnp.where(i == j, 1.0, 0.0)
        else:
            same = (i // (2 * h)) == (j // (2 * h))
            i_hi = (i % (2 * h)) >= h
            j_hi = (j % (2 * h)) >= h
            if fwd:
                ok = jnp.where(same, jnp.where(i_hi, jnp.where(j_hi, 0.0, 1.0), 0.0), 0.0)
            else:
                ok = jnp.where(same, jnp.where(i_hi, 0.0, jnp.where(j_hi, 1.0, 0.0)), 0.0)
        masks.append((ok * in_block).astype(F32))
    return masks


def _split_roles(q, kg, h, fwd, rloc):
    c = q.shape[0]
    if h >= SUBLANES:
        parts = []
        for b in range(c // (2 * h)):
            lo = slice(2 * h * b, 2 * h * b + h)
            hi = slice(2 * h * b + h, 2 * h * (b + 1))
            parts += [kg[lo], q[hi]] if fwd else [q[lo], kg[hi]]
        return jnp.concatenate(parts, axis=0)
    in_hi = (rloc % (2 * h)) >= h
    return jnp.where(in_hi, q, kg) if fwd else jnp.where(in_hi, kg, q)


def _split3(x):
    hi = x.astype(BF16)
    r1 = x - hi.astype(F32)
    mid = r1.astype(BF16)
    lo = (r1 - mid.astype(F32)).astype(BF16)
    return jnp.concatenate([hi, mid, lo], axis=-1)


def _gla_chunks(chains, rloc):
    c, d = chains[0][0].shape
    per_group = MXU_WIDTH // c
    blocks = _gla_blocks(c)
    n_groups = -(-len(blocks) // per_group)
    n_kvregs = -(-len(blocks) * c // LANES)

    es = [None] * len(chains)
    for direction in (True, False):
        ids = [n for n, ch in enumerate(chains) if ch[6] == direction]
        if ids:
            e3 = jnp.dot(chains[ids[0]][7], jnp.concatenate([chains[n][4] for n in ids], axis=1),
                         preferred_element_type=F32)
            for pos, n in enumerate(ids):
                base = 3 * d * pos
                es[n] = e3[:, base:base + d] + e3[:, base + d:base + 2 * d] + e3[:, base + 2 * d:base + 3 * d]

    sides = []
    for (q, kg, fg, v, logf, st, fwd, tri_bf, masks), e in zip(chains, es):
        qs, ks = [], []
        for h in blocks:
            if h == 0:
                qs.append(q.astype(BF16))
                ks.append(kg.astype(BF16))
                continue
            if h == 1:
                in_hi = (rloc % 2) >= 1
                w = jnp.where(in_hi, q * fg, kg) if fwd else jnp.where(in_hi, kg, q * fg)
            else:
                r = h - 1 if fwd else h
                if 2 * h >= SUBLANES:
                    e_ref = _bcast_row_in_blocks(e, 2 * h, r)
                else:
                    e_ref = jnp.where(rloc < 2 * h, _bcast_row_in_blocks(e, SUBLANES, r),
                                      _bcast_row_in_blocks(e, SUBLANES, 2 * h + r))
                w = _split_roles(q, kg, h, fwd, rloc) * (1.0 / jnp.exp2(jnp.abs(e - e_ref)))
            qs.append(w.astype(BF16))
            ks.append(qs[-1])
        pad = [jnp.zeros((c, d), BF16)] * (n_groups * per_group - len(blocks))
        sides.append((qs + pad, ks + pad))

    atts = [[None] * n_kvregs for _ in chains]
    for g in range(n_groups):
        for ci, (qs, ks) in enumerate(sides):
            sl = slice(g * per_group, (g + 1) * per_group)
            prod = lax.dot_general(jnp.concatenate(qs[sl], axis=0), jnp.concatenate(ks[sl], axis=0),
                                   NT_DIMS, preferred_element_type=F32)
            masks = chains[ci][8]
            for b in range(g * per_group, min((g + 1) * per_group, len(blocks))):
                t = b * c // LANES
                col = (t * LANES) % MXU_WIDTH
                row = (b - g * per_group) * c
                term = prod[row:row + c, col:col + LANES] * masks[b]
                atts[ci][t] = term if atts[ci][t] is None else atts[ci][t] + term

    outs = []
    for (q, kg, fg, v, logf, st, fwd, tri_bf, masks), e, att in zip(chains, es, atts):
        vb = v.astype(BF16)
        att_sum = functools.reduce(lambda x, y: x + y, att)
        v_rep = jnp.concatenate([vb] * (LANES // c), axis=0)
        o = jnp.dot(att_sum.astype(BF16), v_rep, preferred_element_type=F32)
        o = o + lax.dot_general((q * jnp.exp2(e)).astype(BF16), st.astype(BF16), NT_DIMS,
                                preferred_element_type=F32)
        e_edge = e[c - 1:c] if fwd else e[0:1]
        kh = (kg * jnp.exp2(e_edge - e)).astype(BF16)
        st_new = st * jnp.exp2(e_edge) + lax.dot_general(vb, kh, TN_DIMS, preferred_element_type=F32)
        outs.append((o, st_new))
    return outs


def _gla_kernel(*refs, seq_len, chunk, has_s0, final):
    refs = list(refs)
    q_ref, ffw_ref, fbw_ref, v_ref, g_ref, lb_ref, on_ref = refs[:7]
    del refs[:7]
    s0_ref = refs.pop(0) if has_s0 else None
    del refs[:2 if final == "join" else 1]
    o_ref = refs.pop(0)
    sf_ref = _final_state_slot(refs.pop(0), final) if final is not None else None
    stf_ref, stb_ref, oacc_ref = refs
    c = chunk
    nch = seq_len // c
    half_n = nch // 2
    d = A_HEAD_DIM
    n_h = q_ref.shape[-1] // d

    i = lax.broadcasted_iota(jnp.int32, (c, c), 0)
    j = lax.broadcasted_iota(jnp.int32, (c, c), 1)
    tri_f = jnp.where(i >= j, 1.0, 0.0).astype(BF16)
    tri_b = jnp.where(i <= j, 1.0, 0.0).astype(BF16)
    masks_f = _gla_masks(c, True)
    masks_b = _gla_masks(c, False)
    rloc = lax.broadcasted_iota(jnp.int32, (c, d), 0) % SUBLANES
    q_scale = d ** -0.5

    def gates(x, lb):
        one_m_lb = 1.0 - lb
        t = jnp.exp(-jnp.abs(x))
        r = 1.0 / (1.0 + t)
        pos = x >= 0.0
        fg = lb + one_m_lb * jnp.where(pos, r, t * r)
        kg = one_m_lb * jnp.where(pos, t * r, r)
        log2f = jnp.maximum(jnp.log2(fg),
                            jnp.log2(one_m_lb) + jnp.minimum(x, 0.0) * LOG2_E - jnp.log2(1.0 + t))
        return fg, kg, log2f

    def chain_rows(t, fwd):
        return pl.ds(pl.multiple_of((t if fwd else nch - 1 - t) * c, c), c)

    chain_ids = [(hh, fwd) for hh in range(n_h) for fwd in (True, False)]

    def all_chains(t):
        where, chains = [], []
        for hh, fwd in chain_ids:
            sl, cols = chain_rows(t, fwd), slice(hh * d, (hh + 1) * d)
            fg, kg, log2f = gates((ffw_ref if fwd else fbw_ref)[sl, cols], lb_ref[:, cols])
            where.append((sl, cols))
            chains.append((_silu(q_ref[sl, cols]) * q_scale, kg, fg, v_ref[sl, cols], _split3(log2f),
                           (stf_ref if fwd else stb_ref)[hh], fwd,
                           tri_f if fwd else tri_b, masks_f if fwd else masks_b))
        outs = _gla_chunks(chains, rloc)
        for n, (o, st) in enumerate(outs):
            (stf_ref if n % 2 == 0 else stb_ref)[n // 2] = st
        return [(sl, cols, o) for (sl, cols), (o, st) in zip(where, outs)]

    def finish(sl, cols, o):
        y = oacc_ref[sl, cols] + o
        y = y * lax.rsqrt(jnp.mean(y * y, axis=-1, keepdims=True) + EPS) * on_ref[...]
        o_ref[sl, cols] = (y * _silu(g_ref[sl, cols])).astype(o_ref.dtype)

    def first_half(t, carry):
        for sl, cols, o in all_chains(t):
            oacc_ref[sl, cols] = o
        return carry

    def second_half(t, carry):
        for sl, cols, o in all_chains(t):
            finish(sl, cols, o)
        return carry

    for hh in range(n_h):
        for st_ref, direction in ((stf_ref, 0), (stb_ref, 1)):
            st_ref[hh] = s0_ref[direction, hh].T if has_s0 else jnp.zeros((d, d), F32)
    unroll = 2 if 2 * n_h < 8 and half_n % 2 == 0 else 1
    lax.fori_loop(0, half_n, first_half, 0, unroll=unroll)
    lax.fori_loop(half_n, nch, second_half, 0, unroll=unroll)
    if final is not None:
        for hh in range(n_h):
            sf_ref[0, hh] = stf_ref[hh].T
            sf_ref[1, hh] = stb_ref[hh].T


GLA_CHUNK = 128


def _join_buffers(bufs, in_specs, args):
    shapes, aliases = [], {}
    for n, buf in enumerate(bufs):
        in_specs.append(pl.BlockSpec(memory_space=pl.ANY))
        args.append(buf)
        shapes.append(jax.ShapeDtypeStruct(buf.shape, buf.dtype))
        aliases[len(args) - 1] = n
    return shapes, aliases


def _final_state_specs(finals, slot, dims, heads_per_step, out_specs):
    n_seq, heads, d = dims
    if finals is None:
        return [], None
    if isinstance(finals, int):
        out_specs.append(pl.BlockSpec((None, finals, 2, heads_per_step, d, d),
                                      lambda n, h: (n, 0, 0, h, 0, 0)))
        return [], (slot, finals)
    out_specs.append(pl.BlockSpec((None, None, 2, heads_per_step, d, d),
                                  lambda n, h: (n, slot, 0, h, 0, 0)))
    return [finals], "join"


def _final_state_slot(sf_ref, final):
    if final == "join":
        return sf_ref
    slot, n_slots = final
    for s in range(n_slots):
        if s != slot:
            sf_ref[s] = jnp.zeros(sf_ref.shape[1:], sf_ref.dtype)
    return sf_ref.at[slot]


def _gla_heads_per_step(seq_len):
    for n_h in (4, 2, 1):
        if 2 * 5 * seq_len * n_h * A_HEAD_DIM * 4 + 4 * seq_len * n_h * A_HEAD_DIM * 2 <= VMEM_TILE_BUDGET:
            return n_h
    raise ValueError(f"sequence of {seq_len} rows does not fit VMEM")


def _gla_call(p, lower_bounds, layer, onorm, eidx, s0, n_seq, seq_len, row0, finals, mix):
    d = A_HEAD_DIM
    n_h = _gla_heads_per_step(seq_len)
    w = n_h * d
    a_width = lower_bounds.shape[-1]
    n_hb = a_width // w
    assert row0 % seq_len == 0 and seq_len % (2 * GLA_CHUNK) == 0 and a_width % w == 0
    sb = row0 // seq_len
    slab = lambda k: pl.BlockSpec((seq_len, w), lambda n, h: (n + sb, k * n_hb + h))
    in_specs = [slab(0), slab(1), slab(2), slab(3), slab(4),
                pl.BlockSpec((None, 1, w), lambda n, h: (layer, 0, h)),
                pl.BlockSpec((None, 1, d), lambda n, h: (eidx, 0, 0))]
    args = [p, p, p, p, p, lower_bounds.reshape(lower_bounds.shape[0], 1, a_width),
            onorm.reshape(onorm.shape[0], 1, d)]
    if s0 is not None:
        in_specs.append(pl.BlockSpec((None, None, 2, n_h, d, d), lambda n, h: (n, eidx, 0, h, 0, 0)))
        args.append(s0)
    out_specs = [pl.BlockSpec((seq_len, w), lambda n, h: (n + sb, h))]
    joined, final = _final_state_specs(finals, eidx, (n_seq, a_width // d, d), n_h, out_specs)
    out_shape, aliases = _join_buffers([mix] + joined, in_specs, args)
    if isinstance(final, tuple):
        out_shape.append(jax.ShapeDtypeStruct((n_seq, final[1], 2, a_width // d, d, d), F32))
    kern = functools.partial(_gla_kernel, seq_len=seq_len, chunk=GLA_CHUNK,
                             has_s0=s0 is not None, final=final)
    return pl.pallas_call(
        kern,
        grid=(n_seq, n_hb),
        in_specs=in_specs,
        out_specs=out_specs,
        out_shape=out_shape,
        input_output_aliases=aliases,
        scratch_shapes=[pltpu.VMEM((n_h, d, d), F32), pltpu.VMEM((n_h, d, d), F32),
                        pltpu.VMEM((seq_len, w), F32)],
        compiler_params=_params("arbitrary", "arbitrary"),
    )(*args)


def _ret_kernel(*refs, seq_len, chunk, has_s0, final, rope):
    refs = list(refs)
    q_ref, k_ref, v_ref, g_ref, dl_ref, gn_ref = refs[:6]
    del refs[:6]
    cos_ref = refs.pop(0) if rope else None
    sin_ref = refs.pop(0) if rope else None
    s0_ref = refs.pop(0) if has_s0 else None
    del refs[:2 if final == "join" else 1]
    o_ref = refs.pop(0)
    sf_ref = _final_state_slot(refs.pop(0), final) if final is not None else None
    stf_ref, stb_ref, oacc_ref = refs
    c = chunk
    nch = seq_len // c
    half_n = nch // 2
    d = dl_ref.shape[-1]
    n_h = q_ref.shape[-1] // d
    half = d // 2
    k_scale = d ** -0.5

    dist = (lax.broadcasted_iota(jnp.int32, (c, c), 0)
            - lax.broadcasted_iota(jnp.int32, (c, c), 1)).astype(F32)
    low = dist >= 0.0
    up = dist <= 0.0
    row = lax.broadcasted_iota(jnp.int32, (c, d), 0).astype(F32)

    def decay_tables(hh):
        lg_f = _log_sigmoid(dl_ref[0, hh])
        lg_b = _log_sigmoid(dl_ref[1, hh])
        intra = (jnp.where(low, jnp.exp(jnp.where(low, dist, 0.0) * lg_f[:, :c]), 0.0)
                 + jnp.where(up, jnp.exp(jnp.where(up, -dist, 0.0) * lg_b[:, :c]), 0.0))
        fwd = (jnp.exp((row + 1.0) * lg_f), jnp.exp((c - 1.0 - row) * lg_f), jnp.exp(c * lg_f))
        bwd = (jnp.exp((c - row) * lg_b), jnp.exp(row * lg_b), jnp.exp(c * lg_b))
        return intra, fwd, bwd

    tables = [decay_tables(hh) for hh in range(n_h)]

    def rotate(x, sl):
        if not rope:
            return x
        cos, sin = cos_ref[sl, :], sin_ref[sl, :]
        x1, x2 = x[:, :half], x[:, half:]
        return jnp.concatenate([x1 * cos - x2 * sin, x1 * sin + x2 * cos], axis=-1)

    def load(ci, hh):
        sl, cols = pl.ds(pl.multiple_of(ci * c, c), c), slice(hh * d, (hh + 1) * d)
        return (sl, cols, rotate(q_ref[sl, cols], sl), rotate(k_ref[sl, cols] * k_scale, sl),
                v_ref[sl, cols].astype(BF16))

    def carried(q, k, vb, st_ref, hh, decs):
        q_dec, k_dec, c_dec = decs
        st = st_ref[hh]
        o = jnp.dot((q * q_dec).astype(BF16), st.astype(BF16), preferred_element_type=F32)
        st_ref[hh] = c_dec * st + lax.dot_general((k * k_dec).astype(BF16), vb, TN_DIMS,
                                                  preferred_element_type=F32)
        return o

    def trip(t):
        left = [load(t, hh) for hh in range(n_h)]
        right = [load(nch - 1 - t, hh) for hh in range(n_h)]
        scores = [lax.dot_general(q.astype(BF16), k.astype(BF16), NT_DIMS, preferred_element_type=F32)
                  for _, _, q, k, _ in left]
        o_left = [carried(q, k, vb, stf_ref, hh, tables[hh][1]) for hh, (_, _, q, k, vb) in enumerate(left)]
        o_right = [carried(q, k, vb, stb_ref, hh, tables[hh][2]) for hh, (_, _, q, k, vb) in enumerate(right)]
        for hh, (_, _, _, _, vb) in enumerate(left):
            o_left[hh] = o_left[hh] + jnp.dot((scores[hh] * tables[hh][0]).astype(BF16), vb,
                                              preferred_element_type=F32)
        return ([(sl, cols, o) for (sl, cols, *_), o in zip(left, o_left)]
                + [(sl, cols, o) for (sl, cols, *_), o in zip(right, o_right)])

    def finish(sl, cols, o):
        y = oacc_ref[sl, cols] + o
        yc = y - jnp.mean(y, axis=-1, keepdims=True)
        yn = yc * lax.rsqrt(jnp.mean(yc * yc, axis=-1, keepdims=True) + EPS)
        o_ref[sl, cols] = (yn * gn_ref[:, cols] * _silu(g_ref[sl, cols])).astype(o_ref.dtype)

    def first_half(t, carry):
        for sl, cols, o in trip(t):
            oacc_ref[sl, cols] = o
        return carry

    def second_half(t, carry):
        for sl, cols, o in trip(t):
            finish(sl, cols, o)
        return carry

    for hh in range(n_h):
        for st_ref, direction in ((stf_ref, 0), (stb_ref, 1)):
            st_ref[hh] = s0_ref[direction, hh] if has_s0 else jnp.zeros((d, d), F32)
    lax.fori_loop(0, half_n, first_half, 0)
    lax.fori_loop(half_n, nch, second_half, 0)
    if final is not None:
        for hh in range(n_h):
            sf_ref[0, hh] = stf_ref[hh]
            sf_ref[1, hh] = stb_ref[hh]


def _ret_call(p, col0, decay_logit, gnorm, eidx, rope_tabs, s0, n_seq, seq_len, row0, finals,
              mix, mix_col0):
    n_heads = decay_logit.shape[-1]
    b_width = gnorm.shape[-1]
    d = b_width // n_heads
    chunk = min(d, seq_len // 2)
    n_h = 2 if n_heads % 2 == 0 else 1
    w = n_h * d
    n_hb = n_heads // n_h
    assert row0 % seq_len == 0 and seq_len % (2 * chunk) == 0 and col0 % w == 0 and chunk <= d
    assert mix_col0 % w == 0
    sb = row0 // seq_len
    cb0 = col0 // w
    ocb0 = mix_col0 // w
    slab = lambda k: pl.BlockSpec((seq_len, w), lambda n, h: (n + sb, cb0 + k * n_hb + h))
    dl = jnp.broadcast_to(decay_logit[:, :, :, None, None], decay_logit.shape + (1, d))
    in_specs = [slab(0), slab(1), slab(2), slab(3),
                pl.BlockSpec((None, 2, n_h, 1, d), lambda n, h: (eidx, 0, h, 0, 0)),
                pl.BlockSpec((None, 1, w), lambda n, h: (eidx, 0, h))]
    args = [p, p, p, p, dl, gnorm.reshape(gnorm.shape[0], 1, b_width)]
    if rope_tabs is not None:
        tab = pl.BlockSpec((seq_len, d // 2), lambda n, h: (0, 0))
        in_specs += [tab, tab]
        args += list(rope_tabs)
    if s0 is not None:
        in_specs.append(pl.BlockSpec((None, None, 2, n_h, d, d), lambda n, h: (n, eidx, 0, h, 0, 0)))
        args.append(s0)
    out_specs = [pl.BlockSpec((seq_len, w), lambda n, h: (n + sb, ocb0 + h))]
    joined, final = _final_state_specs(finals, eidx, (n_seq, n_heads, d), n_h, out_specs)
    out_shape, aliases = _join_buffers([mix] + joined, in_specs, args)
    if isinstance(final, tuple):
        out_shape.append(jax.ShapeDtypeStruct((n_seq, final[1], 2, n_heads, d, d), F32))
    kern = functools.partial(_ret_kernel, seq_len=seq_len, chunk=chunk, has_s0=s0 is not None,
                             final=final, rope=rope_tabs is not None)
    return pl.pallas_call(
        kern,
        grid=(n_seq, n_hb),
        in_specs=in_specs,
        out_specs=out_specs,
        out_shape=out_shape,
        input_output_aliases=aliases,
        scratch_shapes=[pltpu.VMEM((n_h, d, d), F32), pltpu.VMEM((n_h, d, d), F32),
                        pltpu.VMEM((seq_len, w), F32)],
        compiler_params=_params("arbitrary", "arbitrary"),
    )(*args)


Q_PRESCALE = ATTN_HEAD_DIM ** -0.5 * LOG2_E


def _head_norm(x, gain):
    return x * lax.rsqrt(jnp.mean(x * x, axis=-1, keepdims=True) + EPS) * gain


def _rotate(y, cos, sin):
    return y * cos + pltpu.roll(y, y.shape[-1] // 2, 1) * sin


def _mm_qproj_kernel(a_ref, w_ref, gain_ref, cos_ref, sin_ref, o_ref, wbf_ref):
    _cast_weights_once((w_ref,), (wbf_ref,))
    y = jnp.dot(a_ref[...], wbf_ref[...], preferred_element_type=F32)
    hd = ATTN_HEAD_DIM
    for h in range(y.shape[-1] // hd):
        cols = slice(h * hd, (h + 1) * hd)
        z = _rotate(_head_norm(y[:, cols], gain_ref[...]), cos_ref[...], sin_ref[...])
        o_ref[:, cols] = (z * Q_PRESCALE).astype(BF16)


def _mm_qproj_call(a, w, widx, n_cols, q_norm, rope_rows):
    m, k = a.shape
    hd = ATTN_HEAD_DIM
    tm, tn = _matmul_tiles(m, k, n_cols, 1, 2, False, m)
    tab = pl.BlockSpec((tm, hd), lambda j, i: (i, 0))
    return pl.pallas_call(
        _mm_qproj_kernel,
        grid=(n_cols // tn, m // tm),
        in_specs=[
            pl.BlockSpec((tm, k), lambda j, i: (i, 0)),
            pl.BlockSpec((None, k, tn), lambda j, i: (widx, 0, j)),
            pl.BlockSpec((None, 1, hd), lambda j, i: (widx, 0, 0)),
            tab, tab,
        ],
        out_specs=pl.BlockSpec((tm, tn), lambda j, i: (i, j)),
        out_shape=jax.ShapeDtypeStruct((m, n_cols), BF16),
        scratch_shapes=[pltpu.VMEM((k, tn), BF16)],
        compiler_params=_params("arbitrary", "arbitrary"),
    )(a, w, q_norm.reshape(-1, 1, hd), *rope_rows)


def _mm_kvproj_kernel(*refs, cache, cache_tiles):
    refs = list(refs)
    a_ref, w_ref, gain_ref, cos_ref, sin_ref = refs[:5]
    del refs[:7 if cache == "join" else 5]
    kf_out, vf_out, k_out, v_out, wbf_ref = refs
    _cast_weights_once((w_ref,), (wbf_ref,))
    y = jnp.dot(a_ref[...], wbf_ref[...], preferred_element_type=F32)
    hd = ATTN_HEAD_DIM
    kvw = y.shape[-1] // 2
    keys = []
    for h in range(kvw // hd):
        cols = slice(h * hd, (h + 1) * hd)
        keys.append(_rotate(_head_norm(y[:, cols], gain_ref[...]), cos_ref[...], sin_ref[...]))
        k_out[:, cols] = keys[-1].astype(BF16)
    v = y[:, kvw:]
    v_out[...] = v.astype(BF16)

    @pl.when(pl.program_id(1) < cache_tiles)
    def _():
        kf, vf = kf_out, vf_out
        if cache != "join":
            slot, n_slots = cache
            for s in range(n_slots):
                if s != slot:
                    kf_out[:, s] = jnp.zeros(kf_out.shape[:1] + kf_out.shape[2:], F32)
                    vf_out[:, s] = jnp.zeros(vf_out.shape[:1] + vf_out.shape[2:], F32)
            kf, vf = kf_out.at[:, slot], vf_out.at[:, slot]
        seqs, seq_len = kf.shape[0], kf.shape[1]
        for h, z in enumerate(keys):
            kf[:, :, h * hd:(h + 1) * hd] = z.reshape(seqs, seq_len, hd)
        vf[...] = v.reshape(seqs, seq_len, kvw)


def _mm_kvproj_call(a, w, widx, col0, k_norm, rope_rows, cache_seqs, cache_seq_len, caches):
    m, k = a.shape
    hd = ATTN_HEAD_DIM
    kvw = ATTN_KV_HEADS * hd
    tn = 2 * kvw
    assert col0 % tn == 0
    tm, _ = _matmul_tiles(m, k, tn, 1, 2 + 4, False, m)
    assert tm % cache_seq_len == 0 and (cache_seqs * cache_seq_len) % tm == 0
    seqs_per_tile = tm // cache_seq_len
    cache_tiles = cache_seqs * cache_seq_len // tm
    tab = pl.BlockSpec((tm, hd), lambda j, i: (i, 0))
    row_spec = pl.BlockSpec((tm, kvw), lambda j, i: (i, 0))
    in_specs = [pl.BlockSpec((tm, k), lambda j, i: (i, 0)),
                pl.BlockSpec((None, k, tn), lambda j, i: (widx, 0, col0 // tn)),
                pl.BlockSpec((None, 1, hd), lambda j, i: (widx, 0, 0)),
                tab, tab]
    args = [a, w, k_norm.reshape(-1, 1, hd), *rope_rows]
    held = lambda i: jnp.minimum(i, cache_tiles - 1)
    if isinstance(caches, int):
        cache = (widx, caches)
        cache_spec = pl.BlockSpec((seqs_per_tile, caches, cache_seq_len, kvw),
                                  lambda j, i: (held(i), 0, 0, 0))
        out_shape = [jax.ShapeDtypeStruct((cache_seqs, caches, cache_seq_len, kvw), F32)] * 2
        aliases = {}
    else:
        cache = "join"
        cache_spec = pl.BlockSpec((seqs_per_tile, None, cache_seq_len, kvw),
                                  lambda j, i: (held(i), widx, 0, 0))
        out_shape, aliases = _join_buffers(list(caches), in_specs, args)
    return pl.pallas_call(
        functools.partial(_mm_kvproj_kernel, cache=cache, cache_tiles=cache_tiles),
        grid=(1, m // tm),
        in_specs=in_specs,
        out_specs=[cache_spec, cache_spec, row_spec, row_spec],
        out_shape=out_shape + [jax.ShapeDtypeStruct((m, kvw), BF16)] * 2,
        input_output_aliases=aliases,
        scratch_shapes=[pltpu.VMEM((k, tn), BF16)],
        compiler_params=_params("arbitrary", "arbitrary"),
    )(*args)


def _attn_kernel(*refs, group):
    q_ref, k_ref, v_ref, _, o_ref = refs
    hd = ATTN_HEAD_DIM
    for kh in range(k_ref.shape[-1] // hd):
        k = k_ref[:, kh * hd:(kh + 1) * hd]
        v = v_ref[:, kh * hd:(kh + 1) * hd]
        v_ext = jnp.concatenate([v, jnp.ones_like(v)], axis=1)
        for g in range(group):
            cols = slice((kh * group + g) * hd, (kh * group + g + 1) * hd)
            s = lax.dot_general(q_ref[:, cols], k, NT_DIMS, preferred_element_type=F32)
            p = jnp.exp2(s - jnp.max(s, axis=-1, keepdims=True))
            o = jnp.dot(p.astype(BF16), v_ext, preferred_element_type=F32)
            o_ref[:, cols] = (o[:, :hd] / o[:, hd:]).astype(o_ref.dtype)


ATTN_SCORE_BUDGET = 4 * 1024 * 1024


ATTN_Q_ROWS = 1024


def _attn_call(q, k, v, n_seq, seq_len, row0, mix):
    kvw = k.shape[-1]
    qw = q.shape[-1]
    hd = ATTN_HEAD_DIM
    n_kv = kvw // hd
    group = qw // hd // n_kv
    tq = min(ATTN_Q_ROWS, seq_len)
    assert row0 % tq == 0 and seq_len % tq == 0 and row0 % seq_len == 0
    per_seq = seq_len // tq
    rb0 = row0 // tq
    lk = seq_len if k.ndim == 2 else k.shape[1]
    kv_step = n_kv if tq * lk * 4 * n_kv <= ATTN_SCORE_BUDGET else 1
    q_spec = pl.BlockSpec((tq, kv_step * group * hd), lambda n, kh, i: (rb0 + n * per_seq + i, kh))
    if k.ndim == 2:
        kv_spec = pl.BlockSpec((seq_len, kv_step * hd), lambda n, kh, i: (row0 // seq_len + n, kh))
    else:
        kv_spec = pl.BlockSpec((None, lk, kv_step * hd), lambda n, kh, i: (n, 0, kh))
    in_specs = [q_spec, kv_spec, kv_spec]
    args = [q, k, v]
    (mix_shape,), aliases = _join_buffers([mix], in_specs, args)
    return pl.pallas_call(
        functools.partial(_attn_kernel, group=group),
        grid=(n_seq, n_kv // kv_step, per_seq),
        in_specs=in_specs,
        out_specs=q_spec,
        out_shape=mix_shape,
        input_output_aliases=aliases,
        compiler_params=_params("arbitrary", "arbitrary", "arbitrary"),
    )(*args)


def _grid_rope(n_tokens, head_dim):
    rows = n_tokens // GRID_W
    row = jnp.broadcast_to(jnp.arange(rows)[:, None], (rows, GRID_W)).reshape(-1).astype(F32)
    col = jnp.broadcast_to(jnp.arange(GRID_W)[None, :], (rows, GRID_W)).reshape(-1).astype(F32)
    per_axis = head_dim // 4
    freqs = ROPE_THETA ** (-jnp.arange(per_axis, dtype=F32) / per_axis)
    ang = jnp.concatenate([row[:, None] * freqs, col[:, None] * freqs], axis=-1)
    return jnp.cos(ang), jnp.sin(ang)


def kernel(x_prompt, x_sample, c, state_hgrn, state_ret, cache_k, cache_v, c_ctx, w_mod, b_mod, norm_mix, norm_ffn, w_in_even, w_out_even, hgrn_lb_logits, hgrn_onorm, ret_decay_logit, ret_gnorm, w_in_attn, w_out_attn, q_norm, k_norm, w_ffn_gate, w_ffn_up, w_ffn_down):
    n_p, l_p, d = x_prompt.shape
    n_s, l_s, _ = x_sample.shape
    depth = w_mod.shape[0]
    rows_p = n_p * l_p
    rows_s = n_s * l_s
    a_width = hgrn_lb_logits.shape[-1]
    b_width = ret_gnorm.shape[-1]
    kv_width = ATTN_KV_HEADS * ATTN_HEAD_DIM
    row_quantum = math.gcd(rows_p, l_s)

    def group_of_row(r):
        return jnp.where(r < rows_p, 0, 1 + (r - rows_p) // l_s)

    cond = jnp.concatenate([c_ctx[None, :], c, jnp.zeros((-(1 + n_s) % SUBLANES, d), F32)], axis=0)
    as_mods = lambda m: m.reshape(cond.shape[0], 6, 1, d)
    mods = as_mods(_adaln_call(cond, w_mod, b_mod, 0))

    cos_a, sin_a = _grid_rope(l_s, ATTN_HEAD_DIM)
    rope_attn = (jnp.concatenate([cos_a, cos_a], axis=-1), jnp.concatenate([-sin_a, sin_a], axis=-1))
    rope_rows = (jnp.concatenate([jnp.ones((rows_p, ATTN_HEAD_DIM), F32), jnp.tile(rope_attn[0], (n_s, 1))]),
                 jnp.concatenate([jnp.zeros((rows_p, ATTN_HEAD_DIM), F32), jnp.tile(rope_attn[1], (n_s, 1))]))
    rope_ret = _grid_rope(l_s, b_width // B_HEADS)
    lb_cum = jnp.cumsum(jax.nn.softmax(hgrn_lb_logits.astype(F32), axis=0), axis=0)
    lower_bounds = lb_cum - lb_cum[0]

    n_even = w_in_even.shape[0]
    out_hgrn, out_ret = n_even, n_even
    out_kv = w_in_attn.shape[0]
    for l in range(depth):
        if l == 0:
            mix, x = _modulate_join_call(x_prompt.reshape(rows_p, d), x_sample.reshape(rows_s, d),
                                         norm_mix, mods, l, 0, group_of_row)
        else:
            mix = _modulate_call(x, norm_mix, mods, l, 0, group_of_row)
        if l % 2 == 0:
            e = l // 2
            p = _mm_plain_call(mix, w_in_even, e, F32)
            mix, out_hgrn = _gla_call(p, lower_bounds, l, hgrn_onorm, e, None, n_p, l_p, 0, out_hgrn, mix)
            (mix,) = _gla_call(p, lower_bounds, l, hgrn_onorm, e, state_hgrn, n_s, l_s, rows_p, None, mix)
            mix, out_ret = _ret_call(p, 5 * a_width, ret_decay_logit, ret_gnorm, e, None, None,
                                     n_p, l_p, 0, out_ret, mix, a_width)
            (mix,) = _ret_call(p, 5 * a_width, ret_decay_logit, ret_gnorm, e, rope_ret, state_ret,
                               n_s, l_s, rows_p, None, mix, a_width)
            w_out, widx = w_out_even, e
        else:
            o = l // 2
            q = _mm_qproj_call(mix, w_in_attn, o, d, q_norm, rope_rows)
            *out_kv, k, v = _mm_kvproj_call(mix, w_in_attn, o, d, k_norm, rope_rows, n_p, l_p, out_kv)
            past = cache_k.shape[2]
            with_cache = lambda new, cache: jnp.concatenate(
                [new[rows_p:].reshape(n_s, l_s, kv_width),
                 cache[:, o].reshape(n_s, past, kv_width).astype(BF16)], axis=1)
            k_all, v_all = with_cache(k, cache_k), with_cache(v, cache_v)
            mix = _attn_call(q, k, v, n_p, l_p, 0, mix)
            mix = _attn_call(q, k_all, v_all, n_s, l_s, rows_p, mix)
            w_out, widx = w_out_attn, o
        x, h = _mm_residual_modulate_call(mix, w_out, widx, x, mods, l, 2, norm_ffn, 3, group_of_row,
                                          row_quantum)
        if l < depth - 1:
            u, next_mods = _mm_swiglu_call(h, w_ffn_gate, w_ffn_up, l, (cond, w_mod, b_mod, l + 1))
        else:
            u = _mm_swiglu_call(h, w_ffn_gate, w_ffn_up, l)
        down = functools.partial(_mm_residual_call, u, w_ffn_down, l, x, mods, 5, group_of_row,
                                 row_quantum)
        if l < depth - 1:
            x = down()
            mods = as_mods(next_mods)
    return (down(0, rows_p).reshape(n_p, l_p, d), down(rows_p, rows_s).reshape(n_s, l_s, d),
            out_hgrn, out_ret,
            *(t.reshape(n_p, -1, l_p, ATTN_KV_HEADS, ATTN_HEAD_DIM) for t in out_kv))
```

```python
import functools
import math

import jax
import jax.numpy as jnp
from jax import lax
from jax.experimental import pallas as pl
from jax.experimental.pallas import tpu as pltpu

F32 = jnp.float32
BF16 = jnp.bfloat16

EPS = 1e-6
ROPE_THETA = 10000.0
GRID_W = 64
A_HEAD_DIM = 128
B_HEADS = 4
ATTN_HEAD_DIM = 128
ATTN_KV_HEADS = 4
LANES = 128
SUBLANES = 8
VMEM_LIMIT = 56 * 1024 * 1024
VMEM_TILE_BUDGET = 50 * 1024 * 1024

LOG2_E = math.log2(math.e)

NT_DIMS = (((1,), (1,)), ((), ()))
TN_DIMS = (((0,), (0,)), ((), ()))


def _params(*sem):
    return pltpu.CompilerParams(dimension_semantics=sem, vmem_limit_bytes=VMEM_LIMIT)


def _silu(x):
    return x / (1.0 + jnp.exp(-x))


def _log_sigmoid(x):
    return jnp.minimum(x, 0.0) - jnp.log1p(jnp.exp(-jnp.abs(x)))


def _adaln_tile(c_ref, w_ref, b_ref):
    s = _silu(c_ref[...]).astype(BF16)
    return jnp.dot(s, w_ref[...].astype(BF16), preferred_element_type=F32) + b_ref[...]


def _adaln_kernel(c_ref, w_ref, b_ref, o_ref):
    o_ref[...] = _adaln_tile(c_ref, w_ref, b_ref)


def _adaln_call(cond, w_mod, b_mod, layer, tn=1024):
    depth, d, n = w_mod.shape
    rows = cond.shape[0]
    return pl.pallas_call(
        _adaln_kernel,
        grid=(n // tn,),
        in_specs=[
            pl.BlockSpec((rows, d), lambda j: (0, 0)),
            pl.BlockSpec((None, d, tn), lambda j: (layer, 0, j)),
            pl.BlockSpec((None, 1, tn), lambda j: (layer, 0, j)),
        ],
        out_specs=pl.BlockSpec((rows, tn), lambda j: (0, j)),
        out_shape=jax.ShapeDtypeStruct((rows, n), F32),
        compiler_params=_params("arbitrary"),
    )(cond, w_mod, b_mod.reshape(depth, 1, n))


def _modulate_kernel(x_ref, g_ref, shift_ref, scale_ref, o_ref):
    x = x_ref[...]
    y = x * lax.rsqrt(jnp.mean(x * x, axis=-1, keepdims=True) + EPS) * g_ref[...]
    o_ref[...] = (y * (1.0 + scale_ref[...]) + shift_ref[...]).astype(BF16)


def _modulate_join_kernel(xa_ref, xb_ref, g_ref, shift_ref, scale_ref, o_ref, ox_ref, *, a_blocks):
    def from_ref(x_ref):
        ox_ref[...] = x_ref[...]
        _modulate_kernel(x_ref, g_ref, shift_ref, scale_ref, o_ref)

    pl.when(pl.program_id(0) < a_blocks)(lambda: from_ref(xa_ref))
    pl.when(pl.program_id(0) >= a_blocks)(lambda: from_ref(xb_ref))


def _modulate_join_call(xa, xb, gains, mods, layer, shift_idx, group_of_row, tr=512):
    (ma, d), mb = xa.shape, xb.shape[0]
    depth = gains.shape[0]
    assert ma % tr == 0 and mb % tr == 0
    a_blocks = ma // tr
    grp = lambda i: group_of_row(i * tr)
    row_spec = pl.BlockSpec((tr, d), lambda i: (i, 0))
    return pl.pallas_call(
        functools.partial(_modulate_join_kernel, a_blocks=a_blocks),
        grid=((ma + mb) // tr,),
        in_specs=[
            pl.BlockSpec((tr, d), lambda i: (jnp.minimum(i, a_blocks - 1), 0)),
            pl.BlockSpec((tr, d), lambda i: (jnp.maximum(i - a_blocks, 0), 0)),
            pl.BlockSpec((None, 1, d), lambda i: (layer, 0, 0)),
            pl.BlockSpec((None, None, 1, d), lambda i: (grp(i), shift_idx, 0, 0)),
            pl.BlockSpec((None, None, 1, d), lambda i: (grp(i), shift_idx + 1, 0, 0)),
        ],
        out_specs=[row_spec, row_spec],
        out_shape=[jax.ShapeDtypeStruct((ma + mb, d), BF16), jax.ShapeDtypeStruct((ma + mb, d), F32)],
        compiler_params=_params("arbitrary"),
    )(xa, xb, gains.reshape(depth, 1, d), mods, mods)


def _modulate_call(x, gains, mods, layer, shift_idx, group_of_row, tr=512):
    m, d = x.shape
    depth = gains.shape[0]
    grp = lambda i: group_of_row(i * tr)
    return pl.pallas_call(
        _modulate_kernel,
        grid=(m // tr,),
        in_specs=[
            pl.BlockSpec((tr, d), lambda i: (i, 0)),
            pl.BlockSpec((None, 1, d), lambda i: (layer, 0, 0)),
            pl.BlockSpec((None, None, 1, d), lambda i: (grp(i), shift_idx, 0, 0)),
            pl.BlockSpec((None, None, 1, d), lambda i: (grp(i), shift_idx + 1, 0, 0)),
        ],
        out_specs=pl.BlockSpec((tr, d), lambda i: (i, 0)),
        out_shape=jax.ShapeDtypeStruct((m, d), BF16),
        compiler_params=_params("arbitrary"),
    )(x, gains.reshape(depth, 1, d), mods, mods)


def _matmul_tiles(m, k, n, n_weights, out_bytes, residual, row_quantum):
    for tn in (2048, 1024, 512, 256, 128):
        if n % tn:
            continue
        for tm in (2048, 1024, 512, 256):
            if m % tm or row_quantum % tm:
                continue
            need = (2 * tm * k * 2 + n_weights * (2 * k * tn * 4 + k * tn * 2 + tm * tn * 4)
                    + 2 * tm * tn * out_bytes + (2 * tm * tn * 4 if residual else 0))
            if need <= VMEM_TILE_BUDGET:
                return tm, tn
    raise ValueError(f"no matmul tiling fits VMEM for {(m, k, n)}")


def _cast_weights_once(w_refs, wbf_refs):
    @pl.when(pl.program_id(1) == 0)
    def _():
        for w_ref, wbf_ref in zip(w_refs, wbf_refs):
            wbf_ref[...] = w_ref[...].astype(BF16)


def _mm_plain_kernel(a_ref, w_ref, o_ref, wbf_ref):
    _cast_weights_once((w_ref,), (wbf_ref,))
    o_ref[...] = jnp.dot(a_ref[...], wbf_ref[...], preferred_element_type=F32).astype(o_ref.dtype)


def _mm_swiglu_kernel(a_ref, wg_ref, wu_ref, o_ref, wgbf_ref, wubf_ref):
    _cast_weights_once((wg_ref, wu_ref), (wgbf_ref, wubf_ref))
    a = a_ref[...]
    g = jnp.dot(a, wgbf_ref[...], preferred_element_type=F32)
    u = jnp.dot(a, wubf_ref[...], preferred_element_type=F32)
    o_ref[...] = (_silu(g) * u).astype(o_ref.dtype)


def _mm_residual_kernel(*refs, adaln_steps):
    if adaln_steps:
        a_ref, w_ref, x_ref, gate_ref, c_ref, wm_ref, bm_ref, o_ref, om_ref, wbf_ref = refs
    else:
        a_ref, w_ref, x_ref, gate_ref, o_ref, wbf_ref = refs
    _cast_weights_once((w_ref,), (wbf_ref,))
    y = jnp.dot(a_ref[...], wbf_ref[...], preferred_element_type=F32)
    o_ref[...] = x_ref[...] + gate_ref[...] * y
    if adaln_steps:
        @pl.when(pl.program_id(0) * pl.num_programs(1) + pl.program_id(1) < adaln_steps)
        def _():
            om_ref[...] = _adaln_tile(c_ref, wm_ref, bm_ref)


def _mm_residual_modulate_kernel(a_ref, w_ref, x_ref, gate_ref, g_ref, shift_ref, scale_ref,
                                 ox_ref, oh_ref, wbf_ref):
    @pl.when(pl.program_id(0) == 0)
    def _():
        wbf_ref[...] = w_ref[...].astype(BF16)
    y = jnp.dot(a_ref[...], wbf_ref[...], preferred_element_type=F32)
    x = x_ref[...] + gate_ref[...] * y
    ox_ref[...] = x
    h = x * lax.rsqrt(jnp.mean(x * x, axis=-1, keepdims=True) + EPS) * g_ref[...]
    oh_ref[...] = (h * (1.0 + scale_ref[...]) + shift_ref[...]).astype(BF16)


def _mm_residual_modulate_call(a, w, widx, x, mods, layer, gate_idx, gains, shift_idx, group_of_row,
                               row_quantum):
    m, k = a.shape
    n = w.shape[-1]
    depth = gains.shape[0]
    for tm in (512, 256, 128):
        need = k * n * (4 + 2) + 2 * tm * (k * 2 + n * (4 + 4 + 2)) + 2 * tm * n * 4
        if m % tm == 0 and row_quantum % tm == 0 and need <= VMEM_TILE_BUDGET:
            break
    else:
        raise ValueError(f"no row tile fits VMEM for {(m, k, n)}")
    grp = lambda i: group_of_row(i * tm)
    mod_spec = lambda which: pl.BlockSpec((None, None, 1, n), lambda i: (grp(i), which, 0, 0))
    row_spec = pl.BlockSpec((tm, n), lambda i: (i, 0))
    return pl.pallas_call(
        _mm_residual_modulate_kernel,
        grid=(m // tm,),
        in_specs=[
            pl.BlockSpec((tm, k), lambda i: (i, 0)),
            pl.BlockSpec((None, k, n), lambda i: (widx, 0, 0), pipeline_mode=pl.Buffered(1)),
            row_spec,
            mod_spec(gate_idx),
            pl.BlockSpec((None, 1, n), lambda i: (layer, 0, 0)),
            mod_spec(shift_idx),
            mod_spec(shift_idx + 1),
        ],
        out_specs=[row_spec, row_spec],
        out_shape=[jax.ShapeDtypeStruct((m, n), F32), jax.ShapeDtypeStruct((m, n), BF16)],
        scratch_shapes=[pltpu.VMEM((k, n), BF16)],
        compiler_params=_params("arbitrary"),
    )(a, w, x, mods, gains.reshape(depth, 1, n), mods, mods)


def _mm_plain_call(a, w, widx, out_dtype):
    m, k = a.shape
    n = w.shape[-1]
    tm, tn = _matmul_tiles(m, k, n, 1, jnp.dtype(out_dtype).itemsize, False, m)
    return pl.pallas_call(
        _mm_plain_kernel,
        grid=(n // tn, m // tm),
        in_specs=[
            pl.BlockSpec((tm, k), lambda j, i: (i, 0)),
            pl.BlockSpec((None, k, tn), lambda j, i: (widx, 0, j)),
        ],
        out_specs=pl.BlockSpec((tm, tn), lambda j, i: (i, j)),
        out_shape=jax.ShapeDtypeStruct((m, n), out_dtype),
        scratch_shapes=[pltpu.VMEM((k, tn), BF16)],
        compiler_params=_params("arbitrary", "arbitrary"),
    )(a, w)


def _mm_swiglu_call(a, wg, wu, widx):
    m, k = a.shape
    n = wg.shape[-1]
    tm, tn = _matmul_tiles(m, k, n, 2, 2, False, m)
    wspec = pl.BlockSpec((None, k, tn), lambda j, i: (widx, 0, j))
    return pl.pallas_call(
        _mm_swiglu_kernel,
        grid=(n // tn, m // tm),
        in_specs=[pl.BlockSpec((tm, k), lambda j, i: (i, 0)), wspec, wspec],
        out_specs=pl.BlockSpec((tm, tn), lambda j, i: (i, j)),
        out_shape=jax.ShapeDtypeStruct((m, n), BF16),
        scratch_shapes=[pltpu.VMEM((k, tn), BF16), pltpu.VMEM((k, tn), BF16)],
        compiler_params=_params("arbitrary", "arbitrary"),
    )(a, wg, wu)


ADALN_TILE = 256


def _mm_residual_call(a, w, widx, x, mods, gate_idx, group_of_row, row_quantum, row0=0, m=None,
                      adaln=None):
    k = a.shape[1]
    m = a.shape[0] if m is None else m
    n = w.shape[-1]
    tm, tn = _matmul_tiles(m, k, n, 1, 4, True, row_quantum)
    assert row0 % tm == 0
    rb0 = row0 // tm
    grp = lambda i: group_of_row((i + rb0) * tm)
    in_specs = [
        pl.BlockSpec((tm, k), lambda j, i: (i + rb0, 0)),
        pl.BlockSpec((None, k, tn), lambda j, i: (widx, 0, j)),
        pl.BlockSpec((tm, tn), lambda j, i: (i + rb0, j)),
        pl.BlockSpec((None, None, 1, tn), lambda j, i: (grp(i), gate_idx, 0, j)),
    ]
    args = [a, w, x, mods]
    out_specs = [pl.BlockSpec((tm, tn), lambda j, i: (i, j))]
    out_shape = [jax.ShapeDtypeStruct((m, n), F32)]
    adaln_steps = 0
    if adaln is not None:
        cond, w_mod, b_mod, layer = adaln
        depth, d, n_mod = w_mod.shape
        rows = cond.shape[0]
        adaln_steps = n_mod // ADALN_TILE
        per_j = m // tm
        assert n_mod % ADALN_TILE == 0 and adaln_steps <= (n // tn) * per_j
        tile = lambda j, i: jnp.minimum(j * per_j + i, adaln_steps - 1)
        in_specs += [pl.BlockSpec((rows, d), lambda j, i: (0, 0)),
                     pl.BlockSpec((None, d, ADALN_TILE), lambda j, i: (layer, 0, tile(j, i))),
                     pl.BlockSpec((None, 1, ADALN_TILE), lambda j, i: (layer, 0, tile(j, i)))]
        args += [cond, w_mod, b_mod.reshape(depth, 1, n_mod)]
        out_specs.append(pl.BlockSpec((rows, ADALN_TILE), lambda j, i: (0, tile(j, i))))
        out_shape.append(jax.ShapeDtypeStruct((rows, n_mod), F32))
    outs = pl.pallas_call(
        functools.partial(_mm_residual_kernel, adaln_steps=adaln_steps),
        grid=(n // tn, m // tm),
        in_specs=in_specs,
        out_specs=out_specs,
        out_shape=out_shape,
        scratch_shapes=[pltpu.VMEM((k, tn), BF16)],
        compiler_params=_params("arbitrary", "arbitrary"),
    )(*args)
    return outs if adaln is not None else outs[0]


def _bcast_row_in_blocks(x, block, r):
    c, d = x.shape
    x3 = x.reshape(c // block, block, d)
    return jnp.broadcast_to(x3[:, r:r + 1, :], x3.shape).reshape(c, d)


MXU_WIDTH = 256


def _gla_blocks(c):
    return [1 << b for b in range((c // 2).bit_length())] + [0]


def _gla_masks(c, fwd):
    i = lax.broadcasted_iota(jnp.int32, (c, LANES), 0)
    lane = lax.broadcasted_iota(jnp.int32, (c, LANES), 1)
    masks = []
    for b, h in enumerate(_gla_blocks(c)):
        j = lane - (b * c) % LANES
        in_block = jnp.where(j >= 0, jnp.where(j < c, 1.0, 0.0), 0.0)
        if h == 0:
            ok = jnp.where(i == j, 1.0, 0.0)
        else:
            same = (i // (2 * h)) == (j // (2 * h))
            i_hi = (i % (2 * h)) >= h
            j_hi = (j % (2 * h)) >= h
            if fwd:
                ok = jnp.where(same, jnp.where(i_hi, jnp.where(j_hi, 0.0, 1.0), 0.0), 0.0)
            else:
                ok = jnp.where(same, jnp.where(i_hi, 0.0, jnp.where(j_hi, 1.0, 0.0)), 0.0)
        masks.append((ok * in_block).astype(F32))
    return masks


def _split_roles(q, kg, h, fwd, rloc):
    c = q.shape[0]
    if h >= SUBLANES:
        parts = []
        for b in range(c // (2 * h)):
            lo = slice(2 * h * b, 2 * h * b + h)
            hi = slice(2 * h * b + h, 2 * h * (b + 1))
            parts += [kg[lo], q[hi]] if fwd else [q[lo], kg[hi]]
        return jnp.concatenate(parts, axis=0)
    in_hi = (rloc % (2 * h)) >= h
    return jnp.where(in_hi, q, kg) if fwd else jnp.where(in_hi, kg, q)


def _split3(x):
    hi = x.astype(BF16)
    r1 = x - hi.astype(F32)
    mid = r1.astype(BF16)
    lo = (r1 - mid.astype(F32)).astype(BF16)
    return jnp.concatenate([hi, mid, lo], axis=-1)


def _gla_chunks(chains, rloc):
    c, d = chains[0][0].shape
    per_group = MXU_WIDTH // c
    blocks = _gla_blocks(c)
    n_groups = -(-len(blocks) // per_group)
    n_kvregs = -(-len(blocks) * c // LANES)

    es = [None] * len(chains)
    for direction in (True, False):
        ids = [n for n, ch in enumerate(chains) if ch[6] == direction]
        if ids:
            e3 = jnp.dot(chains[ids[0]][7], jnp.concatenate([chains[n][4] for n in ids], axis=1),
                         preferred_element_type=F32)
            for pos, n in enumerate(ids):
                base = 3 * d * pos
                es[n] = e3[:, base:base + d] + e3[:, base + d:base + 2 * d] + e3[:, base + 2 * d:base + 3 * d]

    sides = []
    for (q, kg, fg, v, logf, st, fwd, tri_bf, masks), e in zip(chains, es):
        qs, ks = [], []
        for h in blocks:
            if h == 0:
                qs.append(q.astype(BF16))
                ks.append(kg.astype(BF16))
                continue
            if h == 1:
                in_hi = (rloc % 2) >= 1
                w = jnp.where(in_hi, q * fg, kg) if fwd else jnp.where(in_hi, kg, q * fg)
            else:
                r = h - 1 if fwd else h
                if 2 * h >= SUBLANES:
                    e_ref = _bcast_row_in_blocks(e, 2 * h, r)
                else:
                    e_ref = jnp.where(rloc < 2 * h, _bcast_row_in_blocks(e, SUBLANES, r),
                                      _bcast_row_in_blocks(e, SUBLANES, 2 * h + r))
                w = _split_roles(q, kg, h, fwd, rloc) * (1.0 / jnp.exp2(jnp.abs(e - e_ref)))
            qs.append(w.astype(BF16))
            ks.append(qs[-1])
        pad = [jnp.zeros((c, d), BF16)] * (n_groups * per_group - len(blocks))
        sides.append((qs + pad, ks + pad))

    atts = [[None] * n_kvregs for _ in chains]
    for g in range(n_groups):
        for ci, (qs, ks) in enumerate(sides):
            sl = slice(g * per_group, (g + 1) * per_group)
            prod = lax.dot_general(jnp.concatenate(qs[sl], axis=0), jnp.concatenate(ks[sl], axis=0),
                                   NT_DIMS, preferred_element_type=F32)
            masks = chains[ci][8]
            for b in range(g * per_group, min((g + 1) * per_group, len(blocks))):
                t = b * c // LANES
                col = (t * LANES) % MXU_WIDTH
                row = (b - g * per_group) * c
                term = prod[row:row + c, col:col + LANES] * masks[b]
                atts[ci][t] = term if atts[ci][t] is None else atts[ci][t] + term

    outs = []
    for (q, kg, fg, v, logf, st, fwd, tri_bf, masks), e, att in zip(chains, es, atts):
        vb = v.astype(BF16)
        att_sum = functools.reduce(lambda x, y: x + y, att)
        v_rep = jnp.concatenate([vb] * (LANES // c), axis=0)
        o = jnp.dot(att_sum.astype(BF16), v_rep, preferred_element_type=F32)
        o = o + lax.dot_general((q * jnp.exp2(e)).astype(BF16), st.astype(BF16), NT_DIMS,
                                preferred_element_type=F32)
        e_edge = e[c - 1:c] if fwd else e[0:1]
        kh = (kg * jnp.exp2(e_edge - e)).astype(BF16)
        st_new = st * jnp.exp2(e_edge) + lax.dot_general(vb, kh, TN_DIMS, preferred_element_type=F32)
        outs.append((o, st_new))
    return outs


def _gla_kernel(*refs, seq_len, chunk, has_s0, final):
    refs = list(refs)
    q_ref, ffw_ref, fbw_ref, v_ref, g_ref, lb_ref, on_ref = refs[:7]
    del refs[:7]
    s0_ref = refs.pop(0) if has_s0 else None
    del refs[:2 if final == "join" else 1]
    o_ref = refs.pop(0)
    sf_ref = _final_state_slot(refs.pop(0), final) if final is not None else None
    stf_ref, stb_ref, oacc_ref = refs
    c = chunk
    nch = seq_len // c
    half_n = nch // 2
    d = A_HEAD_DIM
    n_h = q_ref.shape[-1] // d

    i = lax.broadcasted_iota(jnp.int32, (c, c), 0)
    j = lax.broadcasted_iota(jnp.int32, (c, c), 1)
    tri_f = jnp.where(i >= j, 1.0, 0.0).astype(BF16)
    tri_b = jnp.where(i <= j, 1.0, 0.0).astype(BF16)
    masks_f = _gla_masks(c, True)
    masks_b = _gla_masks(c, False)
    rloc = lax.broadcasted_iota(jnp.int32, (c, d), 0) % SUBLANES
    q_scale = d ** -0.5

    def gates(x, lb):
        one_m_lb = 1.0 - lb
        t = jnp.exp(-jnp.abs(x))
        r = 1.0 / (1.0 + t)
        pos = x >= 0.0
        fg = lb + one_m_lb * jnp.where(pos, r, t * r)
        kg = one_m_lb * jnp.where(pos, t * r, r)
        log2f = jnp.maximum(jnp.log2(fg),
                            jnp.log2(one_m_lb) + jnp.minimum(x, 0.0) * LOG2_E - jnp.log2(1.0 + t))
        return fg, kg, log2f

    def chain_rows(t, fwd):
        return pl.ds(pl.multiple_of((t if fwd else nch - 1 - t) * c, c), c)

    chain_ids = [(hh, fwd) for hh in range(n_h) for fwd in (True, False)]

    def all_chains(t):
        where, chains = [], []
        for hh, fwd in chain_ids:
            sl, cols = chain_rows(t, fwd), slice(hh * d, (hh + 1) * d)
            fg, kg, log2f = gates((ffw_ref if fwd else fbw_ref)[sl, cols], lb_ref[:, cols])
            where.append((sl, cols))
            chains.append((_silu(q_ref[sl, cols]) * q_scale, kg, fg, v_ref[sl, cols], _split3(log2f),
                           (stf_ref if fwd else stb_ref)[hh], fwd,
                           tri_f if fwd else tri_b, masks_f if fwd else masks_b))
        outs = _gla_chunks(chains, rloc)
        for n, (o, st) in enumerate(outs):
            (stf_ref if n % 2 == 0 else stb_ref)[n // 2] = st
        return [(sl, cols, o) for (sl, cols), (o, st) in zip(where, outs)]

    def finish(sl, cols, o):
        y = oacc_ref[sl, cols] + o
        y = y * lax.rsqrt(jnp.mean(y * y, axis=-1, keepdims=True) + EPS) * on_ref[...]
        o_ref[sl, cols] = (y * _silu(g_ref[sl, cols])).astype(o_ref.dtype)

    def first_half(t, carry):
        for sl, cols, o in all_chains(t):
            oacc_ref[sl, cols] = o
        return carry

    def second_half(t, carry):
        for sl, cols, o in all_chains(t):
            finish(sl, cols, o)
        return carry

    for hh in range(n_h):
        for st_ref, direction in ((stf_ref, 0), (stb_ref, 1)):
            st_ref[hh] = s0_ref[direction, hh].T if has_s0 else jnp.zeros((d, d), F32)
    unroll = 2 if 2 * n_h < 8 and half_n % 2 == 0 else 1
    lax.fori_loop(0, half_n, first_half, 0, unroll=unroll)
    lax.fori_loop(half_n, nch, second_half, 0, unroll=unroll)
    if final is not None:
        for hh in range(n_h):
            sf_ref[0, hh] = stf_ref[hh].T
            sf_ref[1, hh] = stb_ref[hh].T


GLA_CHUNK = 128


def _join_buffers(bufs, in_specs, args):
    shapes, aliases = [], {}
    for n, buf in enumerate(bufs):
        in_specs.append(pl.BlockSpec(memory_space=pl.ANY))
        args.append(buf)
        shapes.append(jax.ShapeDtypeStruct(buf.shape, buf.dtype))
        aliases[len(args) - 1] = n
    return shapes, aliases


def _final_state_specs(finals, slot, dims, heads_per_step, out_specs):
    n_seq, heads, d = dims
    if finals is None:
        return [], None
    if isinstance(finals, int):
        out_specs.append(pl.BlockSpec((None, finals, 2, heads_per_step, d, d),
                                      lambda n, h: (n, 0, 0, h, 0, 0)))
        return [], (slot, finals)
    out_specs.append(pl.BlockSpec((None, None, 2, heads_per_step, d, d),
                                  lambda n, h: (n, slot, 0, h, 0, 0)))
    return [finals], "join"


def _final_state_slot(sf_ref, final):
    if final == "join":
        return sf_ref
    slot, n_slots = final
    for s in range(n_slots):
        if s != slot:
            sf_ref[s] = jnp.zeros(sf_ref.shape[1:], sf_ref.dtype)
    return sf_ref.at[slot]


def _gla_heads_per_step(seq_len):
    for n_h in (4, 2, 1):
        if 2 * 5 * seq_len * n_h * A_HEAD_DIM * 4 + 4 * seq_len * n_h * A_HEAD_DIM * 2 <= VMEM_TILE_BUDGET:
            return n_h
    raise ValueError(f"sequence of {seq_len} rows does not fit VMEM")


def _gla_call(p, lower_bounds, layer, onorm, eidx, s0, n_seq, seq_len, row0, finals, mix):
    d = A_HEAD_DIM
    n_h = _gla_heads_per_step(seq_len)
    w = n_h * d
    a_width = lower_bounds.shape[-1]
    n_hb = a_width // w
    assert row0 % seq_len == 0 and seq_len % (2 * GLA_CHUNK) == 0 and a_width % w == 0
    sb = row0 // seq_len
    slab = lambda k: pl.BlockSpec((seq_len, w), lambda n, h: (n + sb, k * n_hb + h))
    in_specs = [slab(0), slab(1), slab(2), slab(3), slab(4),
                pl.BlockSpec((None, 1, w), lambda n, h: (layer, 0, h)),
                pl.BlockSpec((None, 1, d), lambda n, h: (eidx, 0, 0))]
    args = [p, p, p, p, p, lower_bounds.reshape(lower_bounds.shape[0], 1, a_width),
            onorm.reshape(onorm.shape[0], 1, d)]
    if s0 is not None:
        in_specs.append(pl.BlockSpec((None, None, 2, n_h, d, d), lambda n, h: (n, eidx, 0, h, 0, 0)))
        args.append(s0)
    out_specs = [pl.BlockSpec((seq_len, w), lambda n, h: (n + sb, h))]
    joined, final = _final_state_specs(finals, eidx, (n_seq, a_width // d, d), n_h, out_specs)
    out_shape, aliases = _join_buffers([mix] + joined, in_specs, args)
    if isinstance(final, tuple):
        out_shape.append(jax.ShapeDtypeStruct((n_seq, final[1], 2, a_width // d, d, d), F32))
    kern = functools.partial(_gla_kernel, seq_len=seq_len, chunk=GLA_CHUNK,
                             has_s0=s0 is not None, final=final)
    return pl.pallas_call(
        kern,
        grid=(n_seq, n_hb),
        in_specs=in_specs,
        out_specs=out_specs,
        out_shape=out_shape,
        input_output_aliases=aliases,
        scratch_shapes=[pltpu.VMEM((n_h, d, d), F32), pltpu.VMEM((n_h, d, d), F32),
                        pltpu.VMEM((seq_len, w), F32)],
        compiler_params=_params("arbitrary", "arbitrary"),
    )(*args)


def _ret_kernel(*refs, seq_len, chunk, has_s0, final, rope):
    refs = list(refs)
    q_ref, k_ref, v_ref, g_ref, dl_ref, gn_ref = refs[:6]
    del refs[:6]
    cos_ref = refs.pop(0) if rope else None
    sin_ref = refs.pop(0) if rope else None
    s0_ref = refs.pop(0) if has_s0 else None
    del refs[:2 if final == "join" else 1]
    o_ref = refs.pop(0)
    sf_ref = _final_state_slot(refs.pop(0), final) if final is not None else None
    stf_ref, stb_ref, oacc_ref = refs
    c = chunk
    nch = seq_len // c
    half_n = nch // 2
    d = dl_ref.shape[-1]
    n_h = q_ref.shape[-1] // d
    half = d // 2
    k_scale = d ** -0.5

    dist = (lax.broadcasted_iota(jnp.int32, (c, c), 0)
            - lax.broadcasted_iota(jnp.int32, (c, c), 1)).astype(F32)
    low = dist >= 0.0
    up = dist <= 0.0
    row = lax.broadcasted_iota(jnp.int32, (c, d), 0).astype(F32)

    def decay_tables(hh):
        lg_f = _log_sigmoid(dl_ref[0, hh])
        lg_b = _log_sigmoid(dl_ref[1, hh])
        intra = (jnp.where(low, jnp.exp(jnp.where(low, dist, 0.0) * lg_f[:, :c]), 0.0)
                 + jnp.where(up, jnp.exp(jnp.where(up, -dist, 0.0) * lg_b[:, :c]), 0.0))
        fwd = (jnp.exp((row + 1.0) * lg_f), jnp.exp((c - 1.0 - row) * lg_f), jnp.exp(c * lg_f))
        bwd = (jnp.exp((c - row) * lg_b), jnp.exp(row * lg_b), jnp.exp(c * lg_b))
        return intra, fwd, bwd

    tables = [decay_tables(hh) for hh in range(n_h)]

    def rotate(x, sl):
        if not rope:
            return x
        cos, sin = cos_ref[sl, :], sin_ref[sl, :]
        x1, x2 = x[:, :half], x[:, half:]
        return jnp.concatenate([x1 * cos - x2 * sin, x1 * sin + x2 * cos], axis=-1)

    def load(ci, hh):
        sl, cols = pl.ds(pl.multiple_of(ci * c, c), c), slice(hh * d, (hh + 1) * d)
        return (sl, cols, rotate(q_ref[sl, cols], sl), rotate(k_ref[sl, cols] * k_scale, sl),
                v_ref[sl, cols].astype(BF16))

    def carried(q, k, vb, st_ref, hh, decs):
        q_dec, k_dec, c_dec = decs
        st = st_ref[hh]
        o = jnp.dot((q * q_dec).astype(BF16), st.astype(BF16), preferred_element_type=F32)
        st_ref[hh] = c_dec * st + lax.dot_general((k * k_dec).astype(BF16), vb, TN_DIMS,
                                                  preferred_element_type=F32)
        return o

    def trip(t):
        left = [load(t, hh) for hh in range(n_h)]
        right = [load(nch - 1 - t, hh) for hh in range(n_h)]
        scores = [lax.dot_general(q.astype(BF16), k.astype(BF16), NT_DIMS, preferred_element_type=F32)
                  for _, _, q, k, _ in left]
        o_left = [carried(q, k, vb, stf_ref, hh, tables[hh][1]) for hh, (_, _, q, k, vb) in enumerate(left)]
        o_right = [carried(q, k, vb, stb_ref, hh, tables[hh][2]) for hh, (_, _, q, k, vb) in enumerate(right)]
        for hh, (_, _, _, _, vb) in enumerate(left):
            o_left[hh] = o_left[hh] + jnp.dot((scores[hh] * tables[hh][0]).astype(BF16), vb,
                                              preferred_element_type=F32)
        return ([(sl, cols, o) for (sl, cols, *_), o in zip(left, o_left)]
                + [(sl, cols, o) for (sl, cols, *_), o in zip(right, o_right)])

    def finish(sl, cols, o):
        y = oacc_ref[sl, cols] + o
        yc = y - jnp.mean(y, axis=-1, keepdims=True)
        yn = yc * lax.rsqrt(jnp.mean(yc * yc, axis=-1, keepdims=True) + EPS)
        o_ref[sl, cols] = (yn * gn_ref[:, cols] * _silu(g_ref[sl, cols])).astype(o_ref.dtype)

    def first_half(t, carry):
        for sl, cols, o in trip(t):
            oacc_ref[sl, cols] = o
        return carry

    def second_half(t, carry):
        for sl, cols, o in trip(t):
            finish(sl, cols, o)
        return carry

    for hh in range(n_h):
        for st_ref, direction in ((stf_ref, 0), (stb_ref, 1)):
            st_ref[hh] = s0_ref[direction, hh] if has_s0 else jnp.zeros((d, d), F32)
    lax.fori_loop(0, half_n, first_half, 0)
    lax.fori_loop(half_n, nch, second_half, 0)
    if final is not None:
        for hh in range(n_h):
            sf_ref[0, hh] = stf_ref[hh]
            sf_ref[1, hh] = stb_ref[hh]


def _ret_call(p, col0, decay_logit, gnorm, eidx, rope_tabs, s0, n_seq, seq_len, row0, finals,
              mix, mix_col0):
    n_heads = decay_logit.shape[-1]
    b_width = gnorm.shape[-1]
    d = b_width // n_heads
    chunk = min(d, seq_len // 2)
    n_h = 2 if n_heads % 2 == 0 else 1
    w = n_h * d
    n_hb = n_heads // n_h
    assert row0 % seq_len == 0 and seq_len % (2 * chunk) == 0 and col0 % w == 0 and chunk <= d
    assert mix_col0 % w == 0
    sb = row0 // seq_len
    cb0 = col0 // w
    ocb0 = mix_col0 // w
    slab = lambda k: pl.BlockSpec((seq_len, w), lambda n, h: (n + sb, cb0 + k * n_hb + h))
    dl = jnp.broadcast_to(decay_logit[:, :, :, None, None], decay_logit.shape + (1, d))
    in_specs = [slab(0), slab(1), slab(2), slab(3),
                pl.BlockSpec((None, 2, n_h, 1, d), lambda n, h: (eidx, 0, h, 0, 0)),
                pl.BlockSpec((None, 1, w), lambda n, h: (eidx, 0, h))]
    args = [p, p, p, p, dl, gnorm.reshape(gnorm.shape[0], 1, b_width)]
    if rope_tabs is not None:
        tab = pl.BlockSpec((seq_len, d // 2), lambda n, h: (0, 0))
        in_specs += [tab, tab]
        args += list(rope_tabs)
    if s0 is not None:
        in_specs.append(pl.BlockSpec((None, None, 2, n_h, d, d), lambda n, h: (n, eidx, 0, h, 0, 0)))
        args.append(s0)
    out_specs = [pl.BlockSpec((seq_len, w), lambda n, h: (n + sb, ocb0 + h))]
    joined, final = _final_state_specs(finals, eidx, (n_seq, n_heads, d), n_h, out_specs)
    out_shape, aliases = _join_buffers([mix] + joined, in_specs, args)
    if isinstance(final, tuple):
        out_shape.append(jax.ShapeDtypeStruct((n_seq, final[1], 2, n_heads, d, d), F32))
    kern = functools.partial(_ret_kernel, seq_len=seq_len, chunk=chunk, has_s0=s0 is not None,
                             final=final, rope=rope_tabs is not None)
    return pl.pallas_call(
        kern,
        grid=(n_seq, n_hb),
        in_specs=in_specs,
        out_specs=out_specs,
        out_shape=out_shape,
        input_output_aliases=aliases,
        scratch_shapes=[pltpu.VMEM((n_h, d, d), F32), pltpu.VMEM((n_h, d, d), F32),
                        pltpu.VMEM((seq_len, w), F32)],
        compiler_params=_params("arbitrary", "arbitrary"),
    )(*args)


Q_PRESCALE = ATTN_HEAD_DIM ** -0.5 * LOG2_E


def _head_norm(x, gain):
    return x * lax.rsqrt(jnp.mean(x * x, axis=-1, keepdims=True) + EPS) * gain


def _rotate(y, cos, sin):
    return y * cos + pltpu.roll(y, y.shape[-1] // 2, 1) * sin


def _mm_qproj_kernel(a_ref, w_ref, gain_ref, cos_ref, sin_ref, o_ref, wbf_ref):
    _cast_weights_once((w_ref,), (wbf_ref,))
    y = jnp.dot(a_ref[...], wbf_ref[...], preferred_element_type=F32)
    hd = ATTN_HEAD_DIM
    for h in range(y.shape[-1] // hd):
        cols = slice(h * hd, (h + 1) * hd)
        z = _rotate(_head_norm(y[:, cols], gain_ref[...]), cos_ref[...], sin_ref[...])
        o_ref[:, cols] = (z * Q_PRESCALE).astype(BF16)


def _mm_qproj_call(a, w, widx, n_cols, q_norm, rope_rows):
    m, k = a.shape
    hd = ATTN_HEAD_DIM
    tm, tn = _matmul_tiles(m, k, n_cols, 1, 2, False, m)
    tab = pl.BlockSpec((tm, hd), lambda j, i: (i, 0))
    return pl.pallas_call(
        _mm_qproj_kernel,
        grid=(n_cols // tn, m // tm),
        in_specs=[
            pl.BlockSpec((tm, k), lambda j, i: (i, 0)),
            pl.BlockSpec((None, k, tn), lambda j, i: (widx, 0, j)),
            pl.BlockSpec((None, 1, hd), lambda j, i: (widx, 0, 0)),
            tab, tab,
        ],
        out_specs=pl.BlockSpec((tm, tn), lambda j, i: (i, j)),
        out_shape=jax.ShapeDtypeStruct((m, n_cols), BF16),
        scratch_shapes=[pltpu.VMEM((k, tn), BF16)],
        compiler_params=_params("arbitrary", "arbitrary"),
    )(a, w, q_norm.reshape(-1, 1, hd), *rope_rows)


def _mm_kvproj_kernel(*refs, cache, cache_tiles):
    refs = list(refs)
    a_ref, w_ref, gain_ref, cos_ref, sin_ref = refs[:5]
    del refs[:7 if cache == "join" else 5]
    kf_out, vf_out, k_out, v_out, wbf_ref = refs
    _cast_weights_once((w_ref,), (wbf_ref,))
    y = jnp.dot(a_ref[...], wbf_ref[...], preferred_element_type=F32)
    hd = ATTN_HEAD_DIM
    kvw = y.shape[-1] // 2
    keys = []
    for h in range(kvw // hd):
        cols = slice(h * hd, (h + 1) * hd)
        keys.append(_rotate(_head_norm(y[:, cols], gain_ref[...]), cos_ref[...], sin_ref[...]))
        k_out[:, cols] = keys[-1].astype(BF16)
    v = y[:, kvw:]
    v_out[...] = v.astype(BF16)

    @pl.when(pl.program_id(1) < cache_tiles)
    def _():
        kf, vf = kf_out, vf_out
        if cache != "join":
            slot, n_slots = cache
            for s in range(n_slots):
                if s != slot:
                    kf_out[:, s] = jnp.zeros(kf_out.shape[:1] + kf_out.shape[2:], F32)
                    vf_out[:, s] = jnp.zeros(vf_out.shape[:1] + vf_out.shape[2:], F32)
            kf, vf = kf_out.at[:, slot], vf_out.at[:, slot]
        seqs, seq_len = kf.shape[0], kf.shape[1]
        for h, z in enumerate(keys):
            kf[:, :, h * hd:(h + 1) * hd] = z.reshape(seqs, seq_len, hd)
        vf[...] = v.reshape(seqs, seq_len, kvw)


def _mm_kvproj_call(a, w, widx, col0, k_norm, rope_rows, cache_seqs, cache_seq_len, caches):
    m, k = a.shape
    hd = ATTN_HEAD_DIM
    kvw = ATTN_KV_HEADS * hd
    tn = 2 * kvw
    assert col0 % tn == 0
    tm, _ = _matmul_tiles(m, k, tn, 1, 2 + 4, False, m)
    assert tm % cache_seq_len == 0 and (cache_seqs * cache_seq_len) % tm == 0
    seqs_per_tile = tm // cache_seq_len
    cache_tiles = cache_seqs * cache_seq_len // tm
    tab = pl.BlockSpec((tm, hd), lambda j, i: (i, 0))
    row_spec = pl.BlockSpec((tm, kvw), lambda j, i: (i, 0))
    in_specs = [pl.BlockSpec((tm, k), lambda j, i: (i, 0)),
                pl.BlockSpec((None, k, tn), lambda j, i: (widx, 0, col0 // tn)),
                pl.BlockSpec((None, 1, hd), lambda j, i: (widx, 0, 0)),
                tab, tab]
    args = [a, w, k_norm.reshape(-1, 1, hd), *rope_rows]
    held = lambda i: jnp.minimum(i, cache_tiles - 1)
    if isinstance(caches, int):
        cache = (widx, caches)
        cache_spec = pl.BlockSpec((seqs_per_tile, caches, cache_seq_len, kvw),
                                  lambda j, i: (held(i), 0, 0, 0))
        out_shape = [jax.ShapeDtypeStruct((cache_seqs, caches, cache_seq_len, kvw), F32)] * 2
        aliases = {}
    else:
        cache = "join"
        cache_spec = pl.BlockSpec((seqs_per_tile, None, cache_seq_len, kvw),
                                  lambda j, i: (held(i), widx, 0, 0))
        out_shape, aliases = _join_buffers(list(caches), in_specs, args)
    return pl.pallas_call(
        functools.partial(_mm_kvproj_kernel, cache=cache, cache_tiles=cache_tiles),
        grid=(1, m // tm),
        in_specs=in_specs,
        out_specs=[cache_spec, cache_spec, row_spec, row_spec],
        out_shape=out_shape + [jax.ShapeDtypeStruct((m, kvw), BF16)] * 2,
        input_output_aliases=aliases,
        scratch_shapes=[pltpu.VMEM((k, tn), BF16)],
        compiler_params=_params("arbitrary", "arbitrary"),
    )(*args)


def _attn_kernel(*refs, group):
    q_ref, k_ref, v_ref, _, o_ref = refs
    hd = ATTN_HEAD_DIM
    for kh in range(k_ref.shape[-1] // hd):
        k = k_ref[:, kh * hd:(kh + 1) * hd]
        v = v_ref[:, kh * hd:(kh + 1) * hd]
        v_ext = jnp.concatenate([v, jnp.ones_like(v)], axis=1)
        for g in range(group):
            cols = slice((kh * group + g) * hd, (kh * group + g + 1) * hd)
            s = lax.dot_general(q_ref[:, cols], k, NT_DIMS, preferred_element_type=F32)
            p = jnp.exp2(s - jnp.max(s, axis=-1, keepdims=True))
            o = jnp.dot(p.astype(BF16), v_ext, preferred_element_type=F32)
            o_ref[:, cols] = (o[:, :hd] / o[:, hd:]).astype(o_ref.dtype)


ATTN_SCORE_BUDGET = 4 * 1024 * 1024


ATTN_Q_ROWS = 1024


def _attn_call(q, k, v, n_seq, seq_len, row0, mix):
    kvw = k.shape[-1]
    qw = q.shape[-1]
    hd = ATTN_HEAD_DIM
    n_kv = kvw // hd
    group = qw // hd // n_kv
    tq = min(ATTN_Q_ROWS, seq_len)
    assert row0 % tq == 0 and seq_len % tq == 0 and row0 % seq_len == 0
    per_seq = seq_len // tq
    rb0 = row0 // tq
    lk = seq_len if k.ndim == 2 else k.shape[1]
    kv_step = n_kv if tq * lk * 4 * n_kv <= ATTN_SCORE_BUDGET else 1
    q_spec = pl.BlockSpec((tq, kv_step * group * hd), lambda n, kh, i: (rb0 + n * per_seq + i, kh))
    if k.ndim == 2:
        kv_spec = pl.BlockSpec((seq_len, kv_step * hd), lambda n, kh, i: (row0 // seq_len + n, kh))
    else:
        kv_spec = pl.BlockSpec((None, lk, kv_step * hd), lambda n, kh, i: (n, 0, kh))
    in_specs = [q_spec, kv_spec, kv_spec]
    args = [q, k, v]
    (mix_shape,), aliases = _join_buffers([mix], in_specs, args)
    return pl.pallas_call(
        functools.partial(_attn_kernel, group=group),
        grid=(n_seq, n_kv // kv_step, per_seq),
        in_specs=in_specs,
        out_specs=q_spec,
        out_shape=mix_shape,
        input_output_aliases=aliases,
        compiler_params=_params("arbitrary", "arbitrary", "arbitrary"),
    )(*args)


def _grid_rope(n_tokens, head_dim):
    rows = n_tokens // GRID_W
    row = jnp.broadcast_to(jnp.arange(rows)[:, None], (rows, GRID_W)).reshape(-1).astype(F32)
    col = jnp.broadcast_to(jnp.arange(GRID_W)[None, :], (rows, GRID_W)).reshape(-1).astype(F32)
    per_axis = head_dim // 4
    freqs = ROPE_THETA ** (-jnp.arange(per_axis, dtype=F32) / per_axis)
    ang = jnp.concatenate([row[:, None] * freqs, col[:, None] * freqs], axis=-1)
    return jnp.cos(ang), jnp.sin(ang)


def kernel(x_prompt, x_sample, c, state_hgrn, state_ret, cache_k, cache_v, c_ctx, w_mod, b_mod, norm_mix, norm_ffn, w_in_even, w_out_even, hgrn_lb_logits, hgrn_onorm, ret_decay_logit, ret_gnorm, w_in_attn, w_out_attn, q_norm, k_norm, w_ffn_gate, w_ffn_up, w_ffn_down):
    n_p, l_p, d = x_prompt.shape
    n_s, l_s, _ = x_sample.shape
    depth = w_mod.shape[0]
    rows_p = n_p * l_p
    rows_s = n_s * l_s
    a_width = hgrn_lb_logits.shape[-1]
    b_width = ret_gnorm.shape[-1]
    kv_width = ATTN_KV_HEADS * ATTN_HEAD_DIM
    row_quantum = math.gcd(rows_p, l_s)

    def group_of_row(r):
        return jnp.where(r < rows_p, 0, 1 + (r - rows_p) // l_s)

    cond = jnp.concatenate([c_ctx[None, :], c, jnp.zeros((-(1 + n_s) % SUBLANES, d), F32)], axis=0)
    as_mods = lambda m: m.reshape(cond.shape[0], 6, 1, d)
    mods = as_mods(_adaln_call(cond, w_mod, b_mod, 0))

    cos_a, sin_a = _grid_rope(l_s, ATTN_HEAD_DIM)
    rope_attn = (jnp.concatenate([cos_a, cos_a], axis=-1), jnp.concatenate([-sin_a, sin_a], axis=-1))
    rope_rows = (jnp.concatenate([jnp.ones((rows_p, ATTN_HEAD_DIM), F32), jnp.tile(rope_attn[0], (n_s, 1))]),
                 jnp.concatenate([jnp.zeros((rows_p, ATTN_HEAD_DIM), F32), jnp.tile(rope_attn[1], (n_s, 1))]))
    rope_ret = _grid_rope(l_s, b_width // B_HEADS)
    lb_cum = jnp.cumsum(jax.nn.softmax(hgrn_lb_logits.astype(F32), axis=0), axis=0)
    lower_bounds = lb_cum - lb_cum[0]

    n_even = w_in_even.shape[0]
    out_hgrn, out_ret = n_even, n_even
    out_kv = w_in_attn.shape[0]
    for l in range(depth):
        if l == 0:
            mix, x = _modulate_join_call(x_prompt.reshape(rows_p, d), x_sample.reshape(rows_s, d),
                                         norm_mix, mods, l, 0, group_of_row)
        else:
            mix = _modulate_call(x, norm_mix, mods, l, 0, group_of_row)
        if l % 2 == 0:
            e = l // 2
            p = _mm_plain_call(mix, w_in_even, e, F32)
            mix, out_hgrn = _gla_call(p, lower_bounds, l, hgrn_onorm, e, None, n_p, l_p, 0, out_hgrn, mix)
            (mix,) = _gla_call(p, lower_bounds, l, hgrn_onorm, e, state_hgrn, n_s, l_s, rows_p, None, mix)
            mix, out_ret = _ret_call(p, 5 * a_width, ret_decay_logit, ret_gnorm, e, None, None,
                                     n_p, l_p, 0, out_ret, mix, a_width)
            (mix,) = _ret_call(p, 5 * a_width, ret_decay_logit, ret_gnorm, e, rope_ret, state_ret,
                               n_s, l_s, rows_p, None, mix, a_width)
            w_out, widx = w_out_even, e
        else:
            o = l // 2
            q = _mm_qproj_call(mix, w_in_attn, o, d, q_norm, rope_rows)
            *out_kv, k, v = _mm_kvproj_call(mix, w_in_attn, o, d, k_norm, rope_rows, n_p, l_p, out_kv)
            past = cache_k.shape[2]
            with_cache = lambda new, cache: jnp.concatenate(
                [new[rows_p:].reshape(n_s, l_s, kv_width),
                 cache[:, o].reshape(n_s, past, kv_width).astype(BF16)], axis=1)
            k_all, v_all = with_cache(k, cache_k), with_cache(v, cache_v)
            mix = _attn_call(q, k, v, n_p, l_p, 0, mix)
            mix = _attn_call(q, k_all, v_all, n_s, l_s, rows_p, mix)
            w_out, widx = w_out_attn, o
        x, h = _mm_residual_modulate_call(mix, w_out, widx, x, mods, l, 2, norm_ffn, 3, group_of_row,
                                          row_quantum)
        u = _mm_swiglu_call(h, w_ffn_gate, w_ffn_up, l)
        down = functools.partial(_mm_residual_call, u, w_ffn_down, l, x, mods, 5, group_of_row,
                                 row_quantum)
        if l < depth - 1:
            x, next_mods = down(adaln=(cond, w_mod, b_mod, l + 1))
            mods = as_mods(next_mods)
    return (down(0, rows_p).reshape(n_p, l_p, d), down(rows_p, rows_s).reshape(n_s, l_s, d),
            out_hgrn, out_ret,
            *(t.reshape(n_p, -1, l_p, ATTN_KV_HEADS, ATTN_HEAD_DIM) for t in out_kv))
```

```python
import functools
import math

import jax
import jax.numpy as jnp
from jax import lax
from jax.experimental import pallas as pl
from jax.experimental.pallas import tpu as pltpu

F32 = jnp.float32
BF16 = jnp.bfloat16

EPS = 1e-6
ROPE_THETA = 10000.0
GRID_W = 64
A_HEAD_DIM = 128
B_HEADS = 4
ATTN_HEAD_DIM = 128
ATTN_KV_HEADS = 4
LANES = 128
SUBLANES = 8
VMEM_LIMIT = 56 * 1024 * 1024
VMEM_TILE_BUDGET = 50 * 1024 * 1024

LOG2_E = math.log2(math.e)

NT_DIMS = (((1,), (1,)), ((), ()))
TN_DIMS = (((0,), (0,)), ((), ()))


def _params(*sem):
    return pltpu.CompilerParams(dimension_semantics=sem, vmem_limit_bytes=VMEM_LIMIT)


def _silu(x):
    return x / (1.0 + jnp.exp(-x))


def _log_sigmoid(x):
    return jnp.minimum(x, 0.0) - jnp.log1p(jnp.exp(-jnp.abs(x)))


def _adaln_tile(c_ref, w_ref, b_ref):
    s = _silu(c_ref[...]).astype(BF16)
    return jnp.dot(s, w_ref[...].astype(BF16), preferred_element_type=F32) + b_ref[...]


def _adaln_kernel(c_ref, w_ref, b_ref, o_ref):
    o_ref[...] = _adaln_tile(c_ref, w_ref, b_ref)


def _adaln_call(cond, w_mod, b_mod, layer, tn=1024):
    depth, d, n = w_mod.shape
    rows = cond.shape[0]
    return pl.pallas_call(
        _adaln_kernel,
        grid=(n // tn,),
        in_specs=[
            pl.BlockSpec((rows, d), lambda j: (0, 0)),
            pl.BlockSpec((None, d, tn), lambda j: (layer, 0, j)),
            pl.BlockSpec((None, 1, tn), lambda j: (layer, 0, j)),
        ],
        out_specs=pl.BlockSpec((rows, tn), lambda j: (0, j)),
        out_shape=jax.ShapeDtypeStruct((rows, n), F32),
        compiler_params=_params("arbitrary"),
    )(cond, w_mod, b_mod.reshape(depth, 1, n))


def _modulate_kernel(x_ref, g_ref, shift_ref, scale_ref, o_ref):
    x = x_ref[...]
    y = x * lax.rsqrt(jnp.mean(x * x, axis=-1, keepdims=True) + EPS) * g_ref[...]
    o_ref[...] = (y * (1.0 + scale_ref[...]) + shift_ref[...]).astype(BF16)


def _modulate_join_kernel(xa_ref, xb_ref, g_ref, shift_ref, scale_ref, o_ref, ox_ref, *, a_blocks):
    def from_ref(x_ref):
        ox_ref[...] = x_ref[...]
        _modulate_kernel(x_ref, g_ref, shift_ref, scale_ref, o_ref)

    pl.when(pl.program_id(0) < a_blocks)(lambda: from_ref(xa_ref))
    pl.when(pl.program_id(0) >= a_blocks)(lambda: from_ref(xb_ref))


def _modulate_join_call(xa, xb, gains, mods, layer, shift_idx, group_of_row, tr=512):
    (ma, d), mb = xa.shape, xb.shape[0]
    depth = gains.shape[0]
    assert ma % tr == 0 and mb % tr == 0
    a_blocks = ma // tr
    grp = lambda i: group_of_row(i * tr)
    row_spec = pl.BlockSpec((tr, d), lambda i: (i, 0))
    return pl.pallas_call(
        functools.partial(_modulate_join_kernel, a_blocks=a_blocks),
        grid=((ma + mb) // tr,),
        in_specs=[
            pl.BlockSpec((tr, d), lambda i: (jnp.minimum(i, a_blocks - 1), 0)),
            pl.BlockSpec((tr, d), lambda i: (jnp.maximum(i - a_blocks, 0), 0)),
            pl.BlockSpec((None, 1, d), lambda i: (layer, 0, 0)),
            pl.BlockSpec((None, None, 1, d), lambda i: (grp(i), shift_idx, 0, 0)),
            pl.BlockSpec((None, None, 1, d), lambda i: (grp(i), shift_idx + 1, 0, 0)),
        ],
        out_specs=[row_spec, row_spec],
        out_shape=[jax.ShapeDtypeStruct((ma + mb, d), BF16), jax.ShapeDtypeStruct((ma + mb, d), F32)],
        compiler_params=_params("arbitrary"),
    )(xa, xb, gains.reshape(depth, 1, d), mods, mods)


def _modulate_call(x, gains, mods, layer, shift_idx, group_of_row, tr=512):
    m, d = x.shape
    depth = gains.shape[0]
    grp = lambda i: group_of_row(i * tr)
    return pl.pallas_call(
        _modulate_kernel,
        grid=(m // tr,),
        in_specs=[
            pl.BlockSpec((tr, d), lambda i: (i, 0)),
            pl.BlockSpec((None, 1, d), lambda i: (layer, 0, 0)),
            pl.BlockSpec((None, None, 1, d), lambda i: (grp(i), shift_idx, 0, 0)),
            pl.BlockSpec((None, None, 1, d), lambda i: (grp(i), shift_idx + 1, 0, 0)),
        ],
        out_specs=pl.BlockSpec((tr, d), lambda i: (i, 0)),
        out_shape=jax.ShapeDtypeStruct((m, d), BF16),
        compiler_params=_params("arbitrary"),
    )(x, gains.reshape(depth, 1, d), mods, mods)


def _matmul_tiles(m, k, n, n_weights, out_bytes, residual, row_quantum, w_buffers=2):
    for tn in (2048, 1024, 512, 256, 128):
        if n % tn:
            continue
        for tm in (1024, 512, 256):
            if m % tm or row_quantum % tm:
                continue
            need = (2 * tm * k * 2 + n_weights * (w_buffers * k * tn * 4 + k * tn * 2 + tm * tn * 4)
                    + 2 * tm * tn * out_bytes + (2 * tm * tn * 4 if residual else 0))
            if need <= VMEM_TILE_BUDGET:
                return tm, tn
    raise ValueError(f"no matmul tiling fits VMEM for {(m, k, n)}")


def _cast_weights_once(w_refs, wbf_refs):
    @pl.when(pl.program_id(1) == 0)
    def _():
        for w_ref, wbf_ref in zip(w_refs, wbf_refs):
            wbf_ref[...] = w_ref[...].astype(BF16)


def _mm_plain_kernel(a_ref, w_ref, o_ref, wbf_ref):
    _cast_weights_once((w_ref,), (wbf_ref,))
    o_ref[...] = jnp.dot(a_ref[...], wbf_ref[...], preferred_element_type=F32).astype(o_ref.dtype)


def _mm_swiglu_kernel(*refs, adaln_steps):
    if adaln_steps:
        a_ref, wg_ref, wu_ref, c_ref, wm_ref, bm_ref, o_ref, om_ref, wgbf_ref, wubf_ref = refs
    else:
        a_ref, wg_ref, wu_ref, o_ref, wgbf_ref, wubf_ref = refs
    _cast_weights_once((wg_ref, wu_ref), (wgbf_ref, wubf_ref))
    a = a_ref[...]
    g = jnp.dot(a, wgbf_ref[...], preferred_element_type=F32)
    u = jnp.dot(a, wubf_ref[...], preferred_element_type=F32)
    o_ref[...] = (_silu(g) * u).astype(o_ref.dtype)
    if adaln_steps:
        @pl.when(pl.program_id(0) * pl.num_programs(1) + pl.program_id(1) < adaln_steps)
        def _():
            om_ref[...] = _adaln_tile(c_ref, wm_ref, bm_ref)


def _mm_residual_kernel(a_ref, w_ref, x_ref, gate_ref, o_ref, wbf_ref):
    _cast_weights_once((w_ref,), (wbf_ref,))
    y = jnp.dot(a_ref[...], wbf_ref[...], preferred_element_type=F32)
    o_ref[...] = x_ref[...] + gate_ref[...] * y


def _mm_residual_modulate_kernel(a_ref, w_ref, x_ref, gate_ref, g_ref, shift_ref, scale_ref,
                                 ox_ref, oh_ref, wbf_ref):
    @pl.when(pl.program_id(0) == 0)
    def _():
        wbf_ref[...] = w_ref[...].astype(BF16)
    y = jnp.dot(a_ref[...], wbf_ref[...], preferred_element_type=F32)
    x = x_ref[...] + gate_ref[...] * y
    ox_ref[...] = x
    h = x * lax.rsqrt(jnp.mean(x * x, axis=-1, keepdims=True) + EPS) * g_ref[...]
    oh_ref[...] = (h * (1.0 + scale_ref[...]) + shift_ref[...]).astype(BF16)


def _mm_residual_modulate_call(a, w, widx, x, mods, layer, gate_idx, gains, shift_idx, group_of_row,
                               row_quantum):
    m, k = a.shape
    n = w.shape[-1]
    depth = gains.shape[0]
    for tm in (512, 256, 128):
        need = k * n * (4 + 2) + 2 * tm * (k * 2 + n * (4 + 4 + 2)) + 2 * tm * n * 4
        if m % tm == 0 and row_quantum % tm == 0 and need <= VMEM_TILE_BUDGET:
            break
    else:
        raise ValueError(f"no row tile fits VMEM for {(m, k, n)}")
    grp = lambda i: group_of_row(i * tm)
    mod_spec = lambda which: pl.BlockSpec((None, None, 1, n), lambda i: (grp(i), which, 0, 0))
    row_spec = pl.BlockSpec((tm, n), lambda i: (i, 0))
    return pl.pallas_call(
        _mm_residual_modulate_kernel,
        grid=(m // tm,),
        in_specs=[
            pl.BlockSpec((tm, k), lambda i: (i, 0)),
            pl.BlockSpec((None, k, n), lambda i: (widx, 0, 0), pipeline_mode=pl.Buffered(1)),
            row_spec,
            mod_spec(gate_idx),
            pl.BlockSpec((None, 1, n), lambda i: (layer, 0, 0)),
            mod_spec(shift_idx),
            mod_spec(shift_idx + 1),
        ],
        out_specs=[row_spec, row_spec],
        out_shape=[jax.ShapeDtypeStruct((m, n), F32), jax.ShapeDtypeStruct((m, n), BF16)],
        scratch_shapes=[pltpu.VMEM((k, n), BF16)],
        compiler_params=_params("arbitrary"),
    )(a, w, x, mods, gains.reshape(depth, 1, n), mods, mods)


def _mm_plain_call(a, w, widx, out_dtype):
    m, k = a.shape
    n = w.shape[-1]
    tm, tn = _matmul_tiles(m, k, n, 1, jnp.dtype(out_dtype).itemsize, False, m)
    return pl.pallas_call(
        _mm_plain_kernel,
        grid=(n // tn, m // tm),
        in_specs=[
            pl.BlockSpec((tm, k), lambda j, i: (i, 0)),
            pl.BlockSpec((None, k, tn), lambda j, i: (widx, 0, j)),
        ],
        out_specs=pl.BlockSpec((tm, tn), lambda j, i: (i, j)),
        out_shape=jax.ShapeDtypeStruct((m, n), out_dtype),
        scratch_shapes=[pltpu.VMEM((k, tn), BF16)],
        compiler_params=_params("arbitrary", "arbitrary"),
    )(a, w)


ADALN_TILE = 256


def _mm_swiglu_call(a, wg, wu, widx, adaln=None):
    m, k = a.shape
    n = wg.shape[-1]
    tm, tn = _matmul_tiles(m, k, n, 2, 2, False, m)
    wspec = pl.BlockSpec((None, k, tn), lambda j, i: (widx, 0, j))
    in_specs = [pl.BlockSpec((tm, k), lambda j, i: (i, 0)), wspec, wspec]
    args = [a, wg, wu]
    out_specs = [pl.BlockSpec((tm, tn), lambda j, i: (i, j))]
    out_shape = [jax.ShapeDtypeStruct((m, n), BF16)]
    adaln_steps = 0
    if adaln is not None:
        cond, w_mod, b_mod, layer = adaln
        depth, d, n_mod = w_mod.shape
        rows = cond.shape[0]
        adaln_steps = n_mod // ADALN_TILE
        per_j = m // tm
        assert n_mod % ADALN_TILE == 0 and adaln_steps <= (n // tn) * per_j
        tile = lambda j, i: jnp.minimum(j * per_j + i, adaln_steps - 1)
        in_specs += [pl.BlockSpec((rows, d), lambda j, i: (0, 0)),
                     pl.BlockSpec((None, d, ADALN_TILE), lambda j, i: (layer, 0, tile(j, i))),
                     pl.BlockSpec((None, 1, ADALN_TILE), lambda j, i: (layer, 0, tile(j, i)))]
        args += [cond, w_mod, b_mod.reshape(depth, 1, n_mod)]
        out_specs.append(pl.BlockSpec((rows, ADALN_TILE), lambda j, i: (0, tile(j, i))))
        out_shape.append(jax.ShapeDtypeStruct((rows, n_mod), F32))
    outs = pl.pallas_call(
        functools.partial(_mm_swiglu_kernel, adaln_steps=adaln_steps),
        grid=(n // tn, m // tm),
        in_specs=in_specs,
        out_specs=out_specs,
        out_shape=out_shape,
        scratch_shapes=[pltpu.VMEM((k, tn), BF16), pltpu.VMEM((k, tn), BF16)],
        compiler_params=_params("arbitrary", "arbitrary"),
    )(*args)
    return outs if adaln is not None else outs[0]


def _mm_residual_call(a, w, widx, x, mods, gate_idx, group_of_row, row_quantum, row0=0, m=None):
    k = a.shape[1]
    m = a.shape[0] if m is None else m
    n = w.shape[-1]
    tm, tn = _matmul_tiles(m, k, n, 1, 4, True, row_quantum)
    w_mode = None
    wide = _matmul_tiles(m, k, n, 1, 4, True, row_quantum, w_buffers=1)
    if wide[1] > tn:
        (tm, tn), w_mode = wide, pl.Buffered(1)
    assert row0 % tm == 0
    rb0 = row0 // tm
    grp = lambda i: group_of_row((i + rb0) * tm)
    return pl.pallas_call(
        _mm_residual_kernel,
        grid=(n // tn, m // tm),
        in_specs=[
            pl.BlockSpec((tm, k), lambda j, i: (i + rb0, 0)),
            pl.BlockSpec((None, k, tn), lambda j, i: (widx, 0, j), pipeline_mode=w_mode),
            pl.BlockSpec((tm, tn), lambda j, i: (i + rb0, j)),
            pl.BlockSpec((None, None, 1, tn), lambda j, i: (grp(i), gate_idx, 0, j)),
        ],
        out_specs=pl.BlockSpec((tm, tn), lambda j, i: (i, j)),
        out_shape=jax.ShapeDtypeStruct((m, n), F32),
        scratch_shapes=[pltpu.VMEM((k, tn), BF16)],
        compiler_params=_params("arbitrary", "arbitrary"),
    )(a, w, x, mods)


def _bcast_row_in_blocks(x, block, r):
    c, d = x.shape
    x3 = x.reshape(c // block, block, d)
    return jnp.broadcast_to(x3[:, r:r + 1, :], x3.shape).reshape(c, d)


MXU_WIDTH = 256


def _gla_blocks(c):
    return [1 << b for b in range((c // 2).bit_length())] + [0]


def _gla_masks(c, fwd):
    i = lax.broadcasted_iota(jnp.int32, (c, LANES), 0)
    lane = lax.broadcasted_iota(jnp.int32, (c, LANES), 1)
    masks = []
    for b, h in enumerate(_gla_blocks(c)):
        j = lane - (b * c) % LANES
        in_block = jnp.where(j >= 0, jnp.where(j < c, 1.0, 0.0), 0.0)
        if h == 0:
            ok = jnp.where(i == j, 1.0, 0.0)
        else:
            same = (i // (2 * h)) == (j // (2 * h))
            i_hi = (i % (2 * h)) >= h
            j_hi = (j % (2 * h)) >= h
            if fwd:
                ok = jnp.where(same, jnp.where(i_hi, jnp.where(j_hi, 0.0, 1.0), 0.0), 0.0)
            else:
                ok = jnp.where(same, jnp.where(i_hi, 0.0, jnp.where(j_hi, 1.0, 0.0)), 0.0)
        masks.append((ok * in_block).astype(F32))
    return masks


def _split_roles(q, kg, h, fwd, rloc):
    c = q.shape[0]
    if h >= SUBLANES:
        parts = []
        for b in range(c // (2 * h)):
            lo = slice(2 * h * b, 2 * h * b + h)
            hi = slice(2 * h * b + h, 2 * h * (b + 1))
            parts += [kg[lo], q[hi]] if fwd else [q[lo], kg[hi]]
        return jnp.concatenate(parts, axis=0)
    in_hi = (rloc % (2 * h)) >= h
    return jnp.where(in_hi, q, kg) if fwd else jnp.where(in_hi, kg, q)


def _split3(x):
    hi = x.astype(BF16)
    r1 = x - hi.astype(F32)
    mid = r1.astype(BF16)
    lo = (r1 - mid.astype(F32)).astype(BF16)
    return jnp.concatenate([hi, mid, lo], axis=-1)


def _gla_chunks(chains, rloc):
    c, d = chains[0][0].shape
    per_group = MXU_WIDTH // c
    blocks = _gla_blocks(c)
    n_groups = -(-len(blocks) // per_group)
    n_kvregs = -(-len(blocks) * c // LANES)

    es = [None] * len(chains)
    for direction in (True, False):
        ids = [n for n, ch in enumerate(chains) if ch[6] == direction]
        if ids:
            e3 = jnp.dot(chains[ids[0]][7], jnp.concatenate([chains[n][4] for n in ids], axis=1),
                         preferred_element_type=F32)
            for pos, n in enumerate(ids):
                base = 3 * d * pos
                es[n] = e3[:, base:base + d] + e3[:, base + d:base + 2 * d] + e3[:, base + 2 * d:base + 3 * d]

    sides = []
    for (q, kg, fg, v, logf, st, fwd, tri_bf, masks), e in zip(chains, es):
        qs, ks = [], []
        for h in blocks:
            if h == 0:
                qs.append(q.astype(BF16))
                ks.append(kg.astype(BF16))
                continue
            if h == 1:
                in_hi = (rloc % 2) >= 1
                w = jnp.where(in_hi, q * fg, kg) if fwd else jnp.where(in_hi, kg, q * fg)
            else:
                r = h - 1 if fwd else h
                if 2 * h >= SUBLANES:
                    e_ref = _bcast_row_in_blocks(e, 2 * h, r)
                else:
                    e_ref = jnp.where(rloc < 2 * h, _bcast_row_in_blocks(e, SUBLANES, r),
                                      _bcast_row_in_blocks(e, SUBLANES, 2 * h + r))
                w = _split_roles(q, kg, h, fwd, rloc) * (1.0 / jnp.exp2(jnp.abs(e - e_ref)))
            qs.append(w.astype(BF16))
            ks.append(qs[-1])
        pad = [jnp.zeros((c, d), BF16)] * (n_groups * per_group - len(blocks))
        sides.append((qs + pad, ks + pad))

    atts = [[None] * n_kvregs for _ in chains]
    for g in range(n_groups):
        for ci, (qs, ks) in enumerate(sides):
            sl = slice(g * per_group, (g + 1) * per_group)
            prod = lax.dot_general(jnp.concatenate(qs[sl], axis=0), jnp.concatenate(ks[sl], axis=0),
                                   NT_DIMS, preferred_element_type=F32)
            masks = chains[ci][8]
            for b in range(g * per_group, min((g + 1) * per_group, len(blocks))):
                t = b * c // LANES
                col = (t * LANES) % MXU_WIDTH
                row = (b - g * per_group) * c
                term = prod[row:row + c, col:col + LANES] * masks[b]
                atts[ci][t] = term if atts[ci][t] is None else atts[ci][t] + term

    outs = []
    for (q, kg, fg, v, logf, st, fwd, tri_bf, masks), e, att in zip(chains, es, atts):
        vb = v.astype(BF16)
        att_sum = functools.reduce(lambda x, y: x + y, att)
        v_rep = jnp.concatenate([vb] * (LANES // c), axis=0)
        o = jnp.dot(att_sum.astype(BF16), v_rep, preferred_element_type=F32)
        o = o + lax.dot_general((q * jnp.exp2(e)).astype(BF16), st.astype(BF16), NT_DIMS,
                                preferred_element_type=F32)
        e_edge = e[c - 1:c] if fwd else e[0:1]
        kh = (kg * jnp.exp2(e_edge - e)).astype(BF16)
        st_new = st * jnp.exp2(e_edge) + lax.dot_general(vb, kh, TN_DIMS, preferred_element_type=F32)
        outs.append((o, st_new))
    return outs


def _gla_kernel(*refs, seq_len, chunk, has_s0, final):
    refs = list(refs)
    q_ref, ffw_ref, fbw_ref, v_ref, g_ref, lb_ref, on_ref = refs[:7]
    del refs[:7]
    s0_ref = refs.pop(0) if has_s0 else None
    del refs[:2 if final == "join" else 1]
    o_ref = refs.pop(0)
    sf_ref = _final_state_slot(refs.pop(0), final) if final is not None else None
    stf_ref, stb_ref, oacc_ref = refs
    c = chunk
    nch = seq_len // c
    half_n = nch // 2
    d = A_HEAD_DIM
    n_h = q_ref.shape[-1] // d

    i = lax.broadcasted_iota(jnp.int32, (c, c), 0)
    j = lax.broadcasted_iota(jnp.int32, (c, c), 1)
    tri_f = jnp.where(i >= j, 1.0, 0.0).astype(BF16)
    tri_b = jnp.where(i <= j, 1.0, 0.0).astype(BF16)
    masks_f = _gla_masks(c, True)
    masks_b = _gla_masks(c, False)
    rloc = lax.broadcasted_iota(jnp.int32, (c, d), 0) % SUBLANES
    q_scale = d ** -0.5

    def gates(x, lb):
        one_m_lb = 1.0 - lb
        t = jnp.exp(-jnp.abs(x))
        r = 1.0 / (1.0 + t)
        pos = x >= 0.0
        fg = lb + one_m_lb * jnp.where(pos, r, t * r)
        kg = one_m_lb * jnp.where(pos, t * r, r)
        log2f = jnp.maximum(jnp.log2(fg),
                            jnp.log2(one_m_lb) + jnp.minimum(x, 0.0) * LOG2_E - jnp.log2(1.0 + t))
        return fg, kg, log2f

    def chain_rows(t, fwd):
        return pl.ds(pl.multiple_of((t if fwd else nch - 1 - t) * c, c), c)

    chain_ids = [(hh, fwd) for hh in range(n_h) for fwd in (True, False)]

    def all_chains(t):
        where, chains = [], []
        for hh, fwd in chain_ids:
            sl, cols = chain_rows(t, fwd), slice(hh * d, (hh + 1) * d)
            fg, kg, log2f = gates((ffw_ref if fwd else fbw_ref)[sl, cols], lb_ref[:, cols])
            where.append((sl, cols))
            chains.append((_silu(q_ref[sl, cols]) * q_scale, kg, fg, v_ref[sl, cols], _split3(log2f),
                           (stf_ref if fwd else stb_ref)[hh], fwd,
                           tri_f if fwd else tri_b, masks_f if fwd else masks_b))
        outs = _gla_chunks(chains, rloc)
        for n, (o, st) in enumerate(outs):
            (stf_ref if n % 2 == 0 else stb_ref)[n // 2] = st
        return [(sl, cols, o) for (sl, cols), (o, st) in zip(where, outs)]

    def finish(sl, cols, o):
        y = oacc_ref[sl, cols] + o
        y = y * lax.rsqrt(jnp.mean(y * y, axis=-1, keepdims=True) + EPS) * on_ref[...]
        o_ref[sl, cols] = (y * _silu(g_ref[sl, cols])).astype(o_ref.dtype)

    def first_half(t, carry):
        for sl, cols, o in all_chains(t):
            oacc_ref[sl, cols] = o
        return carry

    def second_half(t, carry):
        for sl, cols, o in all_chains(t):
            finish(sl, cols, o)
        return carry

    for hh in range(n_h):
        for st_ref, direction in ((stf_ref, 0), (stb_ref, 1)):
            st_ref[hh] = s0_ref[direction, hh].T if has_s0 else jnp.zeros((d, d), F32)
    unroll = 2 if 2 * n_h < 8 and half_n % 2 == 0 else 1
    lax.fori_loop(0, half_n, first_half, 0, unroll=unroll)
    lax.fori_loop(half_n, nch, second_half, 0, unroll=unroll)
    if final is not None:
        for hh in range(n_h):
            sf_ref[0, hh] = stf_ref[hh].T
            sf_ref[1, hh] = stb_ref[hh].T


GLA_CHUNK = 128


def _join_buffers(bufs, in_specs, args):
    shapes, aliases = [], {}
    for n, buf in enumerate(bufs):
        in_specs.append(pl.BlockSpec(memory_space=pl.ANY))
        args.append(buf)
        shapes.append(jax.ShapeDtypeStruct(buf.shape, buf.dtype))
        aliases[len(args) - 1] = n
    return shapes, aliases


def _final_state_specs(finals, slot, dims, heads_per_step, out_specs):
    n_seq, heads, d = dims
    if finals is None:
        return [], None
    if isinstance(finals, int):
        out_specs.append(pl.BlockSpec((None, finals, 2, heads_per_step, d, d),
                                      lambda n, h: (n, 0, 0, h, 0, 0)))
        return [], (slot, finals)
    out_specs.append(pl.BlockSpec((None, None, 2, heads_per_step, d, d),
                                  lambda n, h: (n, slot, 0, h, 0, 0)))
    return [finals], "join"


def _final_state_slot(sf_ref, final):
    if final == "join":
        return sf_ref
    slot, n_slots = final
    for s in range(n_slots):
        if s != slot:
            sf_ref[s] = jnp.zeros(sf_ref.shape[1:], sf_ref.dtype)
    return sf_ref.at[slot]


def _gla_heads_per_step(seq_len):
    for n_h in (8, 4, 2, 1):
        if 2 * 5 * seq_len * n_h * A_HEAD_DIM * 4 + 4 * seq_len * n_h * A_HEAD_DIM * 2 <= VMEM_TILE_BUDGET:
            return n_h
    raise ValueError(f"sequence of {seq_len} rows does not fit VMEM")


def _gla_call(p, lower_bounds, layer, onorm, eidx, s0, n_seq, seq_len, row0, finals, mix):
    d = A_HEAD_DIM
    n_h = _gla_heads_per_step(seq_len)
    w = n_h * d
    a_width = lower_bounds.shape[-1]
    n_hb = a_width // w
    assert row0 % seq_len == 0 and seq_len % (2 * GLA_CHUNK) == 0 and a_width % w == 0
    sb = row0 // seq_len
    slab = lambda k: pl.BlockSpec((seq_len, w), lambda n, h: (n + sb, k * n_hb + h))
    in_specs = [slab(0), slab(1), slab(2), slab(3), slab(4),
                pl.BlockSpec((None, 1, w), lambda n, h: (layer, 0, h)),
                pl.BlockSpec((None, 1, d), lambda n, h: (eidx, 0, 0))]
    args = [p, p, p, p, p, lower_bounds.reshape(lower_bounds.shape[0], 1, a_width),
            onorm.reshape(onorm.shape[0], 1, d)]
    if s0 is not None:
        in_specs.append(pl.BlockSpec((None, None, 2, n_h, d, d), lambda n, h: (n, eidx, 0, h, 0, 0)))
        args.append(s0)
    out_specs = [pl.BlockSpec((seq_len, w), lambda n, h: (n + sb, h))]
    joined, final = _final_state_specs(finals, eidx, (n_seq, a_width // d, d), n_h, out_specs)
    out_shape, aliases = _join_buffers([mix] + joined, in_specs, args)
    if isinstance(final, tuple):
        out_shape.append(jax.ShapeDtypeStruct((n_seq, final[1], 2, a_width // d, d, d), F32))
    kern = functools.partial(_gla_kernel, seq_len=seq_len, chunk=GLA_CHUNK,
                             has_s0=s0 is not None, final=final)
    return pl.pallas_call(
        kern,
        grid=(n_seq, n_hb),
        in_specs=in_specs,
        out_specs=out_specs,
        out_shape=out_shape,
        input_output_aliases=aliases,
        scratch_shapes=[pltpu.VMEM((n_h, d, d), F32), pltpu.VMEM((n_h, d, d), F32),
                        pltpu.VMEM((seq_len, w), F32)],
        compiler_params=_params("arbitrary", "arbitrary"),
    )(*args)


def _ret_kernel(*refs, seq_len, chunk, has_s0, final, rope):
    refs = list(refs)
    q_ref, k_ref, v_ref, g_ref, dl_ref, gn_ref = refs[:6]
    del refs[:6]
    cos_ref = refs.pop(0) if rope else None
    sin_ref = refs.pop(0) if rope else None
    s0_ref = refs.pop(0) if has_s0 else None
    del refs[:2 if final == "join" else 1]
    o_ref = refs.pop(0)
    sf_ref = _final_state_slot(refs.pop(0), final) if final is not None else None
    stf_ref, stb_ref, oacc_ref = refs
    c = chunk
    nch = seq_len // c
    half_n = nch // 2
    d = dl_ref.shape[-1]
    n_h = q_ref.shape[-1] // d
    half = d // 2
    k_scale = d ** -0.5

    dist = (lax.broadcasted_iota(jnp.int32, (c, c), 0)
            - lax.broadcasted_iota(jnp.int32, (c, c), 1)).astype(F32)
    low = dist >= 0.0
    up = dist <= 0.0
    row = lax.broadcasted_iota(jnp.int32, (c, d), 0).astype(F32)

    def decay_tables(hh):
        lg_f = _log_sigmoid(dl_ref[0, hh])
        lg_b = _log_sigmoid(dl_ref[1, hh])
        intra = (jnp.where(low, jnp.exp(jnp.where(low, dist, 0.0) * lg_f[:, :c]), 0.0)
                 + jnp.where(up, jnp.exp(jnp.where(up, -dist, 0.0) * lg_b[:, :c]), 0.0))
        fwd = (jnp.exp((row + 1.0) * lg_f), jnp.exp((c - 1.0 - row) * lg_f), jnp.exp(c * lg_f))
        bwd = (jnp.exp((c - row) * lg_b), jnp.exp(row * lg_b), jnp.exp(c * lg_b))
        return intra, fwd, bwd

    tables = [decay_tables(hh) for hh in range(n_h)]

    def rotate(x, sl):
        if not rope:
            return x
        cos, sin = cos_ref[sl, :], sin_ref[sl, :]
        x1, x2 = x[:, :half], x[:, half:]
        return jnp.concatenate([x1 * cos - x2 * sin, x1 * sin + x2 * cos], axis=-1)

    def load(ci, hh):
        sl, cols = pl.ds(pl.multiple_of(ci * c, c), c), slice(hh * d, (hh + 1) * d)
        return (sl, cols, rotate(q_ref[sl, cols], sl), rotate(k_ref[sl, cols] * k_scale, sl),
                v_ref[sl, cols].astype(BF16))

    def carried(q, k, vb, st_ref, hh, decs):
        q_dec, k_dec, c_dec = decs
        st = st_ref[hh]
        o = jnp.dot((q * q_dec).astype(BF16), st.astype(BF16), preferred_element_type=F32)
        st_ref[hh] = c_dec * st + lax.dot_general((k * k_dec).astype(BF16), vb, TN_DIMS,
                                                  preferred_element_type=F32)
        return o

    def trip(t):
        left = [load(t, hh) for hh in range(n_h)]
        right = [load(nch - 1 - t, hh) for hh in range(n_h)]
        scores = [lax.dot_general(q.astype(BF16), k.astype(BF16), NT_DIMS, preferred_element_type=F32)
                  for _, _, q, k, _ in left]
        o_left = [carried(q, k, vb, stf_ref, hh, tables[hh][1]) for hh, (_, _, q, k, vb) in enumerate(left)]
        o_right = [carried(q, k, vb, stb_ref, hh, tables[hh][2]) for hh, (_, _, q, k, vb) in enumerate(right)]
        for hh, (_, _, _, _, vb) in enumerate(left):
            o_left[hh] = o_left[hh] + jnp.dot((scores[hh] * tables[hh][0]).astype(BF16), vb,
                                              preferred_element_type=F32)
        return ([(sl, cols, o) for (sl, cols, *_), o in zip(left, o_left)]
                + [(sl, cols, o) for (sl, cols, *_), o in zip(right, o_right)])

    def finish(sl, cols, o):
        y = oacc_ref[sl, cols] + o
        yc = y - jnp.mean(y, axis=-1, keepdims=True)
        yn = yc * lax.rsqrt(jnp.mean(yc * yc, axis=-1, keepdims=True) + EPS)
        o_ref[sl, cols] = (yn * gn_ref[:, cols] * _silu(g_ref[sl, cols])).astype(o_ref.dtype)

    def first_half(t, carry):
        for sl, cols, o in trip(t):
            oacc_ref[sl, cols] = o
        return carry

    def second_half(t, carry):
        for sl, cols, o in trip(t):
            finish(sl, cols, o)
        return carry

    for hh in range(n_h):
        for st_ref, direction in ((stf_ref, 0), (stb_ref, 1)):
            st_ref[hh] = s0_ref[direction, hh] if has_s0 else jnp.zeros((d, d), F32)
    lax.fori_loop(0, half_n, first_half, 0)
    lax.fori_loop(half_n, nch, second_half, 0)
    if final is not None:
        for hh in range(n_h):
            sf_ref[0, hh] = stf_ref[hh]
            sf_ref[1, hh] = stb_ref[hh]


def _ret_call(p, col0, decay_logit, gnorm, eidx, rope_tabs, s0, n_seq, seq_len, row0, finals,
              mix, mix_col0):
    n_heads = decay_logit.shape[-1]
    b_width = gnorm.shape[-1]
    d = b_width // n_heads
    chunk = min(d, seq_len // 2)
    n_h = 2 if n_heads % 2 == 0 else 1
    w = n_h * d
    n_hb = n_heads // n_h
    assert row0 % seq_len == 0 and seq_len % (2 * chunk) == 0 and col0 % w == 0 and chunk <= d
    assert mix_col0 % w == 0
    sb = row0 // seq_len
    cb0 = col0 // w
    ocb0 = mix_col0 // w
    slab = lambda k: pl.BlockSpec((seq_len, w), lambda n, h: (n + sb, cb0 + k * n_hb + h))
    dl = jnp.broadcast_to(decay_logit[:, :, :, None, None], decay_logit.shape + (1, d))
    in_specs = [slab(0), slab(1), slab(2), slab(3),
                pl.BlockSpec((None, 2, n_h, 1, d), lambda n, h: (eidx, 0, h, 0, 0)),
                pl.BlockSpec((None, 1, w), lambda n, h: (eidx, 0, h))]
    args = [p, p, p, p, dl, gnorm.reshape(gnorm.shape[0], 1, b_width)]
    if rope_tabs is not None:
        tab = pl.BlockSpec((seq_len, d // 2), lambda n, h: (0, 0))
        in_specs += [tab, tab]
        args += list(rope_tabs)
    if s0 is not None:
        in_specs.append(pl.BlockSpec((None, None, 2, n_h, d, d), lambda n, h: (n, eidx, 0, h, 0, 0)))
        args.append(s0)
    out_specs = [pl.BlockSpec((seq_len, w), lambda n, h: (n + sb, ocb0 + h))]
    joined, final = _final_state_specs(finals, eidx, (n_seq, n_heads, d), n_h, out_specs)
    out_shape, aliases = _join_buffers([mix] + joined, in_specs, args)
    if isinstance(final, tuple):
        out_shape.append(jax.ShapeDtypeStruct((n_seq, final[1], 2, n_heads, d, d), F32))
    kern = functools.partial(_ret_kernel, seq_len=seq_len, chunk=chunk, has_s0=s0 is not None,
                             final=final, rope=rope_tabs is not None)
    return pl.pallas_call(
        kern,
        grid=(n_seq, n_hb),
        in_specs=in_specs,
        out_specs=out_specs,
        out_shape=out_shape,
        input_output_aliases=aliases,
        scratch_shapes=[pltpu.VMEM((n_h, d, d), F32), pltpu.VMEM((n_h, d, d), F32),
                        pltpu.VMEM((seq_len, w), F32)],
        compiler_params=_params("arbitrary", "arbitrary"),
    )(*args)


Q_PRESCALE = ATTN_HEAD_DIM ** -0.5 * LOG2_E


def _head_norm(x, gain):
    return x * lax.rsqrt(jnp.mean(x * x, axis=-1, keepdims=True) + EPS) * gain


def _rotate(y, cos, sin):
    return y * cos + pltpu.roll(y, y.shape[-1] // 2, 1) * sin


def _mm_qproj_kernel(a_ref, w_ref, gain_ref, cos_ref, sin_ref, o_ref, wbf_ref):
    _cast_weights_once((w_ref,), (wbf_ref,))
    y = jnp.dot(a_ref[...], wbf_ref[...], preferred_element_type=F32)
    hd = ATTN_HEAD_DIM
    for h in range(y.shape[-1] // hd):
        cols = slice(h * hd, (h + 1) * hd)
        z = _rotate(_head_norm(y[:, cols], gain_ref[...]), cos_ref[...], sin_ref[...])
        o_ref[:, cols] = (z * Q_PRESCALE).astype(BF16)


def _mm_qproj_call(a, w, widx, n_cols, q_norm, rope_rows):
    m, k = a.shape
    hd = ATTN_HEAD_DIM
    tm, tn = _matmul_tiles(m, k, n_cols, 1, 2, False, m)
    tab = pl.BlockSpec((tm, hd), lambda j, i: (i, 0))
    return pl.pallas_call(
        _mm_qproj_kernel,
        grid=(n_cols // tn, m // tm),
        in_specs=[
            pl.BlockSpec((tm, k), lambda j, i: (i, 0)),
            pl.BlockSpec((None, k, tn), lambda j, i: (widx, 0, j)),
            pl.BlockSpec((None, 1, hd), lambda j, i: (widx, 0, 0)),
            tab, tab,
        ],
        out_specs=pl.BlockSpec((tm, tn), lambda j, i: (i, j)),
        out_shape=jax.ShapeDtypeStruct((m, n_cols), BF16),
        scratch_shapes=[pltpu.VMEM((k, tn), BF16)],
        compiler_params=_params("arbitrary", "arbitrary"),
    )(a, w, q_norm.reshape(-1, 1, hd), *rope_rows)


def _mm_kvproj_kernel(*refs, cache, cache_tiles):
    refs = list(refs)
    a_ref, w_ref, gain_ref, cos_ref, sin_ref = refs[:5]
    del refs[:7 if cache == "join" else 5]
    kf_out, vf_out, k_out, v_out, wbf_ref = refs
    _cast_weights_once((w_ref,), (wbf_ref,))
    y = jnp.dot(a_ref[...], wbf_ref[...], preferred_element_type=F32)
    hd = ATTN_HEAD_DIM
    kvw = y.shape[-1] // 2
    keys = []
    for h in range(kvw // hd):
        cols = slice(h * hd, (h + 1) * hd)
        keys.append(_rotate(_head_norm(y[:, cols], gain_ref[...]), cos_ref[...], sin_ref[...]))
        k_out[:, cols] = keys[-1].astype(BF16)
    v = y[:, kvw:]
    v_out[...] = v.astype(BF16)

    @pl.when(pl.program_id(1) < cache_tiles)
    def _():
        kf, vf = kf_out, vf_out
        if cache != "join":
            slot, n_slots = cache
            for s in range(n_slots):
                if s != slot:
                    kf_out[:, s] = jnp.zeros(kf_out.shape[:1] + kf_out.shape[2:], F32)
                    vf_out[:, s] = jnp.zeros(vf_out.shape[:1] + vf_out.shape[2:], F32)
            kf, vf = kf_out.at[:, slot], vf_out.at[:, slot]
        seqs, seq_len = kf.shape[0], kf.shape[1]
        for h, z in enumerate(keys):
            kf[:, :, h * hd:(h + 1) * hd] = z.reshape(seqs, seq_len, hd)
        vf[...] = v.reshape(seqs, seq_len, kvw)


def _mm_kvproj_call(a, w, widx, col0, k_norm, rope_rows, cache_seqs, cache_seq_len, caches):
    m, k = a.shape
    hd = ATTN_HEAD_DIM
    kvw = ATTN_KV_HEADS * hd
    tn = 2 * kvw
    assert col0 % tn == 0
    tm, _ = _matmul_tiles(m, k, tn, 1, 2 + 4, False, m)
    assert tm % cache_seq_len == 0 and (cache_seqs * cache_seq_len) % tm == 0
    seqs_per_tile = tm // cache_seq_len
    cache_tiles = cache_seqs * cache_seq_len // tm
    tab = pl.BlockSpec((tm, hd), lambda j, i: (i, 0))
    row_spec = pl.BlockSpec((tm, kvw), lambda j, i: (i, 0))
    in_specs = [pl.BlockSpec((tm, k), lambda j, i: (i, 0)),
                pl.BlockSpec((None, k, tn), lambda j, i: (widx, 0, col0 // tn)),
                pl.BlockSpec((None, 1, hd), lambda j, i: (widx, 0, 0)),
                tab, tab]
    args = [a, w, k_norm.reshape(-1, 1, hd), *rope_rows]
    held = lambda i: jnp.minimum(i, cache_tiles - 1)
    if isinstance(caches, int):
        cache = (widx, caches)
        cache_spec = pl.BlockSpec((seqs_per_tile, caches, cache_seq_len, kvw),
                                  lambda j, i: (held(i), 0, 0, 0))
        out_shape = [jax.ShapeDtypeStruct((cache_seqs, caches, cache_seq_len, kvw), F32)] * 2
        aliases = {}
    else:
        cache = "join"
        cache_spec = pl.BlockSpec((seqs_per_tile, None, cache_seq_len, kvw),
                                  lambda j, i: (held(i), widx, 0, 0))
        out_shape, aliases = _join_buffers(list(caches), in_specs, args)
    return pl.pallas_call(
        functools.partial(_mm_kvproj_kernel, cache=cache, cache_tiles=cache_tiles),
        grid=(1, m // tm),
        in_specs=in_specs,
        out_specs=[cache_spec, cache_spec, row_spec, row_spec],
        out_shape=out_shape + [jax.ShapeDtypeStruct((m, kvw), BF16)] * 2,
        input_output_aliases=aliases,
        scratch_shapes=[pltpu.VMEM((k, tn), BF16)],
        compiler_params=_params("arbitrary", "arbitrary"),
    )(*args)


def _attn_kernel(*refs, group):
    q_ref, k_ref, v_ref, _, o_ref = refs
    hd = ATTN_HEAD_DIM
    for kh in range(k_ref.shape[-1] // hd):
        k = k_ref[:, kh * hd:(kh + 1) * hd]
        v = v_ref[:, kh * hd:(kh + 1) * hd]
        v_ext = jnp.concatenate([v, jnp.ones_like(v)], axis=1)
        for g in range(group):
            cols = slice((kh * group + g) * hd, (kh * group + g + 1) * hd)
            s = lax.dot_general(q_ref[:, cols], k, NT_DIMS, preferred_element_type=F32)
            p = jnp.exp2(s - jnp.max(s, axis=-1, keepdims=True))
            o = jnp.dot(p.astype(BF16), v_ext, preferred_element_type=F32)
            o_ref[:, cols] = (o[:, :hd] / o[:, hd:]).astype(o_ref.dtype)


ATTN_SCORE_BUDGET = 4 * 1024 * 1024


ATTN_Q_ROWS = 1024


def _attn_call(q, k, v, n_seq, seq_len, row0, mix):
    kvw = k.shape[-1]
    qw = q.shape[-1]
    hd = ATTN_HEAD_DIM
    n_kv = kvw // hd
    group = qw // hd // n_kv
    tq = min(ATTN_Q_ROWS, seq_len)
    assert row0 % tq == 0 and seq_len % tq == 0 and row0 % seq_len == 0
    per_seq = seq_len // tq
    rb0 = row0 // tq
    lk = seq_len if k.ndim == 2 else k.shape[1]
    kv_step = n_kv if tq * lk * 4 * n_kv <= ATTN_SCORE_BUDGET else 1
    q_spec = pl.BlockSpec((tq, kv_step * group * hd), lambda n, kh, i: (rb0 + n * per_seq + i, kh))
    if k.ndim == 2:
        kv_spec = pl.BlockSpec((seq_len, kv_step * hd), lambda n, kh, i: (row0 // seq_len + n, kh))
    else:
        kv_spec = pl.BlockSpec((None, lk, kv_step * hd), lambda n, kh, i: (n, 0, kh))
    in_specs = [q_spec, kv_spec, kv_spec]
    args = [q, k, v]
    (mix_shape,), aliases = _join_buffers([mix], in_specs, args)
    return pl.pallas_call(
        functools.partial(_attn_kernel, group=group),
        grid=(n_seq, n_kv // kv_step, per_seq),
        in_specs=in_specs,
        out_specs=q_spec,
        out_shape=mix_shape,
        input_output_aliases=aliases,
        compiler_params=_params("arbitrary", "arbitrary", "arbitrary"),
    )(*args)


def _grid_rope(n_tokens, head_dim):
    rows = n_tokens // GRID_W
    row = jnp.broadcast_to(jnp.arange(rows)[:, None], (rows, GRID_W)).reshape(-1).astype(F32)
    col = jnp.broadcast_to(jnp.arange(GRID_W)[None, :], (rows, GRID_W)).reshape(-1).astype(F32)
    per_axis = head_dim // 4
    freqs = ROPE_THETA ** (-jnp.arange(per_axis, dtype=F32) / per_axis)
    ang = jnp.concatenate([row[:, None] * freqs, col[:, None] * freqs], axis=-1)
    return jnp.cos(ang), jnp.sin(ang)


def kernel(x_prompt, x_sample, c, state_hgrn, state_ret, cache_k, cache_v, c_ctx, w_mod, b_mod, norm_mix, norm_ffn, w_in_even, w_out_even, hgrn_lb_logits, hgrn_onorm, ret_decay_logit, ret_gnorm, w_in_attn, w_out_attn, q_norm, k_norm, w_ffn_gate, w_ffn_up, w_ffn_down):
    n_p, l_p, d = x_prompt.shape
    n_s, l_s, _ = x_sample.shape
    depth = w_mod.shape[0]
    rows_p = n_p * l_p
    rows_s = n_s * l_s
    a_width = hgrn_lb_logits.shape[-1]
    b_width = ret_gnorm.shape[-1]
    kv_width = ATTN_KV_HEADS * ATTN_HEAD_DIM
    row_quantum = math.gcd(rows_p, l_s)

    def group_of_row(r):
        return jnp.where(r < rows_p, 0, 1 + (r - rows_p) // l_s)

    cond = jnp.concatenate([c_ctx[None, :], c, jnp.zeros((-(1 + n_s) % SUBLANES, d), F32)], axis=0)
    as_mods = lambda m: m.reshape(cond.shape[0], 6, 1, d)
    mods = as_mods(_adaln_call(cond, w_mod, b_mod, 0))

    cos_a, sin_a = _grid_rope(l_s, ATTN_HEAD_DIM)
    rope_attn = (jnp.concatenate([cos_a, cos_a], axis=-1), jnp.concatenate([-sin_a, sin_a], axis=-1))
    rope_rows = (jnp.concatenate([jnp.ones((rows_p, ATTN_HEAD_DIM), F32), jnp.tile(rope_attn[0], (n_s, 1))]),
                 jnp.concatenate([jnp.zeros((rows_p, ATTN_HEAD_DIM), F32), jnp.tile(rope_attn[1], (n_s, 1))]))
    rope_ret = _grid_rope(l_s, b_width // B_HEADS)
    lb_cum = jnp.cumsum(jax.nn.softmax(hgrn_lb_logits.astype(F32), axis=0), axis=0)
    lower_bounds = lb_cum - lb_cum[0]

    n_even = w_in_even.shape[0]
    out_hgrn, out_ret = n_even, n_even
    out_kv = w_in_attn.shape[0]
    for l in range(depth):
        if l == 0:
            mix, x = _modulate_join_call(x_prompt.reshape(rows_p, d), x_sample.reshape(rows_s, d),
                                         norm_mix, mods, l, 0, group_of_row)
        else:
            mix = _modulate_call(x, norm_mix, mods, l, 0, group_of_row)
        if l % 2 == 0:
            e = l // 2
            p = _mm_plain_call(mix, w_in_even, e, F32)
            mix, out_hgrn = _gla_call(p, lower_bounds, l, hgrn_onorm, e, None, n_p, l_p, 0, out_hgrn, mix)
            (mix,) = _gla_call(p, lower_bounds, l, hgrn_onorm, e, state_hgrn, n_s, l_s, rows_p, None, mix)
            mix, out_ret = _ret_call(p, 5 * a_width, ret_decay_logit, ret_gnorm, e, None, None,
                                     n_p, l_p, 0, out_ret, mix, a_width)
            (mix,) = _ret_call(p, 5 * a_width, ret_decay_logit, ret_gnorm, e, rope_ret, state_ret,
                               n_s, l_s, rows_p, None, mix, a_width)
            w_out, widx = w_out_even, e
        else:
            o = l // 2
            q = _mm_qproj_call(mix, w_in_attn, o, d, q_norm, rope_rows)
            *out_kv, k, v = _mm_kvproj_call(mix, w_in_attn, o, d, k_norm, rope_rows, n_p, l_p, out_kv)
            past = cache_k.shape[2]
            with_cache = lambda new, cache: jnp.concatenate(
                [new[rows_p:].reshape(n_s, l_s, kv_width),
                 cache[:, o].reshape(n_s, past, kv_width).astype(BF16)], axis=1)
            k_all, v_all = with_cache(k, cache_k), with_cache(v, cache_v)
            mix = _attn_call(q, k, v, n_p, l_p, 0, mix)
            mix = _attn_call(q, k_all, v_all, n_s, l_s, rows_p, mix)
            w_out, widx = w_out_attn, o
        x, h = _mm_residual_modulate_call(mix, w_out, widx, x, mods, l, 2, norm_ffn, 3, group_of_row,
                                          row_quantum)
        if l < depth - 1:
            u, next_mods = _mm_swiglu_call(h, w_ffn_gate, w_ffn_up, l, (cond, w_mod, b_mod, l + 1))
        else:
            u = _mm_swiglu_call(h, w_ffn_gate, w_ffn_up, l)
        down = functools.partial(_mm_residual_call, u, w_ffn_down, l, x, mods, 5, group_of_row,
                                 row_quantum)
        if l < depth - 1:
            x = down()
            mods = as_mods(next_mods)
    return (down(0, rows_p).reshape(n_p, l_p, d), down(rows_p, rows_s).reshape(n_s, l_s, d),
            out_hgrn, out_ret,
            *(t.reshape(n_p, -1, l_p, ATTN_KV_HEADS, ATTN_HEAD_DIM) for t in out_kv))
```

```python
import functools
import math

import jax
import jax.numpy as jnp
from jax import lax
from jax.experimental import pallas as pl
from jax.experimental.pallas import tpu as pltpu

F32 = jnp.float32
BF16 = jnp.bfloat16

EPS = 1e-6
ROPE_THETA = 10000.0
GRID_W = 64
A_HEAD_DIM = 128
B_HEADS = 4
ATTN_HEAD_DIM = 128
ATTN_KV_HEADS = 4
LANES = 128
SUBLANES = 8
VMEM_LIMIT = 56 * 1024 * 1024
VMEM_TILE_BUDGET = 50 * 1024 * 1024

LOG2_E = math.log2(math.e)

NT_DIMS = (((1,), (1,)), ((), ()))
TN_DIMS = (((0,), (0,)), ((), ()))


def _params(*sem):
    return pltpu.CompilerParams(dimension_semantics=sem, vmem_limit_bytes=VMEM_LIMIT)


def _silu(x):
    return x / (1.0 + jnp.exp(-x))


def _log_sigmoid(x):
    return jnp.minimum(x, 0.0) - jnp.log1p(jnp.exp(-jnp.abs(x)))


def _adaln_tile(c_ref, w_ref, b_ref):
    s = _silu(c_ref[...]).astype(BF16)
    return jnp.dot(s, w_ref[...].astype(BF16), preferred_element_type=F32) + b_ref[...]


def _adaln_kernel(c_ref, w_ref, b_ref, o_ref):
    o_ref[...] = _adaln_tile(c_ref, w_ref, b_ref)


def _adaln_call(cond, w_mod, b_mod, layer, tn=1024):
    depth, d, n = w_mod.shape
    rows = cond.shape[0]
    return pl.pallas_call(
        _adaln_kernel,
        grid=(n // tn,),
        in_specs=[
            pl.BlockSpec((rows, d), lambda j: (0, 0)),
            pl.BlockSpec((None, d, tn), lambda j: (layer, 0, j)),
            pl.BlockSpec((None, 1, tn), lambda j: (layer, 0, j)),
        ],
        out_specs=pl.BlockSpec((rows, tn), lambda j: (0, j)),
        out_shape=jax.ShapeDtypeStruct((rows, n), F32),
        compiler_params=_params("arbitrary"),
    )(cond, w_mod, b_mod.reshape(depth, 1, n))


def _modulate_kernel(x_ref, g_ref, shift_ref, scale_ref, o_ref):
    x = x_ref[...]
    y = x * lax.rsqrt(jnp.mean(x * x, axis=-1, keepdims=True) + EPS) * g_ref[...]
    o_ref[...] = (y * (1.0 + scale_ref[...]) + shift_ref[...]).astype(BF16)


def _modulate_join_kernel(xa_ref, xb_ref, g_ref, shift_ref, scale_ref, o_ref, ox_ref, *, a_blocks):
    def from_ref(x_ref):
        ox_ref[...] = x_ref[...]
        _modulate_kernel(x_ref, g_ref, shift_ref, scale_ref, o_ref)

    pl.when(pl.program_id(0) < a_blocks)(lambda: from_ref(xa_ref))
    pl.when(pl.program_id(0) >= a_blocks)(lambda: from_ref(xb_ref))


def _modulate_join_call(xa, xb, gains, mods, layer, shift_idx, group_of_row, tr=512):
    (ma, d), mb = xa.shape, xb.shape[0]
    depth = gains.shape[0]
    assert ma % tr == 0 and mb % tr == 0
    a_blocks = ma // tr
    grp = lambda i: group_of_row(i * tr)
    row_spec = pl.BlockSpec((tr, d), lambda i: (i, 0))
    return pl.pallas_call(
        functools.partial(_modulate_join_kernel, a_blocks=a_blocks),
        grid=((ma + mb) // tr,),
        in_specs=[
            pl.BlockSpec((tr, d), lambda i: (jnp.minimum(i, a_blocks - 1), 0)),
            pl.BlockSpec((tr, d), lambda i: (jnp.maximum(i - a_blocks, 0), 0)),
            pl.BlockSpec((None, 1, d), lambda i: (layer, 0, 0)),
            pl.BlockSpec((None, None, 1, d), lambda i: (grp(i), shift_idx, 0, 0)),
            pl.BlockSpec((None, None, 1, d), lambda i: (grp(i), shift_idx + 1, 0, 0)),
        ],
        out_specs=[row_spec, row_spec],
        out_shape=[jax.ShapeDtypeStruct((ma + mb, d), BF16), jax.ShapeDtypeStruct((ma + mb, d), F32)],
        compiler_params=_params("arbitrary"),
    )(xa, xb, gains.reshape(depth, 1, d), mods, mods)


def _modulate_call(x, gains, mods, layer, shift_idx, group_of_row, tr=512):
    m, d = x.shape
    depth = gains.shape[0]
    grp = lambda i: group_of_row(i * tr)
    return pl.pallas_call(
        _modulate_kernel,
        grid=(m // tr,),
        in_specs=[
            pl.BlockSpec((tr, d), lambda i: (i, 0)),
            pl.BlockSpec((None, 1, d), lambda i: (layer, 0, 0)),
            pl.BlockSpec((None, None, 1, d), lambda i: (grp(i), shift_idx, 0, 0)),
            pl.BlockSpec((None, None, 1, d), lambda i: (grp(i), shift_idx + 1, 0, 0)),
        ],
        out_specs=pl.BlockSpec((tr, d), lambda i: (i, 0)),
        out_shape=jax.ShapeDtypeStruct((m, d), BF16),
        compiler_params=_params("arbitrary"),
    )(x, gains.reshape(depth, 1, d), mods, mods)


def _matmul_tiles(m, k, n, n_weights, out_bytes, residual, row_quantum):
    for tn in (2048, 1024, 512, 256, 128):
        if n % tn:
            continue
        for tm in (1024, 512, 256):
            if m % tm or row_quantum % tm:
                continue
            need = (2 * tm * k * 2 + n_weights * (2 * k * tn * 4 + k * tn * 2 + tm * tn * 4)
                    + 2 * tm * tn * out_bytes + (2 * tm * tn * 4 if residual else 0))
            if need <= VMEM_TILE_BUDGET:
                return tm, tn
    raise ValueError(f"no matmul tiling fits VMEM for {(m, k, n)}")


def _cast_weights_once(w_refs, wbf_refs):
    @pl.when(pl.program_id(1) == 0)
    def _():
        for w_ref, wbf_ref in zip(w_refs, wbf_refs):
            wbf_ref[...] = w_ref[...].astype(BF16)


def _mm_plain_kernel(a_ref, w_ref, o_ref, wbf_ref):
    _cast_weights_once((w_ref,), (wbf_ref,))
    o_ref[...] = jnp.dot(a_ref[...], wbf_ref[...], preferred_element_type=F32).astype(o_ref.dtype)


def _mm_swiglu_kernel(*refs, adaln_steps):
    if adaln_steps:
        a_ref, wg_ref, wu_ref, c_ref, wm_ref, bm_ref, o_ref, om_ref, wgbf_ref, wubf_ref = refs
    else:
        a_ref, wg_ref, wu_ref, o_ref, wgbf_ref, wubf_ref = refs
    _cast_weights_once((wg_ref, wu_ref), (wgbf_ref, wubf_ref))
    a = a_ref[...]
    g = jnp.dot(a, wgbf_ref[...], preferred_element_type=F32)
    u = jnp.dot(a, wubf_ref[...], preferred_element_type=F32)
    o_ref[...] = (_silu(g) * u).astype(o_ref.dtype)
    if adaln_steps:
        @pl.when(pl.program_id(0) * pl.num_programs(1) + pl.program_id(1) < adaln_steps)
        def _():
            om_ref[...] = _adaln_tile(c_ref, wm_ref, bm_ref)


def _mm_residual_kernel(a_ref, w_ref, x_ref, gate_ref, o_ref, wbf_ref):
    _cast_weights_once((w_ref,), (wbf_ref,))
    y = jnp.dot(a_ref[...], wbf_ref[...], preferred_element_type=F32)
    o_ref[...] = x_ref[...] + gate_ref[...] * y


def _mm_residual_modulate_kernel(a_ref, w_ref, x_ref, gate_ref, g_ref, shift_ref, scale_ref,
                                 ox_ref, oh_ref, wbf_ref):
    @pl.when(pl.program_id(0) == 0)
    def _():
        wbf_ref[...] = w_ref[...].astype(BF16)
    y = jnp.dot(a_ref[...], wbf_ref[...], preferred_element_type=F32)
    x = x_ref[...] + gate_ref[...] * y
    ox_ref[...] = x
    h = x * lax.rsqrt(jnp.mean(x * x, axis=-1, keepdims=True) + EPS) * g_ref[...]
    oh_ref[...] = (h * (1.0 + scale_ref[...]) + shift_ref[...]).astype(BF16)


def _mm_residual_modulate_call(a, w, widx, x, mods, layer, gate_idx, gains, shift_idx, group_of_row,
                               row_quantum):
    m, k = a.shape
    n = w.shape[-1]
    depth = gains.shape[0]
    for tm in (512, 256, 128):
        need = k * n * (4 + 2) + 2 * tm * (k * 2 + n * (4 + 4 + 2)) + 2 * tm * n * 4
        if m % tm == 0 and row_quantum % tm == 0 and need <= VMEM_TILE_BUDGET:
            break
    else:
        raise ValueError(f"no row tile fits VMEM for {(m, k, n)}")
    grp = lambda i: group_of_row(i * tm)
    mod_spec = lambda which: pl.BlockSpec((None, None, 1, n), lambda i: (grp(i), which, 0, 0))
    row_spec = pl.BlockSpec((tm, n), lambda i: (i, 0))
    return pl.pallas_call(
        _mm_residual_modulate_kernel,
        grid=(m // tm,),
        in_specs=[
            pl.BlockSpec((tm, k), lambda i: (i, 0)),
            pl.BlockSpec((None, k, n), lambda i: (widx, 0, 0), pipeline_mode=pl.Buffered(1)),
            row_spec,
            mod_spec(gate_idx),
            pl.BlockSpec((None, 1, n), lambda i: (layer, 0, 0)),
            mod_spec(shift_idx),
            mod_spec(shift_idx + 1),
        ],
        out_specs=[row_spec, row_spec],
        out_shape=[jax.ShapeDtypeStruct((m, n), F32), jax.ShapeDtypeStruct((m, n), BF16)],
        scratch_shapes=[pltpu.VMEM((k, n), BF16)],
        compiler_params=_params("arbitrary"),
    )(a, w, x, mods, gains.reshape(depth, 1, n), mods, mods)


def _mm_plain_call(a, w, widx, out_dtype):
    m, k = a.shape
    n = w.shape[-1]
    tm, tn = _matmul_tiles(m, k, n, 1, jnp.dtype(out_dtype).itemsize, False, m)
    return pl.pallas_call(
        _mm_plain_kernel,
        grid=(n // tn, m // tm),
        in_specs=[
            pl.BlockSpec((tm, k), lambda j, i: (i, 0)),
            pl.BlockSpec((None, k, tn), lambda j, i: (widx, 0, j)),
        ],
        out_specs=pl.BlockSpec((tm, tn), lambda j, i: (i, j)),
        out_shape=jax.ShapeDtypeStruct((m, n), out_dtype),
        scratch_shapes=[pltpu.VMEM((k, tn), BF16)],
        compiler_params=_params("arbitrary", "arbitrary"),
    )(a, w)


ADALN_TILE = 256


def _mm_swiglu_call(a, wg, wu, widx, adaln=None):
    m, k = a.shape
    n = wg.shape[-1]
    tm, tn = _matmul_tiles(m, k, n, 2, 2, False, m)
    wspec = pl.BlockSpec((None, k, tn), lambda j, i: (widx, 0, j))
    in_specs = [pl.BlockSpec((tm, k), lambda j, i: (i, 0)), wspec, wspec]
    args = [a, wg, wu]
    out_specs = [pl.BlockSpec((tm, tn), lambda j, i: (i, j))]
    out_shape = [jax.ShapeDtypeStruct((m, n), BF16)]
    adaln_steps = 0
    if adaln is not None:
        cond, w_mod, b_mod, layer = adaln
        depth, d, n_mod = w_mod.shape
        rows = cond.shape[0]
        adaln_steps = n_mod // ADALN_TILE
        per_j = m // tm
        assert n_mod % ADALN_TILE == 0 and adaln_steps <= (n // tn) * per_j
        tile = lambda j, i: jnp.minimum(j * per_j + i, adaln_steps - 1)
        in_specs += [pl.BlockSpec((rows, d), lambda j, i: (0, 0)),
                     pl.BlockSpec((None, d, ADALN_TILE), lambda j, i: (layer, 0, tile(j, i))),
                     pl.BlockSpec((None, 1, ADALN_TILE), lambda j, i: (layer, 0, tile(j, i)))]
        args += [cond, w_mod, b_mod.reshape(depth, 1, n_mod)]
        out_specs.append(pl.BlockSpec((rows, ADALN_TILE), lambda j, i: (0, tile(j, i))))
        out_shape.append(jax.ShapeDtypeStruct((rows, n_mod), F32))
    outs = pl.pallas_call(
        functools.partial(_mm_swiglu_kernel, adaln_steps=adaln_steps),
        grid=(n // tn, m // tm),
        in_specs=in_specs,
        out_specs=out_specs,
        out_shape=out_shape,
        scratch_shapes=[pltpu.VMEM((k, tn), BF16), pltpu.VMEM((k, tn), BF16)],
        compiler_params=_params("arbitrary", "arbitrary"),
    )(*args)
    return outs if adaln is not None else outs[0]


def _mm_residual_call(a, w, widx, x, mods, gate_idx, group_of_row, row_quantum, row0=0, m=None):
    k = a.shape[1]
    m = a.shape[0] if m is None else m
    n = w.shape[-1]
    tm, tn = _matmul_tiles(m, k, n, 1, 4, True, row_quantum)
    assert row0 % tm == 0
    rb0 = row0 // tm
    grp = lambda i: group_of_row((i + rb0) * tm)
    return pl.pallas_call(
        _mm_residual_kernel,
        grid=(n // tn, m // tm),
        in_specs=[
            pl.BlockSpec((tm, k), lambda j, i: (i + rb0, 0)),
            pl.BlockSpec((None, k, tn), lambda j, i: (widx, 0, j)),
            pl.BlockSpec((tm, tn), lambda j, i: (i + rb0, j)),
            pl.BlockSpec((None, None, 1, tn), lambda j, i: (grp(i), gate_idx, 0, j)),
        ],
        out_specs=pl.BlockSpec((tm, tn), lambda j, i: (i, j)),
        out_shape=jax.ShapeDtypeStruct((m, n), F32),
        scratch_shapes=[pltpu.VMEM((k, tn), BF16)],
        compiler_params=_params("arbitrary", "arbitrary"),
    )(a, w, x, mods)


def _bcast_row_in_blocks(x, block, r):
    c, d = x.shape
    x3 = x.reshape(c // block, block, d)
    return jnp.broadcast_to(x3[:, r:r + 1, :], x3.shape).reshape(c, d)


MXU_WIDTH = 256


def _gla_blocks(c):
    return [1 << b for b in range((c // 2).bit_length())] + [0]


def _gla_masks(c, fwd):
    i = lax.broadcasted_iota(jnp.int32, (c, LANES), 0)
    lane = lax.broadcasted_iota(jnp.int32, (c, LANES), 1)
    masks = []
    for b, h in enumerate(_gla_blocks(c)):
        j = lane - (b * c) % LANES
        in_block = jnp.where(j >= 0, jnp.where(j < c, 1.0, 0.0), 0.0)
        if h == 0:
            ok = jnp.where(i == j, 1.0, 0.0)
        else:
            same = (i // (2 * h)) == (j // (2 * h))
            i_hi = (i % (2 * h)) >= h
            j_hi = (j % (2 * h)) >= h
            if fwd:
                ok = jnp.where(same, jnp.where(i_hi, jnp.where(j_hi, 0.0, 1.0), 0.0), 0.0)
            else:
                ok = jnp.where(same, jnp.where(i_hi, 0.0, jnp.where(j_hi, 1.0, 0.0)), 0.0)
        masks.append((ok * in_block).astype(F32))
    return masks


def _split_roles(q, kg, h, fwd, rloc):
    c = q.shape[0]
    if h >= SUBLANES:
        parts = []
        for b in range(c // (2 * h)):
            lo = slice(2 * h * b, 2 * h * b + h)
            hi = slice(2 * h * b + h, 2 * h * (b + 1))
            parts += [kg[lo], q[hi]] if fwd else [q[lo], kg[hi]]
        return jnp.concatenate(parts, axis=0)
    in_hi = (rloc % (2 * h)) >= h
    return jnp.where(in_hi, q, kg) if fwd else jnp.where(in_hi, kg, q)


def _split3(x):
    hi = x.astype(BF16)
    r1 = x - hi.astype(F32)
    mid = r1.astype(BF16)
    lo = (r1 - mid.astype(F32)).astype(BF16)
    return jnp.concatenate([hi, mid, lo], axis=-1)


def _gla_chunks(chains, rloc):
    c, d = chains[0][0].shape
    per_group = MXU_WIDTH // c
    blocks = _gla_blocks(c)
    n_groups = -(-len(blocks) // per_group)
    n_kvregs = -(-len(blocks) * c // LANES)

    es = [None] * len(chains)
    for direction in (True, False):
        ids = [n for n, ch in enumerate(chains) if ch[6] == direction]
        if ids:
            e3 = jnp.dot(chains[ids[0]][7], jnp.concatenate([chains[n][4] for n in ids], axis=1),
                         preferred_element_type=F32)
            for pos, n in enumerate(ids):
                base = 3 * d * pos
                es[n] = e3[:, base:base + d] + e3[:, base + d:base + 2 * d] + e3[:, base + 2 * d:base + 3 * d]

    sides = []
    for (q, kg, fg, v, logf, st, fwd, tri_bf, masks), e in zip(chains, es):
        qs, ks = [], []
        for h in blocks:
            if h == 0:
                qs.append(q.astype(BF16))
                ks.append(kg.astype(BF16))
                continue
            if h == 1:
                in_hi = (rloc % 2) >= 1
                w = jnp.where(in_hi, q * fg, kg) if fwd else jnp.where(in_hi, kg, q * fg)
            else:
                r = h - 1 if fwd else h
                if 2 * h >= SUBLANES:
                    e_ref = _bcast_row_in_blocks(e, 2 * h, r)
                else:
                    e_ref = jnp.where(rloc < 2 * h, _bcast_row_in_blocks(e, SUBLANES, r),
                                      _bcast_row_in_blocks(e, SUBLANES, 2 * h + r))
                w = _split_roles(q, kg, h, fwd, rloc) * (1.0 / jnp.exp2(jnp.abs(e - e_ref)))
            qs.append(w.astype(BF16))
            ks.append(qs[-1])
        pad = [jnp.zeros((c, d), BF16)] * (n_groups * per_group - len(blocks))
        sides.append((qs + pad, ks + pad))

    atts = [[None] * n_kvregs for _ in chains]
    for g in range(n_groups):
        for ci, (qs, ks) in enumerate(sides):
            sl = slice(g * per_group, (g + 1) * per_group)
            prod = lax.dot_general(jnp.concatenate(qs[sl], axis=0), jnp.concatenate(ks[sl], axis=0),
                                   NT_DIMS, preferred_element_type=F32)
            masks = chains[ci][8]
            for b in range(g * per_group, min((g + 1) * per_group, len(blocks))):
                t = b * c // LANES
                col = (t * LANES) % MXU_WIDTH
                row = (b - g * per_group) * c
                term = prod[row:row + c, col:col + LANES] * masks[b]
                atts[ci][t] = term if atts[ci][t] is None else atts[ci][t] + term

    outs = []
    for (q, kg, fg, v, logf, st, fwd, tri_bf, masks), e, att in zip(chains, es, atts):
        vb = v.astype(BF16)
        att_sum = functools.reduce(lambda x, y: x + y, att)
        v_rep = jnp.concatenate([vb] * (LANES // c), axis=0)
        o = jnp.dot(att_sum.astype(BF16), v_rep, preferred_element_type=F32)
        o = o + lax.dot_general((q * jnp.exp2(e)).astype(BF16), st.astype(BF16), NT_DIMS,
                                preferred_element_type=F32)
        e_edge = e[c - 1:c] if fwd else e[0:1]
        kh = (kg * jnp.exp2(e_edge - e)).astype(BF16)
        st_new = st * jnp.exp2(e_edge) + lax.dot_general(vb, kh, TN_DIMS, preferred_element_type=F32)
        outs.append((o, st_new))
    return outs


def _gla_kernel(*refs, seq_len, chunk, has_s0, final):
    refs = list(refs)
    q_ref, ffw_ref, fbw_ref, v_ref, g_ref, lb_ref, on_ref = refs[:7]
    del refs[:7]
    s0_ref = refs.pop(0) if has_s0 else None
    del refs[:2 if final == "join" else 1]
    o_ref = refs.pop(0)
    sf_ref = _final_state_slot(refs.pop(0), final) if final is not None else None
    stf_ref, stb_ref, oacc_ref = refs
    c = chunk
    nch = seq_len // c
    half_n = nch // 2
    d = A_HEAD_DIM
    n_h = q_ref.shape[-1] // d

    i = lax.broadcasted_iota(jnp.int32, (c, c), 0)
    j = lax.broadcasted_iota(jnp.int32, (c, c), 1)
    tri_f = jnp.where(i >= j, 1.0, 0.0).astype(BF16)
    tri_b = jnp.where(i <= j, 1.0, 0.0).astype(BF16)
    masks_f = _gla_masks(c, True)
    masks_b = _gla_masks(c, False)
    rloc = lax.broadcasted_iota(jnp.int32, (c, d), 0) % SUBLANES
    q_scale = d ** -0.5

    def gates(x, lb):
        one_m_lb = 1.0 - lb
        t = jnp.exp(-jnp.abs(x))
        r = 1.0 / (1.0 + t)
        pos = x >= 0.0
        fg = lb + one_m_lb * jnp.where(pos, r, t * r)
        kg = one_m_lb * jnp.where(pos, t * r, r)
        log2f = jnp.maximum(jnp.log2(fg),
                            jnp.log2(one_m_lb) + jnp.minimum(x, 0.0) * LOG2_E - jnp.log2(1.0 + t))
        return fg, kg, log2f

    def chain_rows(t, fwd):
        return pl.ds(pl.multiple_of((t if fwd else nch - 1 - t) * c, c), c)

    chain_ids = [(hh, fwd) for hh in range(n_h) for fwd in (True, False)]

    def all_chains(t):
        where, chains = [], []
        for hh, fwd in chain_ids:
            sl, cols = chain_rows(t, fwd), slice(hh * d, (hh + 1) * d)
            fg, kg, log2f = gates((ffw_ref if fwd else fbw_ref)[sl, cols], lb_ref[:, cols])
            where.append((sl, cols))
            chains.append((_silu(q_ref[sl, cols]) * q_scale, kg, fg, v_ref[sl, cols], _split3(log2f),
                           (stf_ref if fwd else stb_ref)[hh], fwd,
                           tri_f if fwd else tri_b, masks_f if fwd else masks_b))
        outs = _gla_chunks(chains, rloc)
        for n, (o, st) in enumerate(outs):
            (stf_ref if n % 2 == 0 else stb_ref)[n // 2] = st
        return [(sl, cols, o) for (sl, cols), (o, st) in zip(where, outs)]

    def finish(sl, cols, o):
        y = oacc_ref[sl, cols] + o
        y = y * lax.rsqrt(jnp.mean(y * y, axis=-1, keepdims=True) + EPS) * on_ref[...]
        o_ref[sl, cols] = (y * _silu(g_ref[sl, cols])).astype(o_ref.dtype)

    def first_half(t, carry):
        for sl, cols, o in all_chains(t):
            oacc_ref[sl, cols] = o
        return carry

    def second_half(t, carry):
        for sl, cols, o in all_chains(t):
            finish(sl, cols, o)
        return carry

    for hh in range(n_h):
        for st_ref, direction in ((stf_ref, 0), (stb_ref, 1)):
            st_ref[hh] = s0_ref[direction, hh].T if has_s0 else jnp.zeros((d, d), F32)
    unroll = 2 if 2 * n_h < 8 and half_n % 2 == 0 else 1
    lax.fori_loop(0, half_n, first_half, 0, unroll=unroll)
    lax.fori_loop(half_n, nch, second_half, 0, unroll=unroll)
    if final is not None:
        for hh in range(n_h):
            sf_ref[0, hh] = stf_ref[hh].T
            sf_ref[1, hh] = stb_ref[hh].T


GLA_CHUNK = 128


def _join_buffers(bufs, in_specs, args):
    shapes, aliases = [], {}
    for n, buf in enumerate(bufs):
        in_specs.append(pl.BlockSpec(memory_space=pl.ANY))
        args.append(buf)
        shapes.append(jax.ShapeDtypeStruct(buf.shape, buf.dtype))
        aliases[len(args) - 1] = n
    return shapes, aliases


def _final_state_specs(finals, slot, dims, heads_per_step, out_specs):
    n_seq, heads, d = dims
    if finals is None:
        return [], None
    if isinstance(finals, int):
        out_specs.append(pl.BlockSpec((None, finals, 2, heads_per_step, d, d),
                                      lambda n, h: (n, 0, 0, h, 0, 0)))
        return [], (slot, finals)
    out_specs.append(pl.BlockSpec((None, None, 2, heads_per_step, d, d),
                                  lambda n, h: (n, slot, 0, h, 0, 0)))
    return [finals], "join"


def _final_state_slot(sf_ref, final):
    if final == "join":
        return sf_ref
    slot, n_slots = final
    for s in range(n_slots):
        if s != slot:
            sf_ref[s] = jnp.zeros(sf_ref.shape[1:], sf_ref.dtype)
    return sf_ref.at[slot]


def _gla_heads_per_step(seq_len):
    for n_h in (8, 4, 2, 1):
        if 2 * 5 * seq_len * n_h * A_HEAD_DIM * 4 + 4 * seq_len * n_h * A_HEAD_DIM * 2 <= VMEM_TILE_BUDGET:
            return n_h
    raise ValueError(f"sequence of {seq_len} rows does not fit VMEM")


def _gla_call(p, lower_bounds, layer, onorm, eidx, s0, n_seq, seq_len, row0, finals, mix):
    d = A_HEAD_DIM
    n_h = _gla_heads_per_step(seq_len)
    w = n_h * d
    a_width = lower_bounds.shape[-1]
    n_hb = a_width // w
    assert row0 % seq_len == 0 and seq_len % (2 * GLA_CHUNK) == 0 and a_width % w == 0
    sb = row0 // seq_len
    slab = lambda k: pl.BlockSpec((seq_len, w), lambda n, h: (n + sb, k * n_hb + h))
    in_specs = [slab(0), slab(1), slab(2), slab(3), slab(4),
                pl.BlockSpec((None, 1, w), lambda n, h: (layer, 0, h)),
                pl.BlockSpec((None, 1, d), lambda n, h: (eidx, 0, 0))]
    args = [p, p, p, p, p, lower_bounds.reshape(lower_bounds.shape[0], 1, a_width),
            onorm.reshape(onorm.shape[0], 1, d)]
    if s0 is not None:
        in_specs.append(pl.BlockSpec((None, None, 2, n_h, d, d), lambda n, h: (n, eidx, 0, h, 0, 0)))
        args.append(s0)
    out_specs = [pl.BlockSpec((seq_len, w), lambda n, h: (n + sb, h))]
    joined, final = _final_state_specs(finals, eidx, (n_seq, a_width // d, d), n_h, out_specs)
    out_shape, aliases = _join_buffers([mix] + joined, in_specs, args)
    if isinstance(final, tuple):
        out_shape.append(jax.ShapeDtypeStruct((n_seq, final[1], 2, a_width // d, d, d), F32))
    kern = functools.partial(_gla_kernel, seq_len=seq_len, chunk=GLA_CHUNK,
                             has_s0=s0 is not None, final=final)
    return pl.pallas_call(
        kern,
        grid=(n_seq, n_hb),
        in_specs=in_specs,
        out_specs=out_specs,
        out_shape=out_shape,
        input_output_aliases=aliases,
        scratch_shapes=[pltpu.VMEM((n_h, d, d), F32), pltpu.VMEM((n_h, d, d), F32),
                        pltpu.VMEM((seq_len, w), F32)],
        compiler_params=_params("arbitrary", "arbitrary"),
    )(*args)


def _ret_kernel(*refs, seq_len, chunk, has_s0, final, rope):
    refs = list(refs)
    q_ref, k_ref, v_ref, g_ref, dl_ref, gn_ref = refs[:6]
    del refs[:6]
    cos_ref = refs.pop(0) if rope else None
    sin_ref = refs.pop(0) if rope else None
    s0_ref = refs.pop(0) if has_s0 else None
    del refs[:2 if final == "join" else 1]
    o_ref = refs.pop(0)
    sf_ref = _final_state_slot(refs.pop(0), final) if final is not None else None
    stf_ref, stb_ref, oacc_ref = refs
    c = chunk
    nch = seq_len // c
    half_n = nch // 2
    d = dl_ref.shape[-1]
    n_h = q_ref.shape[-1] // d
    half = d // 2
    k_scale = d ** -0.5

    dist = (lax.broadcasted_iota(jnp.int32, (c, c), 0)
            - lax.broadcasted_iota(jnp.int32, (c, c), 1)).astype(F32)
    low = dist >= 0.0
    up = dist <= 0.0
    row = lax.broadcasted_iota(jnp.int32, (c, d), 0).astype(F32)

    def decay_tables(hh):
        lg_f = _log_sigmoid(dl_ref[0, hh])
        lg_b = _log_sigmoid(dl_ref[1, hh])
        intra = (jnp.where(low, jnp.exp(jnp.where(low, dist, 0.0) * lg_f[:, :c]), 0.0)
                 + jnp.where(up, jnp.exp(jnp.where(up, -dist, 0.0) * lg_b[:, :c]), 0.0))
        fwd = (jnp.exp((row + 1.0) * lg_f), jnp.exp((c - 1.0 - row) * lg_f), jnp.exp(c * lg_f))
        bwd = (jnp.exp((c - row) * lg_b), jnp.exp(row * lg_b), jnp.exp(c * lg_b))
        return intra, fwd, bwd

    tables = [decay_tables(hh) for hh in range(n_h)]

    def rotate(x, sl):
        if not rope:
            return x
        cos, sin = cos_ref[sl, :], sin_ref[sl, :]
        x1, x2 = x[:, :half], x[:, half:]
        return jnp.concatenate([x1 * cos - x2 * sin, x1 * sin + x2 * cos], axis=-1)

    def load(ci, hh):
        sl, cols = pl.ds(pl.multiple_of(ci * c, c), c), slice(hh * d, (hh + 1) * d)
        return (sl, cols, rotate(q_ref[sl, cols], sl), rotate(k_ref[sl, cols] * k_scale, sl),
                v_ref[sl, cols].astype(BF16))

    def carried(q, k, vb, st_ref, hh, decs):
        q_dec, k_dec, c_dec = decs
        st = st_ref[hh]
        o = jnp.dot((q * q_dec).astype(BF16), st.astype(BF16), preferred_element_type=F32)
        st_ref[hh] = c_dec * st + lax.dot_general((k * k_dec).astype(BF16), vb, TN_DIMS,
                                                  preferred_element_type=F32)
        return o

    def trip(t):
        left = [load(t, hh) for hh in range(n_h)]
        right = [load(nch - 1 - t, hh) for hh in range(n_h)]
        scores = [lax.dot_general(q.astype(BF16), k.astype(BF16), NT_DIMS, preferred_element_type=F32)
                  for _, _, q, k, _ in left]
        o_left = [carried(q, k, vb, stf_ref, hh, tables[hh][1]) for hh, (_, _, q, k, vb) in enumerate(left)]
        o_right = [carried(q, k, vb, stb_ref, hh, tables[hh][2]) for hh, (_, _, q, k, vb) in enumerate(right)]
        for hh, (_, _, _, _, vb) in enumerate(left):
            o_left[hh] = o_left[hh] + jnp.dot((scores[hh] * tables[hh][0]).astype(BF16), vb,
                                              preferred_element_type=F32)
        return ([(sl, cols, o) for (sl, cols, *_), o in zip(left, o_left)]
                + [(sl, cols, o) for (sl, cols, *_), o in zip(right, o_right)])

    def finish(sl, cols, o):
        y = oacc_ref[sl, cols] + o
        yc = y - jnp.mean(y, axis=-1, keepdims=True)
        yn = yc * lax.rsqrt(jnp.mean(yc * yc, axis=-1, keepdims=True) + EPS)
        o_ref[sl, cols] = (yn * gn_ref[:, cols] * _silu(g_ref[sl, cols])).astype(o_ref.dtype)

    def first_half(t, carry):
        for sl, cols, o in trip(t):
            oacc_ref[sl, cols] = o
        return carry

    def second_half(t, carry):
        for sl, cols, o in trip(t):
            finish(sl, cols, o)
        return carry

    for hh in range(n_h):
        for st_ref, direction in ((stf_ref, 0), (stb_ref, 1)):
            st_ref[hh] = s0_ref[direction, hh] if has_s0 else jnp.zeros((d, d), F32)
    lax.fori_loop(0, half_n, first_half, 0)
    lax.fori_loop(half_n, nch, second_half, 0)
    if final is not None:
        for hh in range(n_h):
            sf_ref[0, hh] = stf_ref[hh]
            sf_ref[1, hh] = stb_ref[hh]


def _ret_call(p, col0, decay_logit, gnorm, eidx, rope_tabs, s0, n_seq, seq_len, row0, finals,
              mix, mix_col0):
    n_heads = decay_logit.shape[-1]
    b_width = gnorm.shape[-1]
    d = b_width // n_heads
    chunk = min(d, seq_len // 2)
    n_h = next(c for c in (4, 2, 1)
               if n_heads % c == 0 and (2 * 4 * 4 + 4 + 2 * 2) * seq_len * c * d <= VMEM_TILE_BUDGET)
    w = n_h * d
    n_hb = n_heads // n_h
    assert row0 % seq_len == 0 and seq_len % (2 * chunk) == 0 and col0 % w == 0 and chunk <= d
    assert mix_col0 % w == 0
    sb = row0 // seq_len
    cb0 = col0 // w
    ocb0 = mix_col0 // w
    slab = lambda k: pl.BlockSpec((seq_len, w), lambda n, h: (n + sb, cb0 + k * n_hb + h))
    dl = jnp.broadcast_to(decay_logit[:, :, :, None, None], decay_logit.shape + (1, d))
    in_specs = [slab(0), slab(1), slab(2), slab(3),
                pl.BlockSpec((None, 2, n_h, 1, d), lambda n, h: (eidx, 0, h, 0, 0)),
                pl.BlockSpec((None, 1, w), lambda n, h: (eidx, 0, h))]
    args = [p, p, p, p, dl, gnorm.reshape(gnorm.shape[0], 1, b_width)]
    if rope_tabs is not None:
        tab = pl.BlockSpec((seq_len, d // 2), lambda n, h: (0, 0))
        in_specs += [tab, tab]
        args += list(rope_tabs)
    if s0 is not None:
        in_specs.append(pl.BlockSpec((None, None, 2, n_h, d, d), lambda n, h: (n, eidx, 0, h, 0, 0)))
        args.append(s0)
    out_specs = [pl.BlockSpec((seq_len, w), lambda n, h: (n + sb, ocb0 + h))]
    joined, final = _final_state_specs(finals, eidx, (n_seq, n_heads, d), n_h, out_specs)
    out_shape, aliases = _join_buffers([mix] + joined, in_specs, args)
    if isinstance(final, tuple):
        out_shape.append(jax.ShapeDtypeStruct((n_seq, final[1], 2, n_heads, d, d), F32))
    kern = functools.partial(_ret_kernel, seq_len=seq_len, chunk=chunk, has_s0=s0 is not None,
                             final=final, rope=rope_tabs is not None)
    return pl.pallas_call(
        kern,
        grid=(n_seq, n_hb),
        in_specs=in_specs,
        out_specs=out_specs,
        out_shape=out_shape,
        input_output_aliases=aliases,
        scratch_shapes=[pltpu.VMEM((n_h, d, d), F32), pltpu.VMEM((n_h, d, d), F32),
                        pltpu.VMEM((seq_len, w), F32)],
        compiler_params=_params("arbitrary", "arbitrary"),
    )(*args)


Q_PRESCALE = ATTN_HEAD_DIM ** -0.5 * LOG2_E


def _head_norm(x, gain):
    return x * lax.rsqrt(jnp.mean(x * x, axis=-1, keepdims=True) + EPS) * gain


def _rotate(y, cos, sin):
    return y * cos + pltpu.roll(y, y.shape[-1] // 2, 1) * sin


def _mm_qproj_kernel(a_ref, w_ref, gain_ref, cos_ref, sin_ref, o_ref, wbf_ref):
    _cast_weights_once((w_ref,), (wbf_ref,))
    y = jnp.dot(a_ref[...], wbf_ref[...], preferred_element_type=F32)
    hd = ATTN_HEAD_DIM
    for h in range(y.shape[-1] // hd):
        cols = slice(h * hd, (h + 1) * hd)
        z = _rotate(_head_norm(y[:, cols], gain_ref[...]), cos_ref[...], sin_ref[...])
        o_ref[:, cols] = (z * Q_PRESCALE).astype(BF16)


def _mm_qproj_call(a, w, widx, n_cols, q_norm, rope_rows):
    m, k = a.shape
    hd = ATTN_HEAD_DIM
    tm, tn = _matmul_tiles(m, k, n_cols, 1, 2, False, m)
    tab = pl.BlockSpec((tm, hd), lambda j, i: (i, 0))
    return pl.pallas_call(
        _mm_qproj_kernel,
        grid=(n_cols // tn, m // tm),
        in_specs=[
            pl.BlockSpec((tm, k), lambda j, i: (i, 0)),
            pl.BlockSpec((None, k, tn), lambda j, i: (widx, 0, j)),
            pl.BlockSpec((None, 1, hd), lambda j, i: (widx, 0, 0)),
            tab, tab,
        ],
        out_specs=pl.BlockSpec((tm, tn), lambda j, i: (i, j)),
        out_shape=jax.ShapeDtypeStruct((m, n_cols), BF16),
        scratch_shapes=[pltpu.VMEM((k, tn), BF16)],
        compiler_params=_params("arbitrary", "arbitrary"),
    )(a, w, q_norm.reshape(-1, 1, hd), *rope_rows)


def _mm_kvproj_kernel(*refs, cache, cache_tiles):
    refs = list(refs)
    a_ref, w_ref, gain_ref, cos_ref, sin_ref = refs[:5]
    del refs[:7 if cache == "join" else 5]
    kf_out, vf_out, k_out, v_out, wbf_ref = refs
    _cast_weights_once((w_ref,), (wbf_ref,))
    y = jnp.dot(a_ref[...], wbf_ref[...], preferred_element_type=F32)
    hd = ATTN_HEAD_DIM
    kvw = y.shape[-1] // 2
    keys = []
    for h in range(kvw // hd):
        cols = slice(h * hd, (h + 1) * hd)
        keys.append(_rotate(_head_norm(y[:, cols], gain_ref[...]), cos_ref[...], sin_ref[...]))
        k_out[:, cols] = keys[-1].astype(BF16)
    v = y[:, kvw:]
    v_out[...] = v.astype(BF16)

    @pl.when(pl.program_id(1) < cache_tiles)
    def _():
        kf, vf = kf_out, vf_out
        if cache != "join":
            slot, n_slots = cache
            for s in range(n_slots):
                if s != slot:
                    kf_out[:, s] = jnp.zeros(kf_out.shape[:1] + kf_out.shape[2:], F32)
                    vf_out[:, s] = jnp.zeros(vf_out.shape[:1] + vf_out.shape[2:], F32)
            kf, vf = kf_out.at[:, slot], vf_out.at[:, slot]
        seqs, seq_len = kf.shape[0], kf.shape[1]
        for h, z in enumerate(keys):
            kf[:, :, h * hd:(h + 1) * hd] = z.reshape(seqs, seq_len, hd)
        vf[...] = v.reshape(seqs, seq_len, kvw)


def _mm_kvproj_call(a, w, widx, col0, k_norm, rope_rows, cache_seqs, cache_seq_len, caches):
    m, k = a.shape
    hd = ATTN_HEAD_DIM
    kvw = ATTN_KV_HEADS * hd
    tn = 2 * kvw
    assert col0 % tn == 0
    tm, _ = _matmul_tiles(m, k, tn, 1, 2 + 4, False, m)
    assert tm % cache_seq_len == 0 and (cache_seqs * cache_seq_len) % tm == 0
    seqs_per_tile = tm // cache_seq_len
    cache_tiles = cache_seqs * cache_seq_len // tm
    tab = pl.BlockSpec((tm, hd), lambda j, i: (i, 0))
    row_spec = pl.BlockSpec((tm, kvw), lambda j, i: (i, 0))
    in_specs = [pl.BlockSpec((tm, k), lambda j, i: (i, 0)),
                pl.BlockSpec((None, k, tn), lambda j, i: (widx, 0, col0 // tn)),
                pl.BlockSpec((None, 1, hd), lambda j, i: (widx, 0, 0)),
                tab, tab]
    args = [a, w, k_norm.reshape(-1, 1, hd), *rope_rows]
    held = lambda i: jnp.minimum(i, cache_tiles - 1)
    if isinstance(caches, int):
        cache = (widx, caches)
        cache_spec = pl.BlockSpec((seqs_per_tile, caches, cache_seq_len, kvw),
                                  lambda j, i: (held(i), 0, 0, 0))
        out_shape = [jax.ShapeDtypeStruct((cache_seqs, caches, cache_seq_len, kvw), F32)] * 2
        aliases = {}
    else:
        cache = "join"
        cache_spec = pl.BlockSpec((seqs_per_tile, None, cache_seq_len, kvw),
                                  lambda j, i: (held(i), widx, 0, 0))
        out_shape, aliases = _join_buffers(list(caches), in_specs, args)
    return pl.pallas_call(
        functools.partial(_mm_kvproj_kernel, cache=cache, cache_tiles=cache_tiles),
        grid=(1, m // tm),
        in_specs=in_specs,
        out_specs=[cache_spec, cache_spec, row_spec, row_spec],
        out_shape=out_shape + [jax.ShapeDtypeStruct((m, kvw), BF16)] * 2,
        input_output_aliases=aliases,
        scratch_shapes=[pltpu.VMEM((k, tn), BF16)],
        compiler_params=_params("arbitrary", "arbitrary"),
    )(*args)


def _attn_kernel(*refs, group):
    q_ref, k_ref, v_ref, _, o_ref = refs
    hd = ATTN_HEAD_DIM
    for kh in range(k_ref.shape[-1] // hd):
        k = k_ref[:, kh * hd:(kh + 1) * hd]
        v = v_ref[:, kh * hd:(kh + 1) * hd]
        v_ext = jnp.concatenate([v, jnp.ones_like(v)], axis=1)
        for g in range(group):
            cols = slice((kh * group + g) * hd, (kh * group + g + 1) * hd)
            s = lax.dot_general(q_ref[:, cols], k, NT_DIMS, preferred_element_type=F32)
            p = jnp.exp2(s - jnp.max(s, axis=-1, keepdims=True))
            o = jnp.dot(p.astype(BF16), v_ext, preferred_element_type=F32)
            o_ref[:, cols] = (o[:, :hd] / o[:, hd:]).astype(o_ref.dtype)


ATTN_SCORE_BUDGET = 4 * 1024 * 1024


ATTN_Q_ROWS = 1024


def _attn_call(q, k, v, n_seq, seq_len, row0, mix):
    kvw = k.shape[-1]
    qw = q.shape[-1]
    hd = ATTN_HEAD_DIM
    n_kv = kvw // hd
    group = qw // hd // n_kv
    tq = min(ATTN_Q_ROWS, seq_len)
    assert row0 % tq == 0 and seq_len % tq == 0 and row0 % seq_len == 0
    per_seq = seq_len // tq
    rb0 = row0 // tq
    lk = seq_len if k.ndim == 2 else k.shape[1]
    kv_step = n_kv if tq * lk * 4 * n_kv <= ATTN_SCORE_BUDGET else 1
    q_spec = pl.BlockSpec((tq, kv_step * group * hd), lambda n, kh, i: (rb0 + n * per_seq + i, kh))
    if k.ndim == 2:
        kv_spec = pl.BlockSpec((seq_len, kv_step * hd), lambda n, kh, i: (row0 // seq_len + n, kh))
    else:
        kv_spec = pl.BlockSpec((None, lk, kv_step * hd), lambda n, kh, i: (n, 0, kh))
    in_specs = [q_spec, kv_spec, kv_spec]
    args = [q, k, v]
    (mix_shape,), aliases = _join_buffers([mix], in_specs, args)
    return pl.pallas_call(
        functools.partial(_attn_kernel, group=group),
        grid=(n_seq, n_kv // kv_step, per_seq),
        in_specs=in_specs,
        out_specs=q_spec,
        out_shape=mix_shape,
        input_output_aliases=aliases,
        compiler_params=_params("arbitrary", "arbitrary", "arbitrary"),
    )(*args)


def _grid_rope(n_tokens, head_dim):
    rows = n_tokens // GRID_W
    row = jnp.broadcast_to(jnp.arange(rows)[:, None], (rows, GRID_W)).reshape(-1).astype(F32)
    col = jnp.broadcast_to(jnp.arange(GRID_W)[None, :], (rows, GRID_W)).reshape(-1).astype(F32)
    per_axis = head_dim // 4
    freqs = ROPE_THETA ** (-jnp.arange(per_axis, dtype=F32) / per_axis)
    ang = jnp.concatenate([row[:, None] * freqs, col[:, None] * freqs], axis=-1)
    return jnp.cos(ang), jnp.sin(ang)


def kernel(x_prompt, x_sample, c, state_hgrn, state_ret, cache_k, cache_v, c_ctx, w_mod, b_mod, norm_mix, norm_ffn, w_in_even, w_out_even, hgrn_lb_logits, hgrn_onorm, ret_decay_logit, ret_gnorm, w_in_attn, w_out_attn, q_norm, k_norm, w_ffn_gate, w_ffn_up, w_ffn_down):
    n_p, l_p, d = x_prompt.shape
    n_s, l_s, _ = x_sample.shape
    depth = w_mod.shape[0]
    rows_p = n_p * l_p
    rows_s = n_s * l_s
    a_width = hgrn_lb_logits.shape[-1]
    b_width = ret_gnorm.shape[-1]
    kv_width = ATTN_KV_HEADS * ATTN_HEAD_DIM
    row_quantum = math.gcd(rows_p, l_s)

    def group_of_row(r):
        return jnp.where(r < rows_p, 0, 1 + (r - rows_p) // l_s)

    cond = jnp.concatenate([c_ctx[None, :], c, jnp.zeros((-(1 + n_s) % SUBLANES, d), F32)], axis=0)
    as_mods = lambda m: m.reshape(cond.shape[0], 6, 1, d)
    mods = as_mods(_adaln_call(cond, w_mod, b_mod, 0))

    cos_a, sin_a = _grid_rope(l_s, ATTN_HEAD_DIM)
    rope_attn = (jnp.concatenate([cos_a, cos_a], axis=-1), jnp.concatenate([-sin_a, sin_a], axis=-1))
    rope_rows = (jnp.concatenate([jnp.ones((rows_p, ATTN_HEAD_DIM), F32), jnp.tile(rope_attn[0], (n_s, 1))]),
                 jnp.concatenate([jnp.zeros((rows_p, ATTN_HEAD_DIM), F32), jnp.tile(rope_attn[1], (n_s, 1))]))
    rope_ret = _grid_rope(l_s, b_width // B_HEADS)
    lb_cum = jnp.cumsum(jax.nn.softmax(hgrn_lb_logits.astype(F32), axis=0), axis=0)
    lower_bounds = lb_cum - lb_cum[0]

    n_even = w_in_even.shape[0]
    out_hgrn, out_ret = n_even, n_even
    out_kv = w_in_attn.shape[0]
    for l in range(depth):
        if l == 0:
            mix, x = _modulate_join_call(x_prompt.reshape(rows_p, d), x_sample.reshape(rows_s, d),
                                         norm_mix, mods, l, 0, group_of_row)
        else:
            mix = _modulate_call(x, norm_mix, mods, l, 0, group_of_row)
        if l % 2 == 0:
            e = l // 2
            p = _mm_plain_call(mix, w_in_even, e, F32)
            mix, out_hgrn = _gla_call(p, lower_bounds, l, hgrn_onorm, e, None, n_p, l_p, 0, out_hgrn, mix)
            (mix,) = _gla_call(p, lower_bounds, l, hgrn_onorm, e, state_hgrn, n_s, l_s, rows_p, None, mix)
            mix, out_ret = _ret_call(p, 5 * a_width, ret_decay_logit, ret_gnorm, e, None, None,
                                     n_p, l_p, 0, out_ret, mix, a_width)
            (mix,) = _ret_call(p, 5 * a_width, ret_decay_logit, ret_gnorm, e, rope_ret, state_ret,
                               n_s, l_s, rows_p, None, mix, a_width)
            w_out, widx = w_out_even, e
        else:
            o = l // 2
            q = _mm_qproj_call(mix, w_in_attn, o, d, q_norm, rope_rows)
            *out_kv, k, v = _mm_kvproj_call(mix, w_in_attn, o, d, k_norm, rope_rows, n_p, l_p, out_kv)
            past = cache_k.shape[2]
            with_cache = lambda new, cache: jnp.concatenate(
                [new[rows_p:].reshape(n_s, l_s, kv_width),
                 cache[:, o].reshape(n_s, past, kv_width).astype(BF16)], axis=1)
            k_all, v_all = with_cache(k, cache_k), with_cache(v, cache_v)
            mix = _attn_call(q, k, v, n_p, l_p, 0, mix)
            mix = _attn_call(q, k_all, v_all, n_s, l_s, rows_p, mix)
            w_out, widx = w_out_attn, o
        x, h = _mm_residual_modulate_call(mix, w_out, widx, x, mods, l, 2, norm_ffn, 3, group_of_row,
                                          row_quantum)
        if l < depth - 1:
            u, next_mods = _mm_swiglu_call(h, w_ffn_gate, w_ffn_up, l, (cond, w_mod, b_mod, l + 1))
        else:
            u = _mm_swiglu_call(h, w_ffn_gate, w_ffn_up, l)
        down = functools.partial(_mm_residual_call, u, w_ffn_down, l, x, mods, 5, group_of_row,
                                 row_quantum)
        if l < depth - 1:
            x = down()
            mods = as_mods(next_mods)
    return (down(0, rows_p).reshape(n_p, l_p, d), down(rows_p, rows_s).reshape(n_s, l_s, d),
            out_hgrn, out_ret,
            *(t.reshape(n_p, -1, l_p, ATTN_KV_HEADS, ATTN_HEAD_DIM) for t in out_kv))
```
